```python
import math
import jax
import jax.numpy as jnp
from jax import lax
import numpy as np


D_MODEL = 2048
BATCH = 4
SEQ = 2048
DEPTH = 1

CHUNK = 64
MIX_WIDTH = D_MODEL
GLA_HEADS = 4
GLA_DV = MIX_WIDTH // 2 // GLA_HEADS
GLA_DK = GLA_DV // 2
GLA_GATE_RANK = 16
GLA_TAU = 16.0
RET_HEADS = 4
RET_DV = MIX_WIDTH // 2 // RET_HEADS
RET_DK = RET_DV
ROPE_BASE = 10000.0
PEER_HEADS = 8
PEER_NKEYS = 128
PEER_N = PEER_NKEYS * PEER_NKEYS
PEER_QDIM = 256
PEER_TOPK = 16
PEER_TOKEN_BLOCK = 128
LN_EPS = 1e-5
DEEPNORM_ALPHA = (2 * DEPTH) ** 0.25
DEEPNORM_BETA = (8 * DEPTH) ** -0.25
PROJ_SIZES = (GLA_HEADS * GLA_DK, GLA_HEADS * GLA_DK, GLA_HEADS * GLA_DV, GLA_HEADS * GLA_DV, GLA_GATE_RANK,
              RET_HEADS * RET_DK, RET_HEADS * RET_DK, RET_HEADS * RET_DV, RET_HEADS * RET_DV)
PROJ_COLS = sum(PROJ_SIZES)
VALUE_SLOTS = (2, 7)

kernel_name = "hybrid_gla_retention_peer_block"


def _layer_norm(x, g, b):
    xf = x.astype(jnp.float32)
    mu = jnp.mean(xf, -1, keepdims=True)
    var = jnp.mean(jnp.square(xf - mu), -1, keepdims=True)
    return ((xf - mu) * lax.rsqrt(var + LN_EPS) * g + b).astype(x.dtype)


def _head_norm(o, gain, center):
    of = o.astype(jnp.float32)
    if center:
        of = of - jnp.mean(of, -1, keepdims=True)
    y = of * lax.rsqrt(jnp.mean(jnp.square(of), -1, keepdims=True) + LN_EPS)
    return y.reshape(o.shape[0], o.shape[1], -1) * gain


def _to_chunks(t):
    b, s, h, d = t.shape
    return t.reshape(b, s // CHUNK, CHUNK, h, d).transpose(1, 0, 3, 2, 4)


def _from_chunks(t):
    nc, b, h, c, d = t.shape
    return t.transpose(1, 0, 3, 2, 4).reshape(b, nc * c, h, d)


def _scan_states(decay, u):
    def step(s, inp):
        d_i, u_i = inp
        return d_i * s + u_i, s
    _, s_prev = lax.scan(step, jnp.zeros(u.shape[1:], u.dtype), (decay, u))
    return s_prev


def _rotary(t, pos):
    half = t.shape[-1] // 2
    inv = ROPE_BASE ** (-jnp.arange(half, dtype=jnp.float32) / half)
    ang = pos.astype(jnp.float32)[:, None] * inv[None, :]
    cos = jnp.cos(ang)[None, :, None, :]
    sin = jnp.sin(ang)[None, :, None, :]
    t1, t2 = t[..., :half], t[..., half:]
    return jnp.concatenate([t1 * cos - t2 * sin, t1 * sin + t2 * cos], -1).astype(t.dtype)


def _gla_chunk_causal(q, k, v, log_a):
    dk = q.shape[-1]
    qc = _to_chunks(q * dk ** -0.5)
    kc = _to_chunks(k)
    vc = _to_chunks(v)
    bc = jnp.cumsum(_to_chunks(log_a), axis=3)
    b_last = bc[:, :, :, -1:, :]
    u = jnp.einsum('nbhcd,nbhce->nbhde', kc * jnp.exp(b_last - bc), vc).astype(jnp.float32)
    s_prev = _scan_states(jnp.exp(b_last[:, :, :, 0, :])[..., None], u)
    o_inter = jnp.einsum('nbhcd,nbhde->nbhce', qc * jnp.exp(bc), s_prev)

    def intra(args):
        q_i, k_i, v_i, b_i = args
        dec = jnp.exp(-jnp.abs(b_i[:, :, :, None, :] - b_i[:, :, None, :, :]))
        a = jnp.einsum('bhnd,bhmd,bhnmd->bhnm', q_i, k_i, dec)
        return jnp.einsum('bhnm,bhme->bhne', a, v_i)

    o_intra = lax.map(intra, (qc, kc, vc, bc))
    return _from_chunks(o_inter + o_intra)


def _retention_chunk_causal(q, k, v):
    dk = q.shape[-1]
    h = q.shape[2]
    log_gamma = jnp.log(1.0 - 2.0 ** (-5.0 - jnp.arange(h, dtype=jnp.float32)))
    idx = jnp.arange(CHUNK, dtype=jnp.float32)
    qc = _to_chunks(q)
    kc = _to_chunks(k * dk ** -0.5)
    vc = _to_chunks(v)
    k_dec = jnp.exp(log_gamma[:, None] * (CHUNK - 1.0 - idx)[None, :])[None, None, :, :, None]
    q_dec = jnp.exp(log_gamma[:, None] * (idx + 1.0)[None, :])[None, None, :, :, None]
    d_intra = jnp.exp(log_gamma[:, None, None] * jnp.abs(idx[:, None] - idx[None, :]))
    u = jnp.einsum('nbhcd,nbhce->nbhde', kc * k_dec, vc).astype(jnp.float32)
    nc = qc.shape[0]
    chunk_decay = jnp.broadcast_to(jnp.exp(log_gamma * CHUNK)[None, None, :, None, None], (nc, 1, h, 1, 1))
    s_prev = _scan_states(chunk_decay, u)
    o_inter = jnp.einsum('nbhcd,nbhde->nbhce', qc, s_prev) * q_dec
    a = jnp.einsum('nbhcd,nbhmd->nbhcm', qc, kc) * d_intra
    o_intra = jnp.einsum('nbhcm,nbhme->nbhce', a, vc)
    return _from_chunks(o_inter + o_intra)


def _hybrid_mixer(h, w_in, w_gate_up, b_gate, gla_g, ret_g, w_out):
    b, s, _ = h.shape
    proj = jnp.einsum('bsd,dn->bsn', h, w_in)
    splits = np.cumsum(PROJ_SIZES)[:-1].tolist()
    gq, gk, gv, gr, glr, rq, rk, rv, rr = jnp.split(proj, splits, axis=-1)
    log_a = jax.nn.log_sigmoid((glr @ w_gate_up + b_gate).astype(jnp.float32)) / GLA_TAU
    o_gla = _gla_chunk_causal(gq.reshape(b, s, GLA_HEADS, GLA_DK), gk.reshape(b, s, GLA_HEADS, GLA_DK),
                              gv.reshape(b, s, GLA_HEADS, GLA_DV), log_a.reshape(b, s, GLA_HEADS, GLA_DK))
    o_gla = _head_norm(o_gla, gla_g, False) * jax.nn.silu(gr)
    pos = jnp.arange(s)
    o_ret = _retention_chunk_causal(_rotary(rq.reshape(b, s, RET_HEADS, RET_DK), pos),
                                    _rotary(rk.reshape(b, s, RET_HEADS, RET_DK), pos),
                                    rv.reshape(b, s, RET_HEADS, RET_DV))
    o_ret = _head_norm(o_ret, ret_g, True) * jax.nn.silu(rr)
    out = jnp.concatenate([o_gla, o_ret], axis=-1) @ w_out
    return out.astype(h.dtype)


def _peer(h, w_q, subkeys, u_tab, v_tab):
    b, s, d = h.shape
    t = b * s
    tb = PEER_TOKEN_BLOCK
    xt = h.reshape(t, d)
    q = (xt @ w_q).reshape(t, PEER_HEADS, 2, PEER_QDIM // 2)
    scores = jnp.einsum('thpd,hpkd->thpk', q, subkeys).astype(jnp.float32)
    vals, idx = lax.top_k(scores, PEER_TOPK)
    cand_s = (vals[:, :, 0, :, None] + vals[:, :, 1, None, :]).reshape(t, PEER_HEADS, -1)
    cand_i = (idx[:, :, 0, :, None] * PEER_NKEYS + idx[:, :, 1, None, :]).reshape(t, PEER_HEADS, -1)
    top_s, sel = lax.top_k(cand_s, PEER_TOPK)
    experts = jnp.take_along_axis(cand_i, sel, axis=-1).reshape(t // tb, tb, -1)
    gates = jax.nn.softmax(top_s, axis=-1).reshape(t // tb, tb, -1)

    def block(args):
        xb, eb, gb = args
        act = jax.nn.gelu(jnp.einsum('tkd,td->tk', u_tab[eb], xb).astype(jnp.float32), approximate=False)
        return jnp.einsum('tk,tkd->td', (gb * act).astype(xb.dtype), v_tab[eb])

    out = lax.map(block, (xt.reshape(t // tb, tb, d), experts, gates))
    return out.reshape(b, s, d).astype(h.dtype)


def setup_inputs(seed: int = 0) -> dict:
    key = jax.random.key(seed)
    ks = jax.random.split(key, 16)
    f32 = jnp.float32

    def nrm(k, shape, scale):
        return jax.random.normal(k, shape, f32) * scale

    col_scale = jnp.concatenate([jnp.full((n,), DEEPNORM_BETA if i in VALUE_SLOTS else 1.0, f32)
                                 for i, n in enumerate(PROJ_SIZES)])
    return {
        "x": nrm(ks[0], (BATCH, SEQ, D_MODEL), 1.0),
        "w_in": nrm(ks[1], (DEPTH, D_MODEL, PROJ_COLS), D_MODEL ** -0.5) * col_scale,
        "w_gla_gate_up": nrm(ks[2], (DEPTH, GLA_GATE_RANK, GLA_HEADS * GLA_DK), GLA_GATE_RANK ** -0.5),
        "b_gla_gate": nrm(ks[3], (DEPTH, GLA_HEADS * GLA_DK), 0.1),
        "gla_norm_g": 1.0 + nrm(ks[4], (DEPTH, GLA_HEADS * GLA_DV), 0.02),
        "ret_norm_g": 1.0 + nrm(ks[5], (DEPTH, RET_HEADS * RET_DV), 0.02),
        "w_out": nrm(ks[6], (DEPTH, MIX_WIDTH, D_MODEL), MIX_WIDTH ** -0.5 * DEEPNORM_BETA),
        "ln1_g": 1.0 + nrm(ks[7], (DEPTH, D_MODEL), 0.02),
        "ln1_b": nrm(ks[8], (DEPTH, D_MODEL), 0.02),
        "w_peer_q": nrm(ks[9], (DEPTH, D_MODEL, PEER_HEADS * PEER_QDIM), D_MODEL ** -0.5),
        "peer_subkeys": nrm(ks[10], (DEPTH, PEER_HEADS, 2, PEER_NKEYS, PEER_QDIM // 2), (PEER_QDIM // 2) ** -0.5),
        "peer_u": nrm(ks[11], (DEPTH, PEER_N, D_MODEL), D_MODEL ** -0.5),
        "peer_v": nrm(ks[12], (DEPTH, PEER_N, D_MODEL), DEEPNORM_BETA),
        "ln2_g": 1.0 + nrm(ks[13], (DEPTH, D_MODEL), 0.02),
        "ln2_b": nrm(ks[14], (DEPTH, D_MODEL), 0.02),
    }


def reference(x, w_in, w_gla_gate_up, b_gla_gate, gla_norm_g, ret_norm_g, w_out, ln1_g, ln1_b,
              w_peer_q, peer_subkeys, peer_u, peer_v, ln2_g, ln2_b):
    h = x
    for l in range(DEPTH):
        mix = _hybrid_mixer(h, w_in[l], w_gla_gate_up[l], b_gla_gate[l], gla_norm_g[l], ret_norm_g[l], w_out[l])
        h = _layer_norm(DEEPNORM_ALPHA * h + mix, ln1_g[l], ln1_b[l])
        ffn = _peer(h, w_peer_q[l], peer_subkeys[l], peer_u[l], peer_v[l])
        h = _layer_norm(DEEPNORM_ALPHA * h + ffn, ln2_g[l], ln2_b[l])
    return h
```

```python
import functools
import math

import jax
import jax.numpy as jnp
from jax import lax
from jax.experimental import pallas as pl
from jax.experimental.pallas import tpu as pltpu

F32 = jnp.float32
BF16 = jnp.bfloat16

DEPTH = 1
CHUNK = 64
GLA_HEADS = 4
GLA_DK = 128
GLA_DV = 256
GLA_GATE_RANK = 16
GLA_TAU = 16.0
RET_HEADS = 4
RET_DK = 256
RET_DV = 256
ROPE_BASE = 10000.0
PEER_HEADS = 8
PEER_NKEYS = 128
PEER_TOPK = 16
LN_EPS = 1e-5
DEEPNORM_ALPHA = (2 * DEPTH) ** 0.25

LANES = 128
SUBLANES = 8
VMEM_LIMIT = 56 * 1024 * 1024

_C_GQ = 0
_C_GK = _C_GQ + GLA_HEADS * GLA_DK
_C_GV = _C_GK + GLA_HEADS * GLA_DK
_C_GR = _C_GV + GLA_HEADS * GLA_DV
_C_RQ = _C_GR + GLA_HEADS * GLA_DV
_C_RK = _C_RQ + RET_HEADS * RET_DK
_C_RV = _C_RK + RET_HEADS * RET_DK
_C_RR = _C_RV + RET_HEADS * RET_DV
_C_END = _C_RR + RET_HEADS * RET_DV

_NT = (((1,), (1,)), ((), ()))
_TN = (((0,), (0,)), ((), ()))


def _params(*sem):
    return pltpu.CompilerParams(dimension_semantics=sem, vmem_limit_bytes=VMEM_LIMIT)


def _mm_kernel(x_ref, w_ref, o_ref):
    o_ref[...] = jnp.dot(x_ref[...], w_ref[...], preferred_element_type=F32).astype(o_ref.dtype)


def _matmul(x, w, out_dtype, tm, tn):
    m, k = x.shape
    n = w.shape[1]
    return pl.pallas_call(
        _mm_kernel,
        grid=(m // tm, n // tn),
        in_specs=[pl.BlockSpec((tm, k), lambda i, j: (i, 0)),
                  pl.BlockSpec((k, tn), lambda i, j: (0, j))],
        out_specs=pl.BlockSpec((tm, tn), lambda i, j: (i, j)),
        out_shape=jax.ShapeDtypeStruct((m, n), out_dtype),
        compiler_params=_params("parallel", "parallel"),
        name="proj_matmul",
    )(x, w)


def _gate_kernel(x_ref, wl_ref, wu_ref, b_ref, o_ref):
    tm = x_ref.shape[0]
    glr = jnp.dot(x_ref[...], wl_ref[...], preferred_element_type=F32)
    z = jnp.dot(glr.astype(BF16), wu_ref[...], preferred_element_type=F32) + b_ref[...]
    log_a = (jnp.minimum(z, 0.0) - jnp.log1p(jnp.exp(-jnp.abs(z)))) / GLA_TAU
    r = lax.broadcasted_iota(jnp.int32, (tm, tm), 0)
    c = lax.broadcasted_iota(jnp.int32, (tm, tm), 1)
    tri = jnp.where((r >= c) & ((r // CHUNK) == (c // CHUNK)), 1.0, 0.0).astype(F32)
    o_ref[...] = jnp.dot(tri, log_a, preferred_element_type=F32, precision=lax.Precision.HIGHEST)


def _gate_call(xb, w_lr, w_up, b_gate, tm):
    t, d = xb.shape
    n = w_up.shape[1]
    return pl.pallas_call(
        _gate_kernel,
        grid=(t // tm,),
        in_specs=[pl.BlockSpec((tm, d), lambda i: (i, 0)),
                  pl.BlockSpec(w_lr.shape, lambda i: (0, 0)),
                  pl.BlockSpec(w_up.shape, lambda i: (0, 0)),
                  pl.BlockSpec((1, n), lambda i: (0, 0))],
        out_specs=pl.BlockSpec((tm, n), lambda i: (i, 0)),
        out_shape=jax.ShapeDtypeStruct((t, n), F32),
        compiler_params=_params("parallel"),
        name="gla_gate",
    )(xb, w_lr, w_up, b_gate)


def _silu(x):
    return x * jax.nn.sigmoid(x)


def _mixer_kernel(p_ref, bc_ref, cos_ref, sin_ref, gg_ref, rg_ref, o_ref, sg_ref, sr_ref, *, n_chunks):
    @pl.when(pl.program_id(1) == 0)
    def _():
        sg_ref[...] = jnp.zeros_like(sg_ref)
        sr_ref[...] = jnp.zeros_like(sr_ref)

    ri = lax.broadcasted_iota(jnp.int32, (CHUNK, CHUNK), 0)
    ci = lax.broadcasted_iota(jnp.int32, (CHUNK, CHUNK), 1)
    lower = ri >= ci
    dist = jnp.abs(ri - ci).astype(F32)
    row = lax.broadcasted_iota(jnp.int32, (CHUNK, RET_DK), 0).astype(F32)
    mid = CHUNK // 2

    def chunk_body(c, carry):
        rows = pl.ds(pl.multiple_of(c * CHUNK, CHUNK), CHUNK)

        def col(ref, base, h, width):
            return ref[rows, base + h * width: base + (h + 1) * width]

        for h in range(GLA_HEADS):
            q = col(p_ref, _C_GQ, h, GLA_DK).astype(F32) * (GLA_DK ** -0.5)
            k = col(p_ref, _C_GK, h, GLA_DK).astype(F32)
            v = col(p_ref, _C_GV, h, GLA_DV)
            gate = col(p_ref, _C_GR, h, GLA_DV).astype(F32)
            bc = col(bc_ref, 0, h, GLA_DK)
            b_last = bc[CHUNK - 1:CHUNK, :]
            b_mid = bc[mid:mid + 1, :]
            e_fwd = jnp.exp(bc - b_mid)
            e_bwd = jnp.exp(b_mid - bc)
            st = sg_ref[h]
            o = lax.dot_general((q * jnp.exp(bc)).astype(BF16), st.astype(BF16), _NT,
                                preferred_element_type=F32)
            a_lo = lax.dot_general((q * e_fwd).astype(BF16), (k * e_bwd).astype(BF16), _NT,
                                   preferred_element_type=F32)
            a_up = lax.dot_general((q * e_bwd).astype(BF16), (k * e_fwd).astype(BF16), _NT,
                                   preferred_element_type=F32)
            a = jnp.where(lower, a_lo, a_up)
            o = o + jnp.dot(a.astype(BF16), v, preferred_element_type=F32)
            u_t = lax.dot_general(v, (k * jnp.exp(b_last - bc)).astype(BF16), _TN,
                                  preferred_element_type=F32)
            sg_ref[h] = st * jnp.exp(b_last) + u_t
            y = o * lax.rsqrt(jnp.mean(o * o, axis=-1, keepdims=True) + LN_EPS)
            y = y * gg_ref[:, h * GLA_DV:(h + 1) * GLA_DV] * _silu(gate)
            o_ref[rows, h * GLA_DV:(h + 1) * GLA_DV] = y.astype(o_ref.dtype)

        cos = cos_ref[rows, :]
        sin = sin_ref[rows, :]
        half = RET_DK // 2

        def rotary(t):
            t1, t2 = t[:, :half], t[:, half:]
            return jnp.concatenate([t1 * cos - t2 * sin, t1 * sin + t2 * cos], axis=-1)

        for h in range(RET_HEADS):
            log_gamma = math.log(1.0 - 2.0 ** (-5.0 - h))
            q = rotary(col(p_ref, _C_RQ, h, RET_DK).astype(F32))
            k = rotary(col(p_ref, _C_RK, h, RET_DK).astype(F32)) * (RET_DK ** -0.5)
            v = col(p_ref, _C_RV, h, RET_DV)
            gate = col(p_ref, _C_RR, h, RET_DV).astype(F32)
            qb = q.astype(BF16)
            st = sr_ref[h]
            o = lax.dot_general(qb, st.astype(BF16), _NT, preferred_element_type=F32)
            o = o * jnp.exp(log_gamma * (row + 1.0))
            a = lax.dot_general(qb, k.astype(BF16), _NT, preferred_element_type=F32)
            a = a * jnp.exp(log_gamma * dist)
            o = o + jnp.dot(a.astype(BF16), v, preferred_element_type=F32)
            k_dec = jnp.exp(log_gamma * (CHUNK - 1.0 - row))
            u_t = lax.dot_general(v, (k * k_dec).astype(BF16), _TN, preferred_element_type=F32)
            sr_ref[h] = st * math.exp(log_gamma * CHUNK) + u_t
            oc = o - jnp.mean(o, axis=-1, keepdims=True)
            y = oc * lax.rsqrt(jnp.mean(oc * oc, axis=-1, keepdims=True) + LN_EPS)
            y = y * rg_ref[:, h * RET_DV:(h + 1) * RET_DV] * _silu(gate)
            c0 = GLA_HEADS * GLA_DV + h * RET_DV
            o_ref[rows, c0:c0 + RET_DV] = y.astype(o_ref.dtype)
        return carry

    lax.fori_loop(0, n_chunks, chunk_body, 0)


def _mixer_call(proj, bc, cos, sin, gla_g, ret_g, batch, seq, n_chunks):
    t = proj.shape[0]
    tb = n_chunks * CHUNK
    groups = seq // tb
    width = GLA_HEADS * GLA_DV + RET_HEADS * RET_DV
    tok = lambda b, g: (b * groups + g, 0)
    return pl.pallas_call(
        functools.partial(_mixer_kernel, n_chunks=n_chunks),
        grid=(batch, groups),
        in_specs=[pl.BlockSpec((tb, proj.shape[1]), tok),
                  pl.BlockSpec((tb, bc.shape[1]), tok),
                  pl.BlockSpec((tb, cos.shape[1]), lambda b, g: (g, 0)),
                  pl.BlockSpec((tb, sin.shape[1]), lambda b, g: (g, 0)),
                  pl.BlockSpec(gla_g.shape, lambda b, g: (0, 0)),
                  pl.BlockSpec(ret_g.shape, lambda b, g: (0, 0))],
        out_specs=pl.BlockSpec((tb, width), tok),
        out_shape=jax.ShapeDtypeStruct((t, width), BF16),
        scratch_shapes=[pltpu.VMEM((GLA_HEADS, GLA_DV, GLA_DK), F32),
                        pltpu.VMEM((RET_HEADS, RET_DV, RET_DK), F32)],
        compiler_params=_params("parallel", "arbitrary"),
        name="mixer",
    )(proj, bc, cos, sin, gla_g, ret_g)


def _layer_norm(y, g, b):
    mu = jnp.mean(y, axis=-1, keepdims=True)
    yc = y - mu
    var = jnp.mean(yc * yc, axis=-1, keepdims=True)
    return yc * lax.rsqrt(var + LN_EPS) * g + b


def _out_ln_kernel(m_ref, w_ref, x_ref, g_ref, b_ref, h_ref, hb_ref):
    mix = jnp.dot(m_ref[...], w_ref[...], preferred_element_type=F32)
    h = _layer_norm(DEEPNORM_ALPHA * x_ref[...] + mix, g_ref[...], b_ref[...])
    h_ref[...] = h
    hb_ref[...] = h.astype(BF16)


def _out_ln_call(mix, w_out, x, g, b, tm):
    t, d = x.shape
    return pl.pallas_call(
        _out_ln_kernel,
        grid=(t // tm,),
        in_specs=[pl.BlockSpec((tm, mix.shape[1]), lambda i: (i, 0)),
                  pl.BlockSpec(w_out.shape, lambda i: (0, 0)),
                  pl.BlockSpec((tm, d), lambda i: (i, 0)),
                  pl.BlockSpec((1, d), lambda i: (0, 0)),
                  pl.BlockSpec((1, d), lambda i: (0, 0))],
        out_specs=[pl.BlockSpec((tm, d), lambda i: (i, 0)),
                   pl.BlockSpec((tm, d), lambda i: (i, 0))],
        out_shape=[jax.ShapeDtypeStruct((t, d), F32), jax.ShapeDtypeStruct((t, d), BF16)],
        compiler_params=_params("parallel"),
        name="out_proj_ln",
    )(mix, w_out, x, g, b)


def _topk_rows(s, k):
    n = s.shape[0]
    iota = lax.broadcasted_iota(jnp.int32, s.shape, 0)
    vals, idxs = [], []
    for _ in range(k):
        m = jnp.max(s, axis=0, keepdims=True)
        idx = jnp.min(jnp.where(s == m, iota, n), axis=0, keepdims=True)
        vals.append(m)
        idxs.append(idx)
        s = jnp.where(iota == idx, -jnp.inf, s)
    return jnp.concatenate(vals, axis=0), jnp.concatenate(idxs, axis=0)


def _route_kernel(h_ref, wq_ref, sk_ref, a_ref, b_ref, g_ref):
    kk = PEER_TOPK
    q = jnp.dot(h_ref[...], wq_ref[...], preferred_element_type=F32).astype(BF16)
    rank = lax.broadcasted_iota(jnp.int32, (kk, q.shape[0]), 0)
    for h in range(PEER_HEADS):
        vals, idxs = [], []
        for p in range(2):
            j = 2 * h + p
            s_t = lax.dot_general(sk_ref[j], q[:, j * LANES:(j + 1) * LANES], _NT,
                                  preferred_element_type=F32)
            v_p, i_p = _topk_rows(s_t, kk)
            vals.append(v_p)
            idxs.append(i_p)
        cand = jnp.concatenate([vals[0][r:r + 1, :] + vals[1] for r in range(kk)], axis=0)
        top_s, sel = _topk_rows(cand, kk)
        k1 = sel >> 4
        k2 = sel & (kk - 1)
        a_rows, b_rows = [], []
        for r in range(kk):
            a_rows.append(jnp.sum(jnp.where(rank == k1[r:r + 1, :], idxs[0], 0), axis=0, keepdims=True))
            b_rows.append(jnp.sum(jnp.where(rank == k2[r:r + 1, :], idxs[1], 0), axis=0, keepdims=True))
        e = jnp.exp(top_s - top_s[0:1, :])
        gates = e / jnp.sum(e, axis=0, keepdims=True)
        a_ref[h * kk:(h + 1) * kk, :] = jnp.concatenate(a_rows, axis=0)
        b_ref[h * kk:(h + 1) * kk, :] = jnp.concatenate(b_rows, axis=0)
        g_ref[h * kk:(h + 1) * kk, :] = gates


def _route_call(hb, w_q, subkeys, tt):
    t, d = hb.shape
    n_sel = PEER_HEADS * PEER_TOPK
    out = lambda dt: jax.ShapeDtypeStruct((n_sel, t), dt)
    return pl.pallas_call(
        _route_kernel,
        grid=(t // tt,),
        in_specs=[pl.BlockSpec((tt, d), lambda i: (i, 0)),
                  pl.BlockSpec(w_q.shape, lambda i: (0, 0)),
                  pl.BlockSpec(subkeys.shape, lambda i: (0, 0, 0))],
        out_specs=[pl.BlockSpec((n_sel, tt), lambda i: (0, i))] * 3,
        out_shape=[out(jnp.int32), out(jnp.int32), out(F32)],
        compiler_params=_params("parallel"),
        name="peer_route",
    )(hb, w_q, subkeys)


def _wbuild_kernel(a_ref, b_ref, g_ref, w_ref):
    n_tok = a_ref.shape[0]
    n_sel = a_ref.shape[1]
    key = lax.broadcasted_iota(jnp.int32, (PEER_NKEYS, n_sel), 0)

    def token(t, carry):
        a = a_ref[pl.ds(t, 1), :]
        b = b_ref[pl.ds(t, 1), :]
        g = g_ref[pl.ds(t, 1), :]
        pa = jnp.where(key == a, g, 0.0).astype(BF16)
        pb = jnp.where(key == b, 1.0, 0.0).astype(BF16)
        w = lax.dot_general(pa, pb, _NT, preferred_element_type=F32)
        w_ref[t // SUBLANES, pl.ds(t % SUBLANES, PEER_NKEYS, stride=SUBLANES), :] = w
        return carry

    lax.fori_loop(0, n_tok, token, 0)


def _wbuild_call(a_tm, b_tm, g_tm, tt):
    t, n_sel = a_tm.shape
    rows = PEER_NKEYS * SUBLANES
    return pl.pallas_call(
        _wbuild_kernel,
        grid=(t // tt,),
        in_specs=[pl.BlockSpec((tt, n_sel), lambda i: (i, 0))] * 3,
        out_specs=pl.BlockSpec((tt // SUBLANES, rows, PEER_NKEYS), lambda i: (i, 0, 0)),
        out_shape=jax.ShapeDtypeStruct((t // SUBLANES, rows, PEER_NKEYS), F32),
        compiler_params=_params("parallel"),
        name="peer_gate_map",
    )(a_tm, b_tm, g_tm)


def _gelu(x):
    return 0.5 * x * (1.0 + lax.erf(x * math.sqrt(0.5)))


def _peer_kernel(hb_ref, ut_ref, v_ref, w_ref, h_ref, g_ref, b_ref, o_ref, acc_ref, gs_ref, *, n_first):
    j = pl.program_id(1)

    @pl.when(j == 0)
    def _():
        acc_ref[...] = jnp.zeros_like(acc_ref)

    tm = hb_ref.shape[0]
    act = _gelu(jnp.dot(hb_ref[...], ut_ref[...], preferred_element_type=F32))
    for a in range(n_first):
        w_a = w_ref[:, a, :, :].reshape(tm, PEER_NKEYS)
        gs_ref[:, a * PEER_NKEYS:(a + 1) * PEER_NKEYS] = (
            w_a * act[:, a * PEER_NKEYS:(a + 1) * PEER_NKEYS]).astype(BF16)
    acc_ref[...] += jnp.dot(gs_ref[...], v_ref[...], preferred_element_type=F32)

    @pl.when(j == pl.num_programs(1) - 1)
    def _():
        o_ref[...] = _layer_norm(DEEPNORM_ALPHA * h_ref[...] + acc_ref[...], g_ref[...], b_ref[...])


def _peer_call(hb, u_t, v_tab, w4, h, g, b, tm, n_first):
    t, d = h.shape
    n_exp = v_tab.shape[0]
    tn = n_first * PEER_NKEYS
    return pl.pallas_call(
        functools.partial(_peer_kernel, n_first=n_first),
        grid=(t // tm, n_exp // tn),
        in_specs=[pl.BlockSpec((tm, d), lambda i, j: (i, 0)),
                  pl.BlockSpec((d, tn), lambda i, j: (0, j)),
                  pl.BlockSpec((tn, d), lambda i, j: (j, 0)),
                  pl.BlockSpec((tm // SUBLANES, n_first, SUBLANES, PEER_NKEYS), lambda i, j: (i, j, 0, 0)),
                  pl.BlockSpec((tm, d), lambda i, j: (i, 0)),
                  pl.BlockSpec((1, d), lambda i, j: (0, 0)),
                  pl.BlockSpec((1, d), lambda i, j: (0, 0))],
        out_specs=pl.BlockSpec((tm, d), lambda i, j: (i, 0)),
        out_shape=jax.ShapeDtypeStruct((t, d), F32),
        scratch_shapes=[pltpu.VMEM((tm, d), F32), pltpu.VMEM((tm, tn), BF16)],
        compiler_params=_params("parallel", "arbitrary"),
        name="peer_dense",
    )(hb, u_t, v_tab, w4, h, g, b)


def _layer(x2, batch, seq, w_in, w_gate_up, b_gate, gla_g, ret_g, w_out, ln1_g, ln1_b,
           w_q, subkeys, u_tab, v_tab, ln2_g, ln2_b):
    t, d = x2.shape
    row = lambda p: p.reshape(1, -1).astype(F32)

    lr0 = _C_RQ
    w_main = jnp.concatenate([w_in[:, :lr0], w_in[:, lr0 + GLA_GATE_RANK:]], axis=1).astype(BF16)
    w_lr = jnp.pad(w_in[:, lr0:lr0 + GLA_GATE_RANK], ((0, 0), (0, LANES - GLA_GATE_RANK))).astype(BF16)
    w_up = jnp.pad(w_gate_up, ((0, LANES - GLA_GATE_RANK), (0, 0))).astype(BF16)
    xb = x2.astype(BF16)

    tm = min(1024, t)
    proj = _matmul(xb, w_main, BF16, tm, 1024)
    bc = _gate_call(xb, w_lr, w_up, row(b_gate), min(256, t))

    half = RET_DK // 2
    inv = ROPE_BASE ** (-jnp.arange(half, dtype=F32) / half)
    ang = jnp.arange(seq).astype(F32)[:, None] * inv[None, :]
    n_chunks = min(8, seq // CHUNK)
    mix = _mixer_call(proj, bc, jnp.cos(ang), jnp.sin(ang), row(gla_g), row(ret_g), batch, seq, n_chunks)

    h, hb = _out_ln_call(mix, w_out.astype(BF16), x2, row(ln1_g), row(ln1_b), min(512, t))

    sk = subkeys.reshape(PEER_HEADS * 2, PEER_NKEYS, -1).astype(BF16)
    a_sel, b_sel, gates = _route_call(hb, w_q.astype(BF16), sk, min(256, t))
    w3 = _wbuild_call(a_sel.T, b_sel.T, gates.T, min(128, t))
    w4 = w3.reshape(t // SUBLANES, PEER_NKEYS, SUBLANES, PEER_NKEYS)

    return _peer_call(hb, u_tab.T.astype(BF16), v_tab.astype(BF16), w4, h, row(ln2_g), row(ln2_b),
                      min(512, t), 8)


def kernel(x, w_in, w_gla_gate_up, b_gla_gate, gla_norm_g, ret_norm_g, w_out, ln1_g, ln1_b,
           w_peer_q, peer_subkeys, peer_u, peer_v, ln2_g, ln2_b):
    batch, seq, d = x.shape
    h = x.reshape(batch * seq, d)
    for l in range(DEPTH):
        h = _layer(h, batch, seq, w_in[l], w_gla_gate_up[l], b_gla_gate[l], gla_norm_g[l], ret_norm_g[l],
                   w_out[l], ln1_g[l], ln1_b[l], w_peer_q[l], peer_subkeys[l], peer_u[l], peer_v[l],
                   ln2_g[l], ln2_b[l])
    return h.reshape(batch, seq, d)
```

```python
import functools
import math

import jax
import jax.numpy as jnp
from jax import lax
from jax.experimental import pallas as pl
from jax.experimental.pallas import tpu as pltpu

F32 = jnp.float32
BF16 = jnp.bfloat16

DEPTH = 1
CHUNK = 64
GLA_HEADS = 4
GLA_DK = 128
GLA_DV = 256
GLA_GATE_RANK = 16
GLA_TAU = 16.0
RET_HEADS = 4
RET_DK = 256
RET_DV = 256
ROPE_BASE = 10000.0
PEER_HEADS = 8
PEER_NKEYS = 128
PEER_TOPK = 16
LN_EPS = 1e-5
DEEPNORM_ALPHA = (2 * DEPTH) ** 0.25

LANES = 128
SUBLANES = 8
VMEM_LIMIT = 56 * 1024 * 1024

_C_GQ = 0
_C_GK = _C_GQ + GLA_HEADS * GLA_DK
_C_GV = _C_GK + GLA_HEADS * GLA_DK
_C_GR = _C_GV + GLA_HEADS * GLA_DV
_C_RQ = _C_GR + GLA_HEADS * GLA_DV
_C_RK = _C_RQ + RET_HEADS * RET_DK
_C_RV = _C_RK + RET_HEADS * RET_DK
_C_RR = _C_RV + RET_HEADS * RET_DV
_C_END = _C_RR + RET_HEADS * RET_DV

_NT = (((1,), (1,)), ((), ()))
_TN = (((0,), (0,)), ((), ()))


def _params(*sem):
    return pltpu.CompilerParams(dimension_semantics=sem, vmem_limit_bytes=VMEM_LIMIT)


def _mm_kernel(x_ref, w_ref, o_ref):
    o_ref[...] = jnp.dot(x_ref[...], w_ref[...], preferred_element_type=F32).astype(o_ref.dtype)


def _matmul(x, w, out_dtype, tm, tn):
    m, k = x.shape
    n = w.shape[1]
    return pl.pallas_call(
        _mm_kernel,
        grid=(m // tm, n // tn),
        in_specs=[pl.BlockSpec((tm, k), lambda i, j: (i, 0)),
                  pl.BlockSpec((k, tn), lambda i, j: (0, j))],
        out_specs=pl.BlockSpec((tm, tn), lambda i, j: (i, j)),
        out_shape=jax.ShapeDtypeStruct((m, n), out_dtype),
        compiler_params=_params("parallel", "parallel"),
        name="proj_matmul",
    )(x, w)


def _gate_kernel(x_ref, wl_ref, wu_ref, b_ref, o_ref):
    tm = x_ref.shape[0]
    glr = jnp.dot(x_ref[...], wl_ref[...], preferred_element_type=F32)
    z = jnp.dot(glr.astype(BF16), wu_ref[...], preferred_element_type=F32) + b_ref[...]
    log_a = (jnp.minimum(z, 0.0) - jnp.log1p(jnp.exp(-jnp.abs(z)))) / GLA_TAU
    r = lax.broadcasted_iota(jnp.int32, (tm, tm), 0)
    c = lax.broadcasted_iota(jnp.int32, (tm, tm), 1)
    tri = jnp.where((r >= c) & ((r // CHUNK) == (c // CHUNK)), 1.0, 0.0).astype(F32)
    o_ref[...] = jnp.dot(tri, log_a, preferred_element_type=F32, precision=lax.Precision.HIGHEST)


def _gate_call(xb, w_lr, w_up, b_gate, tm):
    t, d = xb.shape
    n = w_up.shape[1]
    return pl.pallas_call(
        _gate_kernel,
        grid=(t // tm,),
        in_specs=[pl.BlockSpec((tm, d), lambda i: (i, 0)),
                  pl.BlockSpec(w_lr.shape, lambda i: (0, 0)),
                  pl.BlockSpec(w_up.shape, lambda i: (0, 0)),
                  pl.BlockSpec((1, n), lambda i: (0, 0))],
        out_specs=pl.BlockSpec((tm, n), lambda i: (i, 0)),
        out_shape=jax.ShapeDtypeStruct((t, n), F32),
        compiler_params=_params("parallel"),
        name="gla_gate",
    )(xb, w_lr, w_up, b_gate)


def _silu(x):
    return x * jax.nn.sigmoid(x)


def _mixer_kernel(p_ref, bc_ref, cos_ref, sin_ref, gg_ref, rg_ref, o_ref, sg_ref, sr_ref, *, n_chunks):
    @pl.when(pl.program_id(1) == 0)
    def _():
        sg_ref[...] = jnp.zeros_like(sg_ref)
        sr_ref[...] = jnp.zeros_like(sr_ref)

    ri = lax.broadcasted_iota(jnp.int32, (CHUNK, CHUNK), 0)
    ci = lax.broadcasted_iota(jnp.int32, (CHUNK, CHUNK), 1)
    lower = ri >= ci
    dist = jnp.abs(ri - ci).astype(F32)
    row = lax.broadcasted_iota(jnp.int32, (CHUNK, RET_DK), 0).astype(F32)
    mid = CHUNK // 2

    def chunk_body(c, carry):
        rows = pl.ds(pl.multiple_of(c * CHUNK, CHUNK), CHUNK)

        def col(ref, base, h, width):
            return ref[rows, base + h * width: base + (h + 1) * width]

        for h in range(GLA_HEADS):
            q = col(p_ref, _C_GQ, h, GLA_DK).astype(F32) * (GLA_DK ** -0.5)
            k = col(p_ref, _C_GK, h, GLA_DK).astype(F32)
            v = col(p_ref, _C_GV, h, GLA_DV)
            gate = col(p_ref, _C_GR, h, GLA_DV).astype(F32)
            bc = col(bc_ref, 0, h, GLA_DK)
            b_last = bc[CHUNK - 1:CHUNK, :]
            b_mid = bc[mid:mid + 1, :]
            e_fwd = jnp.exp(bc - b_mid)
            e_bwd = jnp.exp(b_mid - bc)
            st = sg_ref[h]
            o = lax.dot_general((q * jnp.exp(bc)).astype(BF16), st.astype(BF16), _NT,
                                preferred_element_type=F32)
            a_lo = lax.dot_general((q * e_fwd).astype(BF16), (k * e_bwd).astype(BF16), _NT,
                                   preferred_element_type=F32)
            a_up = lax.dot_general((q * e_bwd).astype(BF16), (k * e_fwd).astype(BF16), _NT,
                                   preferred_element_type=F32)
            a = jnp.where(lower, a_lo, a_up)
            o = o + jnp.dot(a.astype(BF16), v, preferred_element_type=F32)
            u_t = lax.dot_general(v, (k * jnp.exp(b_last - bc)).astype(BF16), _TN,
                                  preferred_element_type=F32)
            sg_ref[h] = st * jnp.exp(b_last) + u_t
            y = o * lax.rsqrt(jnp.mean(o * o, axis=-1, keepdims=True) + LN_EPS)
            y = y * gg_ref[:, h * GLA_DV:(h + 1) * GLA_DV] * _silu(gate)
            o_ref[rows, h * GLA_DV:(h + 1) * GLA_DV] = y.astype(o_ref.dtype)

        cos = cos_ref[rows, :]
        sin = sin_ref[rows, :]
        half = RET_DK // 2

        def rotary(t):
            t1, t2 = t[:, :half], t[:, half:]
            return jnp.concatenate([t1 * cos - t2 * sin, t1 * sin + t2 * cos], axis=-1)

        for h in range(RET_HEADS):
            log_gamma = math.log(1.0 - 2.0 ** (-5.0 - h))
            q = rotary(col(p_ref, _C_RQ, h, RET_DK).astype(F32))
            k = rotary(col(p_ref, _C_RK, h, RET_DK).astype(F32)) * (RET_DK ** -0.5)
            v = col(p_ref, _C_RV, h, RET_DV)
            gate = col(p_ref, _C_RR, h, RET_DV).astype(F32)
            qb = q.astype(BF16)
            st = sr_ref[h]
            o = lax.dot_general(qb, st.astype(BF16), _NT, preferred_element_type=F32)
            o = o * jnp.exp(log_gamma * (row + 1.0))
            a = lax.dot_general(qb, k.astype(BF16), _NT, preferred_element_type=F32)
            a = a * jnp.exp(log_gamma * dist)
            o = o + jnp.dot(a.astype(BF16), v, preferred_element_type=F32)
            k_dec = jnp.exp(log_gamma * (CHUNK - 1.0 - row))
            u_t = lax.dot_general(v, (k * k_dec).astype(BF16), _TN, preferred_element_type=F32)
            sr_ref[h] = st * math.exp(log_gamma * CHUNK) + u_t
            oc = o - jnp.mean(o, axis=-1, keepdims=True)
            y = oc * lax.rsqrt(jnp.mean(oc * oc, axis=-1, keepdims=True) + LN_EPS)
            y = y * rg_ref[:, h * RET_DV:(h + 1) * RET_DV] * _silu(gate)
            c0 = GLA_HEADS * GLA_DV + h * RET_DV
            o_ref[rows, c0:c0 + RET_DV] = y.astype(o_ref.dtype)
        return carry

    lax.fori_loop(0, n_chunks, chunk_body, 0)


def _mixer_call(proj, bc, cos, sin, gla_g, ret_g, batch, seq, n_chunks):
    t = proj.shape[0]
    tb = n_chunks * CHUNK
    groups = seq // tb
    width = GLA_HEADS * GLA_DV + RET_HEADS * RET_DV
    tok = lambda b, g: (b * groups + g, 0)
    return pl.pallas_call(
        functools.partial(_mixer_kernel, n_chunks=n_chunks),
        grid=(batch, groups),
        in_specs=[pl.BlockSpec((tb, proj.shape[1]), tok),
                  pl.BlockSpec((tb, bc.shape[1]), tok),
                  pl.BlockSpec((tb, cos.shape[1]), lambda b, g: (g, 0)),
                  pl.BlockSpec((tb, sin.shape[1]), lambda b, g: (g, 0)),
                  pl.BlockSpec(gla_g.shape, lambda b, g: (0, 0)),
                  pl.BlockSpec(ret_g.shape, lambda b, g: (0, 0))],
        out_specs=pl.BlockSpec((tb, width), tok),
        out_shape=jax.ShapeDtypeStruct((t, width), BF16),
        scratch_shapes=[pltpu.VMEM((GLA_HEADS, GLA_DV, GLA_DK), F32),
                        pltpu.VMEM((RET_HEADS, RET_DV, RET_DK), F32)],
        compiler_params=_params("parallel", "arbitrary"),
        name="mixer",
    )(proj, bc, cos, sin, gla_g, ret_g)


def _layer_norm(y, g, b):
    mu = jnp.mean(y, axis=-1, keepdims=True)
    yc = y - mu
    var = jnp.mean(yc * yc, axis=-1, keepdims=True)
    return yc * lax.rsqrt(var + LN_EPS) * g + b


def _out_ln_kernel(m_ref, w_ref, x_ref, g_ref, b_ref, h_ref, hb_ref):
    mix = jnp.dot(m_ref[...], w_ref[...], preferred_element_type=F32)
    h = _layer_norm(DEEPNORM_ALPHA * x_ref[...] + mix, g_ref[...], b_ref[...])
    h_ref[...] = h
    hb_ref[...] = h.astype(BF16)


def _out_ln_call(mix, w_out, x, g, b, tm):
    t, d = x.shape
    return pl.pallas_call(
        _out_ln_kernel,
        grid=(t // tm,),
        in_specs=[pl.BlockSpec((tm, mix.shape[1]), lambda i: (i, 0)),
                  pl.BlockSpec(w_out.shape, lambda i: (0, 0)),
                  pl.BlockSpec((tm, d), lambda i: (i, 0)),
                  pl.BlockSpec((1, d), lambda i: (0, 0)),
                  pl.BlockSpec((1, d), lambda i: (0, 0))],
        out_specs=[pl.BlockSpec((tm, d), lambda i: (i, 0)),
                   pl.BlockSpec((tm, d), lambda i: (i, 0))],
        out_shape=[jax.ShapeDtypeStruct((t, d), F32), jax.ShapeDtypeStruct((t, d), BF16)],
        compiler_params=_params("parallel"),
        name="out_proj_ln",
    )(mix, w_out, x, g, b)


def _topk_rows(s, k, payload=None):
    n = s.shape[0]
    iota = lax.broadcasted_iota(jnp.int32, s.shape, 0).astype(F32)
    vals, picks = [], []
    for _ in range(k):
        m = jnp.max(s, axis=0, keepdims=True)
        idx = jnp.min(jnp.where(s == m, iota, float(n)), axis=0, keepdims=True)
        hit = iota == idx
        vals.append(m)
        if payload is None:
            picks.append(idx)
        else:
            picks.append(jnp.sum(jnp.where(hit, payload, 0.0), axis=0, keepdims=True))
        s = jnp.where(hit, -jnp.inf, s)
    return jnp.concatenate(vals, axis=0), jnp.concatenate(picks, axis=0)


def _route_kernel(h_ref, wq_ref, sk_ref, a_ref, b_ref, g_ref):
    kk = PEER_TOPK
    q = jnp.dot(h_ref[...], wq_ref[...], preferred_element_type=F32).astype(BF16)
    n_tok = q.shape[0]
    for h in range(PEER_HEADS):
        vals, idxs = [], []
        for p in range(2):
            j = 2 * h + p
            s_t = lax.dot_general(sk_ref[j], q[:, j * LANES:(j + 1) * LANES], _NT,
                                  preferred_element_type=F32)
            v_p, i_p = _topk_rows(s_t, kk)
            vals.append(v_p)
            idxs.append(i_p)
        cand, expert = [], []
        for r1 in range(kk):
            n2 = kk // (r1 + 1)
            cand.append(vals[0][r1:r1 + 1, :] + vals[1][:n2, :])
            expert.append(idxs[0][r1:r1 + 1, :] * float(PEER_NKEYS) + idxs[1][:n2, :])
        n_cand = sum(kk // (r1 + 1) for r1 in range(kk))
        pad = (-n_cand) % SUBLANES
        cand.append(jnp.full((pad, n_tok), -jnp.inf, F32))
        expert.append(jnp.zeros((pad, n_tok), F32))
        top_s, top_e = _topk_rows(jnp.concatenate(cand, axis=0), kk, jnp.concatenate(expert, axis=0))
        top_e = top_e.astype(jnp.int32)
        e = jnp.exp(top_s - top_s[0:1, :])
        gates = e / jnp.sum(e, axis=0, keepdims=True)
        a_ref[h * kk:(h + 1) * kk, :] = top_e >> (PEER_NKEYS.bit_length() - 1)
        b_ref[h * kk:(h + 1) * kk, :] = top_e & (PEER_NKEYS - 1)
        g_ref[h * kk:(h + 1) * kk, :] = gates


def _route_call(hb, w_q, subkeys, tt):
    t, d = hb.shape
    n_sel = PEER_HEADS * PEER_TOPK
    out = lambda dt: jax.ShapeDtypeStruct((n_sel, t), dt)
    return pl.pallas_call(
        _route_kernel,
        grid=(t // tt,),
        in_specs=[pl.BlockSpec((tt, d), lambda i: (i, 0)),
                  pl.BlockSpec(w_q.shape, lambda i: (0, 0)),
                  pl.BlockSpec(subkeys.shape, lambda i: (0, 0, 0))],
        out_specs=[pl.BlockSpec((n_sel, tt), lambda i: (0, i))] * 3,
        out_shape=[out(jnp.int32), out(jnp.int32), out(F32)],
        compiler_params=_params("parallel"),
        name="peer_route",
    )(hb, w_q, subkeys)


_WBUILD_UNROLL = 4


def _wbuild_kernel(a_ref, b_ref, g_ref, w_ref):
    n_tok = a_ref.shape[0]
    n_sel = a_ref.shape[1]
    key = lax.broadcasted_iota(jnp.int32, (PEER_NKEYS, n_sel), 0)

    def token_groups(it, carry):
        for u in range(_WBUILD_UNROLL):
            tg = it * _WBUILD_UNROLL + u
            t0 = pl.multiple_of(tg * SUBLANES, SUBLANES)
            a8 = a_ref[pl.ds(t0, SUBLANES), :]
            b8 = b_ref[pl.ds(t0, SUBLANES), :]
            g8 = g_ref[pl.ds(t0, SUBLANES), :]
            for i in range(SUBLANES):
                pa = jnp.where(key == a8[i:i + 1, :], g8[i:i + 1, :], 0.0).astype(BF16)
                pb = jnp.where(key == b8[i:i + 1, :], 1.0, 0.0).astype(BF16)
                w = lax.dot_general(pa, pb, _NT, preferred_element_type=F32)
                w_ref[tg, pl.ds(i, PEER_NKEYS, stride=SUBLANES), :] = w
        return carry

    lax.fori_loop(0, n_tok // (SUBLANES * _WBUILD_UNROLL), token_groups, 0)


def _wbuild_call(a_tm, b_tm, g_tm, tt):
    t, n_sel = a_tm.shape
    rows = PEER_NKEYS * SUBLANES
    return pl.pallas_call(
        _wbuild_kernel,
        grid=(t // tt,),
        in_specs=[pl.BlockSpec((tt, n_sel), lambda i: (i, 0))] * 3,
        out_specs=pl.BlockSpec((tt // SUBLANES, rows, PEER_NKEYS), lambda i: (i, 0, 0)),
        out_shape=jax.ShapeDtypeStruct((t // SUBLANES, rows, PEER_NKEYS), F32),
        compiler_params=_params("parallel"),
        name="peer_gate_map",
    )(a_tm, b_tm, g_tm)


def _gelu(x):
    return 0.5 * x * (1.0 + lax.erf(x * math.sqrt(0.5)))


def _peer_kernel(hb_ref, ut_ref, v_ref, w_ref, h_ref, g_ref, b_ref, o_ref, acc_ref, gs_ref, *, n_first):
    j = pl.program_id(1)

    @pl.when(j == 0)
    def _():
        acc_ref[...] = jnp.zeros_like(acc_ref)

    tm = hb_ref.shape[0]
    act = _gelu(jnp.dot(hb_ref[...], ut_ref[...], preferred_element_type=F32))
    for a in range(n_first):
        w_a = w_ref[:, a, :, :].reshape(tm, PEER_NKEYS)
        gs_ref[:, a * PEER_NKEYS:(a + 1) * PEER_NKEYS] = (
            w_a * act[:, a * PEER_NKEYS:(a + 1) * PEER_NKEYS]).astype(BF16)
    acc_ref[...] += jnp.dot(gs_ref[...], v_ref[...], preferred_element_type=F32)

    @pl.when(j == pl.num_programs(1) - 1)
    def _():
        o_ref[...] = _layer_norm(DEEPNORM_ALPHA * h_ref[...] + acc_ref[...], g_ref[...], b_ref[...])


def _peer_call(hb, u_t, v_tab, w4, h, g, b, tm, n_first):
    t, d = h.shape
    n_exp = v_tab.shape[0]
    tn = n_first * PEER_NKEYS
    return pl.pallas_call(
        functools.partial(_peer_kernel, n_first=n_first),
        grid=(t // tm, n_exp // tn),
        in_specs=[pl.BlockSpec((tm, d), lambda i, j: (i, 0)),
                  pl.BlockSpec((d, tn), lambda i, j: (0, j)),
                  pl.BlockSpec((tn, d), lambda i, j: (j, 0)),
                  pl.BlockSpec((tm // SUBLANES, n_first, SUBLANES, PEER_NKEYS), lambda i, j: (i, j, 0, 0)),
                  pl.BlockSpec((tm, d), lambda i, j: (i, 0)),
                  pl.BlockSpec((1, d), lambda i, j: (0, 0)),
                  pl.BlockSpec((1, d), lambda i, j: (0, 0))],
        out_specs=pl.BlockSpec((tm, d), lambda i, j: (i, 0)),
        out_shape=jax.ShapeDtypeStruct((t, d), F32),
        scratch_shapes=[pltpu.VMEM((tm, d), F32), pltpu.VMEM((tm, tn), BF16)],
        compiler_params=_params("parallel", "arbitrary"),
        name="peer_dense",
    )(hb, u_t, v_tab, w4, h, g, b)


def _layer(x2, batch, seq, w_in, w_gate_up, b_gate, gla_g, ret_g, w_out, ln1_g, ln1_b,
           w_q, subkeys, u_tab, v_tab, ln2_g, ln2_b):
    t, d = x2.shape
    row = lambda p: p.reshape(1, -1).astype(F32)

    lr0 = _C_RQ
    w_main = jnp.concatenate([w_in[:, :lr0], w_in[:, lr0 + GLA_GATE_RANK:]], axis=1).astype(BF16)
    w_lr = jnp.pad(w_in[:, lr0:lr0 + GLA_GATE_RANK], ((0, 0), (0, LANES - GLA_GATE_RANK))).astype(BF16)
    w_up = jnp.pad(w_gate_up, ((0, LANES - GLA_GATE_RANK), (0, 0))).astype(BF16)
    xb = x2.astype(BF16)

    tm = min(1024, t)
    proj = _matmul(xb, w_main, BF16, tm, 1024)
    bc = _gate_call(xb, w_lr, w_up, row(b_gate), min(256, t))

    half = RET_DK // 2
    inv = ROPE_BASE ** (-jnp.arange(half, dtype=F32) / half)
    ang = jnp.arange(seq).astype(F32)[:, None] * inv[None, :]
    n_chunks = min(8, seq // CHUNK)
    mix = _mixer_call(proj, bc, jnp.cos(ang), jnp.sin(ang), row(gla_g), row(ret_g), batch, seq, n_chunks)

    h, hb = _out_ln_call(mix, w_out.astype(BF16), x2, row(ln1_g), row(ln1_b), min(512, t))

    sk = subkeys.reshape(PEER_HEADS * 2, PEER_NKEYS, -1).astype(BF16)
    a_sel, b_sel, gates = _route_call(hb, w_q.astype(BF16), sk, min(256, t))
    w3 = _wbuild_call(a_sel.T, b_sel.T, gates.T, min(128, t))
    w4 = w3.reshape(t // SUBLANES, PEER_NKEYS, SUBLANES, PEER_NKEYS)

    return _peer_call(hb, u_tab.T.astype(BF16), v_tab.astype(BF16), w4, h, row(ln2_g), row(ln2_b),
                      min(512, t), 8)


def kernel(x, w_in, w_gla_gate_up, b_gla_gate, gla_norm_g, ret_norm_g, w_out, ln1_g, ln1_b,
           w_peer_q, peer_subkeys, peer_u, peer_v, ln2_g, ln2_b):
    batch, seq, d = x.shape
    h = x.reshape(batch * seq, d)
    for l in range(DEPTH):
        h = _layer(h, batch, seq, w_in[l], w_gla_gate_up[l], b_gla_gate[l], gla_norm_g[l], ret_norm_g[l],
                   w_out[l], ln1_g[l], ln1_b[l], w_peer_q[l], peer_subkeys[l], peer_u[l], peer_v[l],
                   ln2_g[l], ln2_b[l])
    return h.reshape(batch, seq, d)
```

```python
import functools
import math

import jax
import jax.numpy as jnp
from jax import lax
from jax.experimental import pallas as pl
from jax.experimental.pallas import tpu as pltpu

F32 = jnp.float32
BF16 = jnp.bfloat16

DEPTH = 1
CHUNK = 64
GLA_HEADS = 4
GLA_DK = 128
GLA_DV = 256
GLA_GATE_RANK = 16
GLA_TAU = 16.0
RET_HEADS = 4
RET_DK = 256
RET_DV = 256
ROPE_BASE = 10000.0
PEER_HEADS = 8
PEER_NKEYS = 128
PEER_TOPK = 16
LN_EPS = 1e-5
DEEPNORM_ALPHA = (2 * DEPTH) ** 0.25

LANES = 128
SUBLANES = 8
VMEM_LIMIT = 56 * 1024 * 1024

_C_GQ = 0
_C_GK = _C_GQ + GLA_HEADS * GLA_DK
_C_GV = _C_GK + GLA_HEADS * GLA_DK
_C_GR = _C_GV + GLA_HEADS * GLA_DV
_C_RQ = _C_GR + GLA_HEADS * GLA_DV
_C_RK = _C_RQ + RET_HEADS * RET_DK
_C_RV = _C_RK + RET_HEADS * RET_DK
_C_RR = _C_RV + RET_HEADS * RET_DV
_C_END = _C_RR + RET_HEADS * RET_DV

_NT = (((1,), (1,)), ((), ()))
_TN = (((0,), (0,)), ((), ()))


def _params(*sem):
    return pltpu.CompilerParams(dimension_semantics=sem, vmem_limit_bytes=VMEM_LIMIT)


def _proj_kernel(x_ref, wg_ref, wr_ref, o_ref, xb_ref, *, n_gla_tiles):
    j = pl.program_id(1)

    @pl.when(j == 0)
    def _():
        xb_ref[...] = x_ref[...].astype(BF16)

    @pl.when(j < n_gla_tiles)
    def _():
        o_ref[...] = jnp.dot(xb_ref[...], wg_ref[...], preferred_element_type=F32).astype(o_ref.dtype)

    @pl.when(j >= n_gla_tiles)
    def _():
        o_ref[...] = jnp.dot(xb_ref[...], wr_ref[...], preferred_element_type=F32).astype(o_ref.dtype)


def _proj_call(x, w_all, w_ret, tm, tn):
    m, k = x.shape
    n_gla_tiles = _C_RQ // tn
    n_ret_tiles = w_ret.shape[1] // tn
    return pl.pallas_call(
        functools.partial(_proj_kernel, n_gla_tiles=n_gla_tiles),
        grid=(m // tm, n_gla_tiles + n_ret_tiles),
        in_specs=[pl.BlockSpec((tm, k), lambda i, j: (i, 0)),
                  pl.BlockSpec((k, tn), lambda i, j: (0, jnp.minimum(j, n_gla_tiles - 1))),
                  pl.BlockSpec((k, tn), lambda i, j: (0, jnp.maximum(j - n_gla_tiles, 0)))],
        out_specs=pl.BlockSpec((tm, tn), lambda i, j: (i, j)),
        out_shape=jax.ShapeDtypeStruct((m, _C_END), BF16),
        scratch_shapes=[pltpu.VMEM((tm, k), BF16)],
        compiler_params=_params("parallel", "arbitrary"),
        name="proj_matmul",
    )(x, w_all, w_ret)


def _gate_kernel(x_ref, wl_ref, wu_ref, b_ref, o_ref):
    tm = x_ref.shape[0]
    glr = jnp.dot(x_ref[...].astype(BF16), wl_ref[...], preferred_element_type=F32)
    lane = lax.broadcasted_iota(jnp.int32, glr.shape, 1)
    glr = jnp.where(lane < GLA_GATE_RANK, glr, 0.0)
    z = jnp.dot(glr.astype(BF16), wu_ref[...], preferred_element_type=F32) + b_ref[...]
    log_a = (jnp.minimum(z, 0.0) - jnp.log1p(jnp.exp(-jnp.abs(z)))) / GLA_TAU
    r = lax.broadcasted_iota(jnp.int32, (tm, tm), 0)
    c = lax.broadcasted_iota(jnp.int32, (tm, tm), 1)
    tri = jnp.where((r >= c) & ((r // CHUNK) == (c // CHUNK)), 1.0, 0.0).astype(F32)
    o_ref[...] = jnp.dot(tri, log_a, preferred_element_type=F32, precision=lax.Precision.HIGHEST)


def _gate_call(x, w_all, w_up, b_gate, tm):
    t, d = x.shape
    n = w_up.shape[1]
    return pl.pallas_call(
        _gate_kernel,
        grid=(t // tm,),
        in_specs=[pl.BlockSpec((tm, d), lambda i: (i, 0)),
                  pl.BlockSpec((d, LANES), lambda i: (0, _C_RQ // LANES)),
                  pl.BlockSpec(w_up.shape, lambda i: (0, 0)),
                  pl.BlockSpec((1, n), lambda i: (0, 0))],
        out_specs=pl.BlockSpec((tm, n), lambda i: (i, 0)),
        out_shape=jax.ShapeDtypeStruct((t, n), F32),
        compiler_params=_params("parallel"),
        name="gla_gate",
    )(x, w_all, w_up, b_gate)


def _silu(x):
    return x * jax.nn.sigmoid(x)


def _mixer_kernel(p_ref, bc_ref, cos_ref, sin_ref, gg_ref, rg_ref, o_ref, sg_ref, sr_ref, *, n_chunks):
    @pl.when(pl.program_id(1) == 0)
    def _():
        sg_ref[...] = jnp.zeros_like(sg_ref)
        sr_ref[...] = jnp.zeros_like(sr_ref)

    ri = lax.broadcasted_iota(jnp.int32, (CHUNK, CHUNK), 0)
    ci = lax.broadcasted_iota(jnp.int32, (CHUNK, CHUNK), 1)
    lower = ri >= ci
    dist = jnp.abs(ri - ci).astype(F32)
    row = lax.broadcasted_iota(jnp.int32, (CHUNK, RET_DK), 0).astype(F32)
    mid = CHUNK // 2

    def chunk_body(c, carry):
        rows = pl.ds(pl.multiple_of(c * CHUNK, CHUNK), CHUNK)

        def col(ref, base, h, width):
            return ref[rows, base + h * width: base + (h + 1) * width]

        for h in range(GLA_HEADS):
            q = col(p_ref, _C_GQ, h, GLA_DK).astype(F32) * (GLA_DK ** -0.5)
            k = col(p_ref, _C_GK, h, GLA_DK).astype(F32)
            v = col(p_ref, _C_GV, h, GLA_DV)
            gate = col(p_ref, _C_GR, h, GLA_DV).astype(F32)
            bc = col(bc_ref, 0, h, GLA_DK)
            b_last = bc[CHUNK - 1:CHUNK, :]
            b_mid = bc[mid:mid + 1, :]
            e_fwd = jnp.exp(bc - b_mid)
            e_bwd = jnp.exp(b_mid - bc)
            st = sg_ref[h]
            o = lax.dot_general((q * jnp.exp(bc)).astype(BF16), st.astype(BF16), _NT,
                                preferred_element_type=F32)
            a_lo = lax.dot_general((q * e_fwd).astype(BF16), (k * e_bwd).astype(BF16), _NT,
                                   preferred_element_type=F32)
            a_up = lax.dot_general((q * e_bwd).astype(BF16), (k * e_fwd).astype(BF16), _NT,
                                   preferred_element_type=F32)
            a = jnp.where(lower, a_lo, a_up)
            o = o + jnp.dot(a.astype(BF16), v, preferred_element_type=F32)
            u_t = lax.dot_general(v, (k * jnp.exp(b_last - bc)).astype(BF16), _TN,
                                  preferred_element_type=F32)
            sg_ref[h] = st * jnp.exp(b_last) + u_t
            y = o * lax.rsqrt(jnp.mean(o * o, axis=-1, keepdims=True) + LN_EPS)
            y = y * gg_ref[:, h * GLA_DV:(h + 1) * GLA_DV] * _silu(gate)
            o_ref[rows, h * GLA_DV:(h + 1) * GLA_DV] = y.astype(o_ref.dtype)

        cos = cos_ref[rows, :]
        sin = sin_ref[rows, :]
        half = RET_DK // 2

        def rotary(t):
            t1, t2 = t[:, :half], t[:, half:]
            return jnp.concatenate([t1 * cos - t2 * sin, t1 * sin + t2 * cos], axis=-1)

        for h in range(RET_HEADS):
            log_gamma = math.log(1.0 - 2.0 ** (-5.0 - h))
            q = rotary(col(p_ref, _C_RQ, h, RET_DK).astype(F32))
            k = rotary(col(p_ref, _C_RK, h, RET_DK).astype(F32)) * (RET_DK ** -0.5)
            v = col(p_ref, _C_RV, h, RET_DV)
            gate = col(p_ref, _C_RR, h, RET_DV).astype(F32)
            qb = q.astype(BF16)
            st = sr_ref[h]
            o = lax.dot_general(qb, st.astype(BF16), _NT, preferred_element_type=F32)
            o = o * jnp.exp(log_gamma * (row + 1.0))
            a = lax.dot_general(qb, k.astype(BF16), _NT, preferred_element_type=F32)
            a = a * jnp.exp(log_gamma * dist)
            o = o + jnp.dot(a.astype(BF16), v, preferred_element_type=F32)
            k_dec = jnp.exp(log_gamma * (CHUNK - 1.0 - row))
            u_t = lax.dot_general(v, (k * k_dec).astype(BF16), _TN, preferred_element_type=F32)
            sr_ref[h] = st * math.exp(log_gamma * CHUNK) + u_t
            oc = o - jnp.mean(o, axis=-1, keepdims=True)
            y = oc * lax.rsqrt(jnp.mean(oc * oc, axis=-1, keepdims=True) + LN_EPS)
            y = y * rg_ref[:, h * RET_DV:(h + 1) * RET_DV] * _silu(gate)
            c0 = GLA_HEADS * GLA_DV + h * RET_DV
            o_ref[rows, c0:c0 + RET_DV] = y.astype(o_ref.dtype)
        return carry

    lax.fori_loop(0, n_chunks, chunk_body, 0)


def _mixer_call(proj, bc, cos, sin, gla_g, ret_g, batch, seq, n_chunks):
    t = proj.shape[0]
    tb = n_chunks * CHUNK
    groups = seq // tb
    width = GLA_HEADS * GLA_DV + RET_HEADS * RET_DV
    tok = lambda b, g: (b * groups + g, 0)
    return pl.pallas_call(
        functools.partial(_mixer_kernel, n_chunks=n_chunks),
        grid=(batch, groups),
        in_specs=[pl.BlockSpec((tb, proj.shape[1]), tok),
                  pl.BlockSpec((tb, bc.shape[1]), tok),
                  pl.BlockSpec((tb, cos.shape[1]), lambda b, g: (g, 0)),
                  pl.BlockSpec((tb, sin.shape[1]), lambda b, g: (g, 0)),
                  pl.BlockSpec(gla_g.shape, lambda b, g: (0, 0)),
                  pl.BlockSpec(ret_g.shape, lambda b, g: (0, 0))],
        out_specs=pl.BlockSpec((tb, width), tok),
        out_shape=jax.ShapeDtypeStruct((t, width), BF16),
        scratch_shapes=[pltpu.VMEM((GLA_HEADS, GLA_DV, GLA_DK), F32),
                        pltpu.VMEM((RET_HEADS, RET_DV, RET_DK), F32)],
        compiler_params=_params("parallel", "arbitrary"),
        name="mixer",
    )(proj, bc, cos, sin, gla_g, ret_g)


def _layer_norm(y, g, b):
    mu = jnp.mean(y, axis=-1, keepdims=True)
    yc = y - mu
    var = jnp.mean(yc * yc, axis=-1, keepdims=True)
    return yc * lax.rsqrt(var + LN_EPS) * g + b


def _out_ln_kernel(m_ref, w_ref, x_ref, g_ref, b_ref, h_ref, hb_ref):
    mix = jnp.dot(m_ref[...], w_ref[...], preferred_element_type=F32)
    h = _layer_norm(DEEPNORM_ALPHA * x_ref[...] + mix, g_ref[...], b_ref[...])
    h_ref[...] = h
    hb_ref[...] = h.astype(BF16)


def _out_ln_call(mix, w_out, x, g, b, tm):
    t, d = x.shape
    return pl.pallas_call(
        _out_ln_kernel,
        grid=(t // tm,),
        in_specs=[pl.BlockSpec((tm, mix.shape[1]), lambda i: (i, 0)),
                  pl.BlockSpec(w_out.shape, lambda i: (0, 0)),
                  pl.BlockSpec((tm, d), lambda i: (i, 0)),
                  pl.BlockSpec((1, d), lambda i: (0, 0)),
                  pl.BlockSpec((1, d), lambda i: (0, 0))],
        out_specs=[pl.BlockSpec((tm, d), lambda i: (i, 0)),
                   pl.BlockSpec((tm, d), lambda i: (i, 0))],
        out_shape=[jax.ShapeDtypeStruct((t, d), F32), jax.ShapeDtypeStruct((t, d), BF16)],
        compiler_params=_params("parallel"),
        name="out_proj_ln",
    )(mix, w_out, x, g, b)


def _topk_rows(s, k, payload=None):
    n = s.shape[0]
    iota = lax.broadcasted_iota(jnp.int32, s.shape, 0).astype(F32)
    vals, picks = [], []
    for _ in range(k):
        m = jnp.max(s, axis=0, keepdims=True)
        idx = jnp.min(jnp.where(s == m, iota, float(n)), axis=0, keepdims=True)
        hit = iota == idx
        vals.append(m)
        if payload is None:
            picks.append(idx)
        else:
            picks.append(jnp.sum(jnp.where(hit, payload, 0.0), axis=0, keepdims=True))
        s = jnp.where(hit, -jnp.inf, s)
    return jnp.concatenate(vals, axis=0), jnp.concatenate(picks, axis=0)


def _route_kernel(h_ref, wq_ref, sk_ref, a_ref, b_ref, g_ref):
    kk = PEER_TOPK
    q = jnp.dot(h_ref[...], wq_ref[...], preferred_element_type=F32).astype(BF16)
    n_tok = q.shape[0]
    experts, gates = [], []
    for h in range(PEER_HEADS):
        vals, idxs = [], []
        for p in range(2):
            j = 2 * h + p
            s_t = lax.dot_general(sk_ref[j], q[:, j * LANES:(j + 1) * LANES], _NT,
                                  preferred_element_type=F32)
            v_p, i_p = _topk_rows(s_t, kk)
            vals.append(v_p)
            idxs.append(i_p)
        cand, expert = [], []
        for r1 in range(kk):
            n2 = kk // (r1 + 1)
            cand.append(vals[0][r1:r1 + 1, :] + vals[1][:n2, :])
            expert.append(idxs[0][r1:r1 + 1, :] * float(PEER_NKEYS) + idxs[1][:n2, :])
        n_cand = sum(kk // (r1 + 1) for r1 in range(kk))
        pad = (-n_cand) % SUBLANES
        cand.append(jnp.full((pad, n_tok), -jnp.inf, F32))
        expert.append(jnp.zeros((pad, n_tok), F32))
        top_s, top_e = _topk_rows(jnp.concatenate(cand, axis=0), kk, jnp.concatenate(expert, axis=0))
        e = jnp.exp(top_s - top_s[0:1, :])
        experts.append(top_e)
        gates.append(e / jnp.sum(e, axis=0, keepdims=True))
    top_e = jnp.concatenate(experts, axis=0).T.astype(jnp.int32)
    a_ref[...] = top_e >> (PEER_NKEYS.bit_length() - 1)
    b_ref[...] = top_e & (PEER_NKEYS - 1)
    g_ref[...] = jnp.concatenate(gates, axis=0).T


def _route_call(hb, w_q, subkeys, tt):
    t, d = hb.shape
    n_sel = PEER_HEADS * PEER_TOPK
    out = lambda dt: jax.ShapeDtypeStruct((t, n_sel), dt)
    return pl.pallas_call(
        _route_kernel,
        grid=(t // tt,),
        in_specs=[pl.BlockSpec((tt, d), lambda i: (i, 0)),
                  pl.BlockSpec(w_q.shape, lambda i: (0, 0)),
                  pl.BlockSpec(subkeys.shape, lambda i: (0, 0, 0))],
        out_specs=[pl.BlockSpec((tt, n_sel), lambda i: (i, 0))] * 3,
        out_shape=[out(jnp.int32), out(jnp.int32), out(F32)],
        compiler_params=_params("parallel"),
        name="peer_route",
    )(hb, w_q, subkeys)


_WBUILD_UNROLL = 8
_W_ROWS = PEER_NKEYS // 2


def _wbuild_kernel(a_ref, b_ref, g_ref, w_ref):
    n_tok = a_ref.shape[0]
    n_sel = a_ref.shape[1]
    key = lax.broadcasted_iota(jnp.int32, (PEER_NKEYS, n_sel), 0)

    def token_groups(it, carry):
        for u in range(_WBUILD_UNROLL):
            tg = it * _WBUILD_UNROLL + u
            t0 = pl.multiple_of(tg * SUBLANES, SUBLANES)
            a8 = a_ref[pl.ds(t0, SUBLANES), :]
            b8 = b_ref[pl.ds(t0, SUBLANES), :]
            g8 = g_ref[pl.ds(t0, SUBLANES), :]
            for i in range(SUBLANES):
                pa = jnp.where(key == a8[i:i + 1, :], g8[i:i + 1, :], 0.0).astype(BF16)
                pb = jnp.where(key == b8[i:i + 1, :], 1.0, 0.0).astype(BF16)
                w = lax.dot_general(pa, pb, _NT, preferred_element_type=F32)
                w = w.reshape(PEER_NKEYS // (2 * SUBLANES), 2 * SUBLANES, PEER_NKEYS)
                lo = w[:, :SUBLANES, :].reshape(_W_ROWS, PEER_NKEYS)
                hi = w[:, SUBLANES:, :].reshape(_W_ROWS, PEER_NKEYS)
                words = lax.bitcast_convert_type(pltpu.pack_elementwise([lo, hi], packed_dtype=BF16), jnp.uint32)
                w_ref[tg, pl.ds(i, _W_ROWS, stride=SUBLANES), :] = words
        return carry

    lax.fori_loop(0, n_tok // (SUBLANES * _WBUILD_UNROLL), token_groups, 0)


def _wbuild_call(a_tm, b_tm, g_tm, tt):
    t, n_sel = a_tm.shape
    rows = _W_ROWS * SUBLANES
    return pl.pallas_call(
        _wbuild_kernel,
        grid=(t // tt,),
        in_specs=[pl.BlockSpec((tt, n_sel), lambda i: (i, 0))] * 3,
        out_specs=pl.BlockSpec((tt // SUBLANES, rows, PEER_NKEYS), lambda i: (i, 0, 0)),
        out_shape=jax.ShapeDtypeStruct((t // SUBLANES, rows, PEER_NKEYS), jnp.uint32),
        compiler_params=_params("parallel"),
        name="peer_gate_map",
    )(a_tm, b_tm, g_tm)


def _gelu(x):
    return 0.5 * x * (1.0 + lax.erf(x * math.sqrt(0.5)))


def _peer_kernel(hb_ref, u_ref, v_ref, w_ref, h_ref, g_ref, b_ref, o_ref, acc_ref, gs_ref, *, n_first):
    j = pl.program_id(1)

    @pl.when(j == 0)
    def _():
        acc_ref[...] = jnp.zeros_like(acc_ref)

    tm = hb_ref.shape[0]
    act = _gelu(lax.dot_general(hb_ref[...], u_ref[...], _NT, preferred_element_type=F32))
    odd = jnp.full((tm, PEER_NKEYS), j % 2, jnp.int32) == 1
    for a in range(n_first):
        word = w_ref[:, a, :, :].reshape(tm, PEER_NKEYS)
        w_a = jnp.where(odd,
                        pltpu.unpack_elementwise(word, index=1, packed_dtype=BF16, unpacked_dtype=F32),
                        pltpu.unpack_elementwise(word, index=0, packed_dtype=BF16, unpacked_dtype=F32))
        gs_ref[:, a * PEER_NKEYS:(a + 1) * PEER_NKEYS] = (
            w_a * act[:, a * PEER_NKEYS:(a + 1) * PEER_NKEYS]).astype(BF16)
    acc_ref[...] += jnp.dot(gs_ref[...], v_ref[...], preferred_element_type=F32)

    @pl.when(j == pl.num_programs(1) - 1)
    def _():
        o_ref[...] = _layer_norm(DEEPNORM_ALPHA * h_ref[...] + acc_ref[...], g_ref[...], b_ref[...])


def _peer_call(hb, u_tab, v_tab, w4, h, g, b, tm, n_first):
    t, d = h.shape
    n_exp = v_tab.shape[0]
    assert n_first == SUBLANES, "gate-map words pair first keys 16m+r / 16m+8+r: two 8-key tiles per word block"
    tn = n_first * PEER_NKEYS
    return pl.pallas_call(
        functools.partial(_peer_kernel, n_first=n_first),
        grid=(t // tm, n_exp // tn),
        in_specs=[pl.BlockSpec((tm, d), lambda i, j: (i, 0)),
                  pl.BlockSpec((tn, d), lambda i, j: (j, 0)),
                  pl.BlockSpec((tn, d), lambda i, j: (j, 0)),
                  pl.BlockSpec((tm // SUBLANES, n_first, SUBLANES, PEER_NKEYS), lambda i, j: (i, j // 2, 0, 0)),
                  pl.BlockSpec((tm, d), lambda i, j: (i, 0)),
                  pl.BlockSpec((1, d), lambda i, j: (0, 0)),
                  pl.BlockSpec((1, d), lambda i, j: (0, 0))],
        out_specs=pl.BlockSpec((tm, d), lambda i, j: (i, 0)),
        out_shape=jax.ShapeDtypeStruct((t, d), F32),
        scratch_shapes=[pltpu.VMEM((tm, d), F32), pltpu.VMEM((tm, tn), BF16)],
        compiler_params=_params("parallel", "arbitrary"),
        name="peer_dense",
    )(hb, u_tab, v_tab, w4, h, g, b)


def _layer(x2, batch, seq, w_in, w_gate_up, b_gate, gla_g, ret_g, w_out, ln1_g, ln1_b,
           w_q, subkeys, u_tab, v_tab, ln2_g, ln2_b):
    t, d = x2.shape
    row = lambda p: p.reshape(1, -1).astype(F32)

    w_all = w_in.astype(BF16)
    w_ret = w_all[:, _C_RQ + GLA_GATE_RANK:]
    w_up = jnp.pad(w_gate_up, ((0, LANES - GLA_GATE_RANK), (0, 0))).astype(BF16)

    proj = _proj_call(x2, w_all, w_ret, min(1024, t), 1024)
    bc = _gate_call(x2, w_all, w_up, row(b_gate), min(256, t))

    half = RET_DK // 2
    inv = ROPE_BASE ** (-jnp.arange(half, dtype=F32) / half)
    ang = jnp.arange(seq).astype(F32)[:, None] * inv[None, :]
    n_chunks = min(8, seq // CHUNK)
    mix = _mixer_call(proj, bc, jnp.cos(ang), jnp.sin(ang), row(gla_g), row(ret_g), batch, seq, n_chunks)

    h, hb = _out_ln_call(mix, w_out.astype(BF16), x2, row(ln1_g), row(ln1_b), min(512, t))

    sk = subkeys.reshape(PEER_HEADS * 2, PEER_NKEYS, -1).astype(BF16)
    a_sel, b_sel, gates = _route_call(hb, w_q.astype(BF16), sk, min(256, t))
    w3 = _wbuild_call(a_sel, b_sel, gates, min(256, t))
    w4 = w3.reshape(t // SUBLANES, _W_ROWS, SUBLANES, PEER_NKEYS)

    return _peer_call(hb, u_tab.astype(BF16), v_tab.astype(BF16), w4, h, row(ln2_g), row(ln2_b),
                      min(512, t), 8)


def kernel(x, w_in, w_gla_gate_up, b_gla_gate, gla_norm_g, ret_norm_g, w_out, ln1_g, ln1_b,
           w_peer_q, peer_subkeys, peer_u, peer_v, ln2_g, ln2_b):
    batch, seq, d = x.shape
    h = x.reshape(batch * seq, d)
    for l in range(DEPTH):
        h = _layer(h, batch, seq, w_in[l], w_gla_gate_up[l], b_gla_gate[l], gla_norm_g[l], ret_norm_g[l],
                   w_out[l], ln1_g[l], ln1_b[l], w_peer_q[l], peer_subkeys[l], peer_u[l], peer_v[l],
                   ln2_g[l], ln2_b[l])
    return h.reshape(batch, seq, d)
```

```python
import functools
import math

import jax
import jax.numpy as jnp
from jax import lax
from jax.experimental import pallas as pl
from jax.experimental.pallas import tpu as pltpu

F32 = jnp.float32
BF16 = jnp.bfloat16

DEPTH = 1
CHUNK = 64
GLA_HEADS = 4
GLA_DK = 128
GLA_DV = 256
GLA_GATE_RANK = 16
GLA_TAU = 16.0
RET_HEADS = 4
RET_DK = 256
RET_DV = 256
ROPE_BASE = 10000.0
PEER_HEADS = 8
PEER_NKEYS = 128
PEER_TOPK = 16
LN_EPS = 1e-5
DEEPNORM_ALPHA = (2 * DEPTH) ** 0.25

LANES = 128
SUBLANES = 8
VMEM_LIMIT = 56 * 1024 * 1024

_C_GQ = 0
_C_GK = _C_GQ + GLA_HEADS * GLA_DK
_C_GV = _C_GK + GLA_HEADS * GLA_DK
_C_GR = _C_GV + GLA_HEADS * GLA_DV
_C_RQ = _C_GR + GLA_HEADS * GLA_DV
_C_RK = _C_RQ + RET_HEADS * RET_DK
_C_RV = _C_RK + RET_HEADS * RET_DK
_C_RR = _C_RV + RET_HEADS * RET_DV
_C_END = _C_RR + RET_HEADS * RET_DV

_NT = (((1,), (1,)), ((), ()))
_TN = (((0,), (0,)), ((), ()))


def _params(*sem):
    return pltpu.CompilerParams(dimension_semantics=sem, vmem_limit_bytes=VMEM_LIMIT)


def _proj_kernel(x_ref, wg_ref, wr_ref, o_ref, xb_ref, *, n_gla_tiles):
    j = pl.program_id(1)

    @pl.when(j == 0)
    def _():
        xb_ref[...] = x_ref[...].astype(BF16)

    @pl.when(j < n_gla_tiles)
    def _():
        o_ref[...] = jnp.dot(xb_ref[...], wg_ref[...], preferred_element_type=F32).astype(o_ref.dtype)

    @pl.when(j >= n_gla_tiles)
    def _():
        o_ref[...] = jnp.dot(xb_ref[...], wr_ref[...], preferred_element_type=F32).astype(o_ref.dtype)


def _proj_call(x, w_all, w_ret, tm, tn):
    m, k = x.shape
    n_gla_tiles = _C_RQ // tn
    n_ret_tiles = w_ret.shape[1] // tn
    return pl.pallas_call(
        functools.partial(_proj_kernel, n_gla_tiles=n_gla_tiles),
        grid=(m // tm, n_gla_tiles + n_ret_tiles),
        in_specs=[pl.BlockSpec((tm, k), lambda i, j: (i, 0)),
                  pl.BlockSpec((k, tn), lambda i, j: (0, jnp.minimum(j, n_gla_tiles - 1))),
                  pl.BlockSpec((k, tn), lambda i, j: (0, jnp.maximum(j - n_gla_tiles, 0)))],
        out_specs=pl.BlockSpec((tm, tn), lambda i, j: (i, j)),
        out_shape=jax.ShapeDtypeStruct((m, _C_END), BF16),
        scratch_shapes=[pltpu.VMEM((tm, k), BF16)],
        compiler_params=_params("parallel", "arbitrary"),
        name="proj_matmul",
    )(x, w_all, w_ret)


def _gate_kernel(x_ref, wl_ref, wu_ref, b_ref, o_ref):
    tm = x_ref.shape[0]
    glr = jnp.dot(x_ref[...].astype(BF16), wl_ref[...], preferred_element_type=F32)
    lane = lax.broadcasted_iota(jnp.int32, glr.shape, 1)
    glr = jnp.where(lane < GLA_GATE_RANK, glr, 0.0)
    z = jnp.dot(glr.astype(BF16), wu_ref[...], preferred_element_type=F32) + b_ref[...]
    log_a = (jnp.minimum(z, 0.0) - jnp.log1p(jnp.exp(-jnp.abs(z)))) / GLA_TAU
    r = lax.broadcasted_iota(jnp.int32, (tm, tm), 0)
    c = lax.broadcasted_iota(jnp.int32, (tm, tm), 1)
    tri = jnp.where((r >= c) & ((r // CHUNK) == (c // CHUNK)), 1.0, 0.0).astype(F32)
    o_ref[...] = jnp.dot(tri, log_a, preferred_element_type=F32, precision=lax.Precision.HIGHEST)


def _gate_call(x, w_all, w_up, b_gate, tm):
    t, d = x.shape
    n = w_up.shape[1]
    return pl.pallas_call(
        _gate_kernel,
        grid=(t // tm,),
        in_specs=[pl.BlockSpec((tm, d), lambda i: (i, 0)),
                  pl.BlockSpec((d, LANES), lambda i: (0, _C_RQ // LANES)),
                  pl.BlockSpec(w_up.shape, lambda i: (0, 0)),
                  pl.BlockSpec((1, n), lambda i: (0, 0))],
        out_specs=pl.BlockSpec((tm, n), lambda i: (i, 0)),
        out_shape=jax.ShapeDtypeStruct((t, n), F32),
        compiler_params=_params("parallel"),
        name="gla_gate",
    )(x, w_all, w_up, b_gate)


def _silu(x):
    return x * jax.nn.sigmoid(x)


def _mixer_kernel(p_ref, bc_ref, cos_ref, sin_ref, gg_ref, rg_ref, o_ref, sg_ref, sr_ref, *, n_chunks):
    @pl.when(pl.program_id(1) == 0)
    def _():
        sg_ref[...] = jnp.zeros_like(sg_ref)
        sr_ref[...] = jnp.zeros_like(sr_ref)

    ri = lax.broadcasted_iota(jnp.int32, (CHUNK, CHUNK), 0)
    ci = lax.broadcasted_iota(jnp.int32, (CHUNK, CHUNK), 1)
    lower = ri >= ci
    dist = jnp.abs(ri - ci).astype(F32)
    row = lax.broadcasted_iota(jnp.int32, (CHUNK, RET_DK), 0).astype(F32)
    mid = CHUNK // 2

    def chunk_body(c, carry):
        rows = pl.ds(pl.multiple_of(c * CHUNK, CHUNK), CHUNK)

        def col(ref, base, h, width):
            return ref[rows, base + h * width: base + (h + 1) * width]

        for h in range(GLA_HEADS):
            q = col(p_ref, _C_GQ, h, GLA_DK).astype(F32) * (GLA_DK ** -0.5)
            k = col(p_ref, _C_GK, h, GLA_DK).astype(F32)
            v = col(p_ref, _C_GV, h, GLA_DV)
            gate = col(p_ref, _C_GR, h, GLA_DV).astype(F32)
            bc = col(bc_ref, 0, h, GLA_DK)
            b_last = bc[CHUNK - 1:CHUNK, :]
            b_mid = bc[mid:mid + 1, :]
            e_fwd = jnp.exp(bc - b_mid)
            e_bwd = jnp.exp(b_mid - bc)
            st = sg_ref[h]
            o = lax.dot_general((q * jnp.exp(bc)).astype(BF16), st.astype(BF16), _NT,
                                preferred_element_type=F32)
            a_lo = lax.dot_general((q * e_fwd).astype(BF16), (k * e_bwd).astype(BF16), _NT,
                                   preferred_element_type=F32)
            a_up = lax.dot_general((q * e_bwd).astype(BF16), (k * e_fwd).astype(BF16), _NT,
                                   preferred_element_type=F32)
            a = jnp.where(lower, a_lo, a_up)
            o = o + jnp.dot(a.astype(BF16), v, preferred_element_type=F32)
            u_t = lax.dot_general(v, (k * jnp.exp(b_last - bc)).astype(BF16), _TN,
                                  preferred_element_type=F32)
            sg_ref[h] = st * jnp.exp(b_last) + u_t
            y = o * lax.rsqrt(jnp.mean(o * o, axis=-1, keepdims=True) + LN_EPS)
            y = y * gg_ref[:, h * GLA_DV:(h + 1) * GLA_DV] * _silu(gate)
            o_ref[rows, h * GLA_DV:(h + 1) * GLA_DV] = y.astype(o_ref.dtype)

        cos = cos_ref[rows, :]
        sin = sin_ref[rows, :]
        half = RET_DK // 2

        def rotary(t):
            t1, t2 = t[:, :half], t[:, half:]
            return jnp.concatenate([t1 * cos - t2 * sin, t1 * sin + t2 * cos], axis=-1)

        for h in range(RET_HEADS):
            log_gamma = math.log(1.0 - 2.0 ** (-5.0 - h))
            q = rotary(col(p_ref, _C_RQ, h, RET_DK).astype(F32))
            k = rotary(col(p_ref, _C_RK, h, RET_DK).astype(F32)) * (RET_DK ** -0.5)
            v = col(p_ref, _C_RV, h, RET_DV)
            gate = col(p_ref, _C_RR, h, RET_DV).astype(F32)
            qb = q.astype(BF16)
            st = sr_ref[h]
            o = lax.dot_general(qb, st.astype(BF16), _NT, preferred_element_type=F32)
            o = o * jnp.exp(log_gamma * (row + 1.0))
            a = lax.dot_general(qb, k.astype(BF16), _NT, preferred_element_type=F32)
            a = a * jnp.exp(log_gamma * dist)
            o = o + jnp.dot(a.astype(BF16), v, preferred_element_type=F32)
            k_dec = jnp.exp(log_gamma * (CHUNK - 1.0 - row))
            u_t = lax.dot_general(v, (k * k_dec).astype(BF16), _TN, preferred_element_type=F32)
            sr_ref[h] = st * math.exp(log_gamma * CHUNK) + u_t
            oc = o - jnp.mean(o, axis=-1, keepdims=True)
            y = oc * lax.rsqrt(jnp.mean(oc * oc, axis=-1, keepdims=True) + LN_EPS)
            y = y * rg_ref[:, h * RET_DV:(h + 1) * RET_DV] * _silu(gate)
            c0 = GLA_HEADS * GLA_DV + h * RET_DV
            o_ref[rows, c0:c0 + RET_DV] = y.astype(o_ref.dtype)
        return carry

    lax.fori_loop(0, n_chunks, chunk_body, 0)


def _mixer_call(proj, bc, cos, sin, gla_g, ret_g, batch, seq, n_chunks):
    t = proj.shape[0]
    tb = n_chunks * CHUNK
    groups = seq // tb
    width = GLA_HEADS * GLA_DV + RET_HEADS * RET_DV
    tok = lambda b, g: (b * groups + g, 0)
    return pl.pallas_call(
        functools.partial(_mixer_kernel, n_chunks=n_chunks),
        grid=(batch, groups),
        in_specs=[pl.BlockSpec((tb, proj.shape[1]), tok),
                  pl.BlockSpec((tb, bc.shape[1]), tok),
                  pl.BlockSpec((tb, cos.shape[1]), lambda b, g: (g, 0)),
                  pl.BlockSpec((tb, sin.shape[1]), lambda b, g: (g, 0)),
                  pl.BlockSpec(gla_g.shape, lambda b, g: (0, 0)),
                  pl.BlockSpec(ret_g.shape, lambda b, g: (0, 0))],
        out_specs=pl.BlockSpec((tb, width), tok),
        out_shape=jax.ShapeDtypeStruct((t, width), BF16),
        scratch_shapes=[pltpu.VMEM((GLA_HEADS, GLA_DV, GLA_DK), F32),
                        pltpu.VMEM((RET_HEADS, RET_DV, RET_DK), F32)],
        compiler_params=_params("parallel", "arbitrary"),
        name="mixer",
    )(proj, bc, cos, sin, gla_g, ret_g)


def _layer_norm(y, g, b):
    mu = jnp.mean(y, axis=-1, keepdims=True)
    yc = y - mu
    var = jnp.mean(yc * yc, axis=-1, keepdims=True)
    return yc * lax.rsqrt(var + LN_EPS) * g + b


def _out_ln_kernel(m_ref, w_ref, x_ref, g_ref, b_ref, h_ref, hb_ref):
    mix = jnp.dot(m_ref[...], w_ref[...], preferred_element_type=F32)
    h = _layer_norm(DEEPNORM_ALPHA * x_ref[...] + mix, g_ref[...], b_ref[...])
    h_ref[...] = h
    hb_ref[...] = h.astype(BF16)


def _out_ln_call(mix, w_out, x, g, b, tm):
    t, d = x.shape
    return pl.pallas_call(
        _out_ln_kernel,
        grid=(t // tm,),
        in_specs=[pl.BlockSpec((tm, mix.shape[1]), lambda i: (i, 0)),
                  pl.BlockSpec(w_out.shape, lambda i: (0, 0)),
                  pl.BlockSpec((tm, d), lambda i: (i, 0)),
                  pl.BlockSpec((1, d), lambda i: (0, 0)),
                  pl.BlockSpec((1, d), lambda i: (0, 0))],
        out_specs=[pl.BlockSpec((tm, d), lambda i: (i, 0)),
                   pl.BlockSpec((tm, d), lambda i: (i, 0))],
        out_shape=[jax.ShapeDtypeStruct((t, d), F32), jax.ShapeDtypeStruct((t, d), BF16)],
        compiler_params=_params("parallel"),
        name="out_proj_ln",
    )(mix, w_out, x, g, b)


def _topk_rows(s, k, payload=None):
    n = s.shape[0]
    iota = lax.broadcasted_iota(jnp.int32, s.shape, 0).astype(F32)
    vals, picks = [], []
    for _ in range(k):
        m = jnp.max(s, axis=0, keepdims=True)
        idx = jnp.min(jnp.where(s == m, iota, float(n)), axis=0, keepdims=True)
        hit = iota == idx
        vals.append(m)
        if payload is None:
            picks.append(idx)
        else:
            picks.append(jnp.sum(jnp.where(hit, payload, 0.0), axis=0, keepdims=True))
        s = jnp.where(hit, -jnp.inf, s)
    return jnp.concatenate(vals, axis=0), jnp.concatenate(picks, axis=0)


def _route_kernel(h_ref, wq_ref, sk_ref, u_ref, v_ref, a_ref, b_ref, bt_ref, g_ref, ub_ref, vb_ref):
    ub_ref[...] = u_ref[...].astype(BF16)
    vb_ref[...] = v_ref[...].astype(BF16)
    kk = PEER_TOPK
    q = jnp.dot(h_ref[...], wq_ref[...], preferred_element_type=F32).astype(BF16)
    n_tok = q.shape[0]
    experts, gates = [], []
    for h in range(PEER_HEADS):
        vals, idxs = [], []
        for p in range(2):
            j = 2 * h + p
            s_t = lax.dot_general(sk_ref[j], q[:, j * LANES:(j + 1) * LANES], _NT,
                                  preferred_element_type=F32)
            v_p, i_p = _topk_rows(s_t, kk)
            vals.append(v_p)
            idxs.append(i_p)
        cand, expert = [], []
        for r1 in range(kk):
            n2 = kk // (r1 + 1)
            cand.append(vals[0][r1:r1 + 1, :] + vals[1][:n2, :])
            expert.append(idxs[0][r1:r1 + 1, :] * float(PEER_NKEYS) + idxs[1][:n2, :])
        n_cand = sum(kk // (r1 + 1) for r1 in range(kk))
        pad = (-n_cand) % SUBLANES
        cand.append(jnp.full((pad, n_tok), -jnp.inf, F32))
        expert.append(jnp.zeros((pad, n_tok), F32))
        top_s, top_e = _topk_rows(jnp.concatenate(cand, axis=0), kk, jnp.concatenate(expert, axis=0))
        e = jnp.exp(top_s - top_s[0:1, :])
        experts.append(top_e)
        gates.append(e / jnp.sum(e, axis=0, keepdims=True))
    top_e = jnp.concatenate(experts, axis=0).astype(jnp.int32)
    a_ref[...] = (top_e >> (PEER_NKEYS.bit_length() - 1)).T
    second = top_e & (PEER_NKEYS - 1)
    b_ref[...] = second.T
    bt_ref[...] = second
    g_ref[...] = jnp.concatenate(gates, axis=0).T


def _route_call(hb, w_q, subkeys, u_tab, v_tab, tt):
    t, d = hb.shape
    n_sel = PEER_HEADS * PEER_TOPK
    n_steps = t // tt
    n_exp, d_exp = u_tab.shape
    slab = n_exp // n_steps
    assert slab * n_steps == n_exp and slab % (2 * SUBLANES) == 0
    tok_major = pl.BlockSpec((tt, n_sel), lambda i: (i, 0))
    table = pl.BlockSpec((slab, d_exp), lambda i: (i, 0))
    return pl.pallas_call(
        _route_kernel,
        grid=(n_steps,),
        in_specs=[pl.BlockSpec((tt, d), lambda i: (i, 0)),
                  pl.BlockSpec(w_q.shape, lambda i: (0, 0)),
                  pl.BlockSpec(subkeys.shape, lambda i: (0, 0, 0)),
                  table, table],
        out_specs=[tok_major, tok_major, pl.BlockSpec((n_sel, tt), lambda i: (0, i)), tok_major, table, table],
        out_shape=[jax.ShapeDtypeStruct((t, n_sel), jnp.int32), jax.ShapeDtypeStruct((t, n_sel), jnp.int32),
                   jax.ShapeDtypeStruct((n_sel, t), jnp.int32), jax.ShapeDtypeStruct((t, n_sel), F32),
                   jax.ShapeDtypeStruct(u_tab.shape, BF16), jax.ShapeDtypeStruct(v_tab.shape, BF16)],
        compiler_params=_params("parallel"),
        name="peer_route",
    )(hb, w_q, subkeys, u_tab, v_tab)


_W_ROWS = PEER_NKEYS // 2


def _wbuild_kernel(a_ref, b_ref, bt_ref, g_ref, w_ref):
    n_tok, n_sel = a_ref.shape
    key_rows = lax.broadcasted_iota(jnp.int32, (PEER_NKEYS, n_sel), 0)
    key_cols = lax.broadcasted_iota(jnp.int32, (n_sel, PEER_NKEYS), 1)
    for t in range(n_tok):
        pa = jnp.where(key_rows == a_ref[t:t + 1, :], g_ref[t:t + 1, :], 0.0).astype(BF16)
        if t % 2:
            pb = jnp.where(key_cols == bt_ref[:, t:t + 1], 1.0, 0.0).astype(BF16)
            w = jnp.dot(pa, pb, preferred_element_type=F32)
        else:
            pb = jnp.where(key_rows == b_ref[t:t + 1, :], 1.0, 0.0).astype(BF16)
            w = lax.dot_general(pa, pb, _NT, preferred_element_type=F32)
        words = pltpu.pack_elementwise([w[:_W_ROWS, :], w[_W_ROWS:, :]], packed_dtype=BF16)
        w_ref[t // SUBLANES, pl.ds(t % SUBLANES, _W_ROWS, stride=SUBLANES), :] = (
            lax.bitcast_convert_type(words, jnp.uint32))


def _wbuild_call(a_tm, b_tm, b_sm, g_tm, tt):
    t, n_sel = a_tm.shape
    rows = _W_ROWS * SUBLANES
    tok_major = pl.BlockSpec((tt, n_sel), lambda i: (i, 0))
    return pl.pallas_call(
        _wbuild_kernel,
        grid=(t // tt,),
        in_specs=[tok_major, tok_major, pl.BlockSpec((n_sel, tt), lambda i: (0, i)), tok_major],
        out_specs=pl.BlockSpec((tt // SUBLANES, rows, PEER_NKEYS), lambda i: (i, 0, 0)),
        out_shape=jax.ShapeDtypeStruct((t // SUBLANES, rows, PEER_NKEYS), jnp.uint32),
        compiler_params=_params("parallel"),
        name="peer_gate_map",
    )(a_tm, b_tm, b_sm, g_tm)


def _gelu(x):
    return 0.5 * x * (1.0 + lax.erf(x * math.sqrt(0.5)))


def _peer_kernel(hb_ref, ul_ref, uh_ref, vl_ref, vh_ref, w_ref, h_ref, g_ref, b_ref, o_ref, acc_ref, gs_ref):
    j = pl.program_id(1)

    @pl.when(j == 0)
    def _():
        acc_ref[...] = jnp.zeros_like(acc_ref)

    tm = hb_ref.shape[0]
    half = ul_ref.shape[0]
    hb = hb_ref[...]
    for part, u_ref in enumerate((ul_ref, uh_ref)):
        act = _gelu(lax.dot_general(hb, u_ref[...], _NT, preferred_element_type=F32))
        for a in range(half // PEER_NKEYS):
            word = w_ref[:, a, :, :].reshape(tm, PEER_NKEYS)
            w_a = pltpu.unpack_elementwise(word, index=part, packed_dtype=BF16, unpacked_dtype=F32)
            cols = slice(a * PEER_NKEYS, (a + 1) * PEER_NKEYS)
            gs_ref[part, :, cols] = (w_a * act[:, cols]).astype(BF16)
    acc_ref[...] += (jnp.dot(gs_ref[0], vl_ref[...], preferred_element_type=F32)
                     + jnp.dot(gs_ref[1], vh_ref[...], preferred_element_type=F32))

    @pl.when(j == pl.num_programs(1) - 1)
    def _():
        o_ref[...] = _layer_norm(DEEPNORM_ALPHA * h_ref[...] + acc_ref[...], g_ref[...], b_ref[...])


def _peer_call(hb, u_tab, v_tab, w4, h, g, b, tm, tn):
    t, d = h.shape
    n_exp = v_tab.shape[0]
    half = tn // 2
    n_steps = n_exp // tn
    lo = pl.BlockSpec((half, d), lambda i, j: (j, 0))
    hi = pl.BlockSpec((half, d), lambda i, j: (j + n_steps, 0))
    tok = pl.BlockSpec((tm, d), lambda i, j: (i, 0))
    vec = pl.BlockSpec((1, d), lambda i, j: (0, 0))
    return pl.pallas_call(
        _peer_kernel,
        grid=(t // tm, n_steps),
        in_specs=[tok, lo, hi, lo, hi,
                  pl.BlockSpec((tm // SUBLANES, half // PEER_NKEYS, SUBLANES, PEER_NKEYS),
                               lambda i, j: (i, j, 0, 0)),
                  tok, vec, vec],
        out_specs=tok,
        out_shape=jax.ShapeDtypeStruct((t, d), F32),
        scratch_shapes=[pltpu.VMEM((tm, d), F32), pltpu.VMEM((2, tm, half), BF16)],
        compiler_params=_params("parallel", "arbitrary"),
        name="peer_dense",
    )(hb, u_tab, u_tab, v_tab, v_tab, w4, h, g, b)


def _layer(x2, batch, seq, w_in, w_gate_up, b_gate, gla_g, ret_g, w_out, ln1_g, ln1_b,
           w_q, subkeys, u_tab, v_tab, ln2_g, ln2_b):
    t, d = x2.shape
    row = lambda p: p.reshape(1, -1).astype(F32)

    w_all = w_in.astype(BF16)
    w_ret = w_all[:, _C_RQ + GLA_GATE_RANK:]
    w_up = jnp.pad(w_gate_up, ((0, LANES - GLA_GATE_RANK), (0, 0))).astype(BF16)

    proj = _proj_call(x2, w_all, w_ret, min(1024, t), 1024)
    bc = _gate_call(x2, w_all, w_up, row(b_gate), min(256, t))

    half = RET_DK // 2
    inv = ROPE_BASE ** (-jnp.arange(half, dtype=F32) / half)
    ang = jnp.arange(seq).astype(F32)[:, None] * inv[None, :]
    n_chunks = min(8, seq // CHUNK)
    mix = _mixer_call(proj, bc, jnp.cos(ang), jnp.sin(ang), row(gla_g), row(ret_g), batch, seq, n_chunks)

    h, hb = _out_ln_call(mix, w_out.astype(BF16), x2, row(ln1_g), row(ln1_b), min(512, t))

    sk = subkeys.reshape(PEER_HEADS * 2, PEER_NKEYS, -1).astype(BF16)
    a_sel, b_sel, bt_sel, gates, u_b, v_b = _route_call(hb, w_q.astype(BF16), sk, u_tab, v_tab, min(256, t))
    w3 = _wbuild_call(a_sel, b_sel, bt_sel, gates, min(128, t))
    w4 = w3.reshape(t // SUBLANES, _W_ROWS, SUBLANES, PEER_NKEYS)

    return _peer_call(hb, u_b, v_b, w4, h, row(ln2_g), row(ln2_b),
                      min(512, t), 1024)


def kernel(x, w_in, w_gla_gate_up, b_gla_gate, gla_norm_g, ret_norm_g, w_out, ln1_g, ln1_b,
           w_peer_q, peer_subkeys, peer_u, peer_v, ln2_g, ln2_b):
    batch, seq, d = x.shape
    h = x.reshape(batch * seq, d)
    for l in range(DEPTH):
        h = _layer(h, batch, seq, w_in[l], w_gla_gate_up[l], b_gla_gate[l], gla_norm_g[l], ret_norm_g[l],
                   w_out[l], ln1_g[l], ln1_b[l], w_peer_q[l], peer_subkeys[l], peer_u[l], peer_v[l],
                   ln2_g[l], ln2_b[l])
    return h.reshape(batch, seq, d)
```

```python
import functools
import math

import jax
import jax.numpy as jnp
from jax import lax
from jax.experimental import pallas as pl
from jax.experimental.pallas import tpu as pltpu

F32 = jnp.float32
BF16 = jnp.bfloat16

DEPTH = 1
CHUNK = 64
GLA_HEADS = 4
GLA_DK = 128
GLA_DV = 256
GLA_GATE_RANK = 16
GLA_TAU = 16.0
RET_HEADS = 4
RET_DK = 256
RET_DV = 256
ROPE_BASE = 10000.0
PEER_HEADS = 8
PEER_NKEYS = 128
PEER_TOPK = 16
LN_EPS = 1e-5
DEEPNORM_ALPHA = (2 * DEPTH) ** 0.25

LANES = 128
SUBLANES = 8
VMEM_LIMIT = 56 * 1024 * 1024

_C_GQ = 0
_C_GK = _C_GQ + GLA_HEADS * GLA_DK
_C_GV = _C_GK + GLA_HEADS * GLA_DK
_C_GR = _C_GV + GLA_HEADS * GLA_DV
_C_RQ = _C_GR + GLA_HEADS * GLA_DV
_C_RK = _C_RQ + RET_HEADS * RET_DK
_C_RV = _C_RK + RET_HEADS * RET_DK
_C_RR = _C_RV + RET_HEADS * RET_DV
_C_END = _C_RR + RET_HEADS * RET_DV

_NT = (((1,), (1,)), ((), ()))
_TN = (((0,), (0,)), ((), ()))


def _params(*sem):
    return pltpu.CompilerParams(dimension_semantics=sem, vmem_limit_bytes=VMEM_LIMIT)


def _gate_kernel(x_ref, wl_ref, ws_ref, wu_ref, b_ref, o_ref, xb_ref, wb_ref):
    tm = x_ref.shape[0]
    xb = x_ref[...].astype(BF16)
    xb_ref[...] = xb
    wb_ref[...] = ws_ref[...].astype(BF16)
    glr = jnp.dot(xb, wl_ref[...].astype(BF16), preferred_element_type=F32)
    lane = lax.broadcasted_iota(jnp.int32, glr.shape, 1)
    glr = jnp.where(lane < GLA_GATE_RANK, glr, 0.0)
    z = jnp.dot(glr.astype(BF16), wu_ref[...], preferred_element_type=F32) + b_ref[...]
    log_a = (jnp.minimum(z, 0.0) - jnp.log1p(jnp.exp(-jnp.abs(z)))) / GLA_TAU
    r = lax.broadcasted_iota(jnp.int32, (tm, tm), 0)
    c = lax.broadcasted_iota(jnp.int32, (tm, tm), 1)
    tri = jnp.where((r >= c) & ((r // CHUNK) == (c // CHUNK)), 1.0, 0.0).astype(F32)
    o_ref[...] = jnp.dot(tri, log_a, preferred_element_type=F32, precision=lax.Precision.HIGHEST)


def _gate_call(x, w_in, w_up, b_gate, tm):
    t, d = x.shape
    n = w_up.shape[1]
    n_steps = t // tm
    slab = d // n_steps
    assert slab * n_steps == d and slab % (2 * SUBLANES) == 0
    return pl.pallas_call(
        _gate_kernel,
        grid=(n_steps,),
        in_specs=[pl.BlockSpec((tm, d), lambda i: (i, 0)),
                  pl.BlockSpec((d, LANES), lambda i: (0, _C_RQ // LANES)),
                  pl.BlockSpec((slab, w_in.shape[1]), lambda i: (i, 0)),
                  pl.BlockSpec(w_up.shape, lambda i: (0, 0)),
                  pl.BlockSpec((1, n), lambda i: (0, 0))],
        out_specs=[pl.BlockSpec((tm, n), lambda i: (i, 0)),
                   pl.BlockSpec((tm, d), lambda i: (i, 0)),
                   pl.BlockSpec((slab, w_in.shape[1]), lambda i: (i, 0))],
        out_shape=[jax.ShapeDtypeStruct((t, n), F32), jax.ShapeDtypeStruct((t, d), BF16),
                   jax.ShapeDtypeStruct(w_in.shape, BF16)],
        compiler_params=_params("parallel"),
        name="gla_gate",
    )(x, w_in, w_in, w_up, b_gate)


def _proj_kernel(x_ref, wg_ref, wr_ref, o_ref, *, n_gla_tiles):
    j = pl.program_id(1)

    @pl.when(j < n_gla_tiles)
    def _():
        o_ref[...] = jnp.dot(x_ref[...], wg_ref[...], preferred_element_type=F32).astype(o_ref.dtype)

    @pl.when(j >= n_gla_tiles)
    def _():
        o_ref[...] = jnp.dot(x_ref[...], wr_ref[...], preferred_element_type=F32).astype(o_ref.dtype)


def _proj_call(xb, w_all, w_ret, tm, tn):
    m, k = xb.shape
    n_gla_tiles = _C_RQ // tn
    n_ret_tiles = w_ret.shape[1] // tn
    return pl.pallas_call(
        functools.partial(_proj_kernel, n_gla_tiles=n_gla_tiles),
        grid=(m // tm, n_gla_tiles + n_ret_tiles),
        in_specs=[pl.BlockSpec((tm, k), lambda i, j: (i, 0)),
                  pl.BlockSpec((k, tn), lambda i, j: (0, jnp.minimum(j, n_gla_tiles - 1))),
                  pl.BlockSpec((k, tn), lambda i, j: (0, jnp.maximum(j - n_gla_tiles, 0)))],
        out_specs=pl.BlockSpec((tm, tn), lambda i, j: (i, j)),
        out_shape=jax.ShapeDtypeStruct((m, _C_END), BF16),
        compiler_params=_params("parallel", "parallel"),
        name="proj_matmul",
    )(xb, w_all, w_ret)


def _silu(x):
    return x * jax.nn.sigmoid(x)


def _mixer_kernel(p_ref, bc_ref, cos_ref, sin_ref, gg_ref, rg_ref, o_ref, sg_ref, sr_ref, *, n_chunks):
    @pl.when(pl.program_id(1) == 0)
    def _():
        sg_ref[...] = jnp.zeros_like(sg_ref)
        sr_ref[...] = jnp.zeros_like(sr_ref)

    ri = lax.broadcasted_iota(jnp.int32, (CHUNK, CHUNK), 0)
    ci = lax.broadcasted_iota(jnp.int32, (CHUNK, CHUNK), 1)
    lower = ri >= ci
    dist = jnp.abs(ri - ci).astype(F32)
    row = lax.broadcasted_iota(jnp.int32, (CHUNK, RET_DK), 0).astype(F32)
    mid = CHUNK // 2

    def chunk_body(c, carry):
        rows = pl.ds(pl.multiple_of(c * CHUNK, CHUNK), CHUNK)

        def col(ref, base, h, width):
            return ref[rows, base + h * width: base + (h + 1) * width]

        for h in range(GLA_HEADS):
            q = col(p_ref, _C_GQ, h, GLA_DK).astype(F32) * (GLA_DK ** -0.5)
            k = col(p_ref, _C_GK, h, GLA_DK).astype(F32)
            v = col(p_ref, _C_GV, h, GLA_DV)
            gate = col(p_ref, _C_GR, h, GLA_DV).astype(F32)
            bc = col(bc_ref, 0, h, GLA_DK)
            b_last = bc[CHUNK - 1:CHUNK, :]
            b_mid = bc[mid:mid + 1, :]
            e_fwd = jnp.exp(bc - b_mid)
            e_bwd = jnp.exp(b_mid - bc)
            st = sg_ref[h]
            o = lax.dot_general((q * jnp.exp(bc)).astype(BF16), st.astype(BF16), _NT,
                                preferred_element_type=F32)
            a_lo = lax.dot_general((q * e_fwd).astype(BF16), (k * e_bwd).astype(BF16), _NT,
                                   preferred_element_type=F32)
            a_up = lax.dot_general((q * e_bwd).astype(BF16), (k * e_fwd).astype(BF16), _NT,
                                   preferred_element_type=F32)
            a = jnp.where(lower, a_lo, a_up)
            o = o + jnp.dot(a.astype(BF16), v, preferred_element_type=F32)
            u_t = lax.dot_general(v, (k * jnp.exp(b_last - bc)).astype(BF16), _TN,
                                  preferred_element_type=F32)
            sg_ref[h] = st * jnp.exp(b_last) + u_t
            y = o * lax.rsqrt(jnp.mean(o * o, axis=-1, keepdims=True) + LN_EPS)
            y = y * gg_ref[:, h * GLA_DV:(h + 1) * GLA_DV] * _silu(gate)
            o_ref[rows, h * GLA_DV:(h + 1) * GLA_DV] = y.astype(o_ref.dtype)

        cos = cos_ref[rows, :]
        sin = sin_ref[rows, :]
        half = RET_DK // 2

        def rotary(t):
            t1, t2 = t[:, :half], t[:, half:]
            return jnp.concatenate([t1 * cos - t2 * sin, t1 * sin + t2 * cos], axis=-1)

        for h in range(RET_HEADS):
            log_gamma = math.log(1.0 - 2.0 ** (-5.0 - h))
            q = rotary(col(p_ref, _C_RQ, h, RET_DK).astype(F32))
            k = rotary(col(p_ref, _C_RK, h, RET_DK).astype(F32)) * (RET_DK ** -0.5)
            v = col(p_ref, _C_RV, h, RET_DV)
            gate = col(p_ref, _C_RR, h, RET_DV).astype(F32)
            qb = q.astype(BF16)
            st = sr_ref[h]
            o = lax.dot_general(qb, st.astype(BF16), _NT, preferred_element_type=F32)
            o = o * jnp.exp(log_gamma * (row + 1.0))
            a = lax.dot_general(qb, k.astype(BF16), _NT, preferred_element_type=F32)
            a = a * jnp.exp(log_gamma * dist)
            o = o + jnp.dot(a.astype(BF16), v, preferred_element_type=F32)
            k_dec = jnp.exp(log_gamma * (CHUNK - 1.0 - row))
            u_t = lax.dot_general(v, (k * k_dec).astype(BF16), _TN, preferred_element_type=F32)
            sr_ref[h] = st * math.exp(log_gamma * CHUNK) + u_t
            oc = o - jnp.mean(o, axis=-1, keepdims=True)
            y = oc * lax.rsqrt(jnp.mean(oc * oc, axis=-1, keepdims=True) + LN_EPS)
            y = y * rg_ref[:, h * RET_DV:(h + 1) * RET_DV] * _silu(gate)
            c0 = GLA_HEADS * GLA_DV + h * RET_DV
            o_ref[rows, c0:c0 + RET_DV] = y.astype(o_ref.dtype)
        return carry

    lax.fori_loop(0, n_chunks, chunk_body, 0)


def _mixer_call(proj, bc, cos, sin, gla_g, ret_g, batch, seq, n_chunks):
    t = proj.shape[0]
    tb = n_chunks * CHUNK
    groups = seq // tb
    width = GLA_HEADS * GLA_DV + RET_HEADS * RET_DV
    tok = lambda b, g: (b * groups + g, 0)
    return pl.pallas_call(
        functools.partial(_mixer_kernel, n_chunks=n_chunks),
        grid=(batch, groups),
        in_specs=[pl.BlockSpec((tb, proj.shape[1]), tok),
                  pl.BlockSpec((tb, bc.shape[1]), tok),
                  pl.BlockSpec((tb, cos.shape[1]), lambda b, g: (g, 0)),
                  pl.BlockSpec((tb, sin.shape[1]), lambda b, g: (g, 0)),
                  pl.BlockSpec(gla_g.shape, lambda b, g: (0, 0)),
                  pl.BlockSpec(ret_g.shape, lambda b, g: (0, 0))],
        out_specs=pl.BlockSpec((tb, width), tok),
        out_shape=jax.ShapeDtypeStruct((t, width), BF16),
        scratch_shapes=[pltpu.VMEM((GLA_HEADS, GLA_DV, GLA_DK), F32),
                        pltpu.VMEM((RET_HEADS, RET_DV, RET_DK), F32)],
        compiler_params=_params("parallel", "arbitrary"),
        name="mixer",
    )(proj, bc, cos, sin, gla_g, ret_g)


def _layer_norm(y, g, b):
    mu = jnp.mean(y, axis=-1, keepdims=True)
    yc = y - mu
    var = jnp.mean(yc * yc, axis=-1, keepdims=True)
    return yc * lax.rsqrt(var + LN_EPS) * g + b


def _out_ln_kernel(m_ref, w_ref, x_ref, g_ref, b_ref, h_ref, hb_ref):
    mix = jnp.dot(m_ref[...], w_ref[...], preferred_element_type=F32)
    h = _layer_norm(DEEPNORM_ALPHA * x_ref[...] + mix, g_ref[...], b_ref[...])
    h_ref[...] = h
    hb_ref[...] = h.astype(BF16)


def _out_ln_call(mix, w_out, x, g, b, tm):
    t, d = x.shape
    return pl.pallas_call(
        _out_ln_kernel,
        grid=(t // tm,),
        in_specs=[pl.BlockSpec((tm, mix.shape[1]), lambda i: (i, 0)),
                  pl.BlockSpec(w_out.shape, lambda i: (0, 0)),
                  pl.BlockSpec((tm, d), lambda i: (i, 0)),
                  pl.BlockSpec((1, d), lambda i: (0, 0)),
                  pl.BlockSpec((1, d), lambda i: (0, 0))],
        out_specs=[pl.BlockSpec((tm, d), lambda i: (i, 0)),
                   pl.BlockSpec((tm, d), lambda i: (i, 0))],
        out_shape=[jax.ShapeDtypeStruct((t, d), F32), jax.ShapeDtypeStruct((t, d), BF16)],
        compiler_params=_params("parallel"),
        name="out_proj_ln",
    )(mix, w_out, x, g, b)


def _topk_rows(s, k, payload=None):
    n = s.shape[0]
    iota = lax.broadcasted_iota(jnp.int32, s.shape, 0).astype(F32)
    vals, picks = [], []
    for _ in range(k):
        m = jnp.max(s, axis=0, keepdims=True)
        idx = jnp.min(jnp.where(s == m, iota, float(n)), axis=0, keepdims=True)
        hit = iota == idx
        vals.append(m)
        if payload is None:
            picks.append(idx)
        else:
            picks.append(jnp.sum(jnp.where(hit, payload, 0.0), axis=0, keepdims=True))
        s = jnp.where(hit, -jnp.inf, s)
    return jnp.concatenate(vals, axis=0), jnp.concatenate(picks, axis=0)


def _topk_paired(s, k):
    n = s.shape[0]
    h = n // 2
    lo, hi = s[:h, :], s[h:, :]
    row = lax.broadcasted_iota(jnp.int32, lo.shape, 0).astype(F32)
    swap = hi > lo
    front = jnp.where(swap, hi, lo)
    back = jnp.where(swap, lo, hi)
    front_row = jnp.where(swap, row + float(h), row)
    back_row = jnp.where(swap, row, row + float(h))
    vals, rows = [], []
    for _ in range(k):
        m = jnp.max(front, axis=0, keepdims=True)
        idx = jnp.min(jnp.where(front == m, front_row, float(n)), axis=0, keepdims=True)
        hit = front_row == idx
        vals.append(m)
        rows.append(idx)
        front = jnp.where(hit, back, front)
        front_row = jnp.where(hit, back_row, front_row)
        back = jnp.where(hit, -jnp.inf, back)
    return jnp.concatenate(vals, axis=0), jnp.concatenate(rows, axis=0)


def _route_kernel(h_ref, wq_ref, sk_ref, u_ref, v_ref, a_ref, b_ref, bt_ref, g_ref, ub_ref, vb_ref):
    ub_ref[...] = u_ref[...].T.astype(BF16)
    vb_ref[...] = v_ref[...].astype(BF16)
    kk = PEER_TOPK
    q = jnp.dot(h_ref[...], wq_ref[...], preferred_element_type=F32).astype(BF16)
    n_tok = q.shape[0]
    experts, gates = [], []
    for h in range(PEER_HEADS):
        vals, idxs = [], []
        for p in range(2):
            j = 2 * h + p
            s_t = lax.dot_general(sk_ref[j], q[:, j * LANES:(j + 1) * LANES], _NT,
                                  preferred_element_type=F32)
            v_p, i_p = _topk_paired(s_t, kk)
            vals.append(v_p)
            idxs.append(i_p)
        cand, expert = [], []
        for r1 in range(kk):
            n2 = kk // (r1 + 1)
            cand.append(vals[0][r1:r1 + 1, :] + vals[1][:n2, :])
            expert.append(idxs[0][r1:r1 + 1, :] * float(PEER_NKEYS) + idxs[1][:n2, :])
        n_cand = sum(kk // (r1 + 1) for r1 in range(kk))
        pad = (-n_cand) % SUBLANES
        cand.append(jnp.full((pad, n_tok), -jnp.inf, F32))
        expert.append(jnp.zeros((pad, n_tok), F32))
        top_s, top_e = _topk_rows(jnp.concatenate(cand, axis=0), kk, jnp.concatenate(expert, axis=0))
        e = jnp.exp(top_s - top_s[0:1, :])
        experts.append(top_e)
        gates.append(e / jnp.sum(e, axis=0, keepdims=True))
    top_e = jnp.concatenate(experts, axis=0).astype(jnp.int32)
    a_ref[...] = (top_e >> (PEER_NKEYS.bit_length() - 1)).T
    second = top_e & (PEER_NKEYS - 1)
    b_ref[...] = second.T
    bt_ref[...] = second
    g_ref[...] = jnp.concatenate(gates, axis=0).T


def _route_call(hb, w_q, subkeys, u_tab, v_tab, tt):
    t, d = hb.shape
    n_sel = PEER_HEADS * PEER_TOPK
    n_steps = t // tt
    n_exp, d_exp = u_tab.shape
    slab = n_exp // n_steps
    assert slab * n_steps == n_exp and slab % (2 * SUBLANES) == 0
    tok_major = pl.BlockSpec((tt, n_sel), lambda i: (i, 0))
    table = pl.BlockSpec((slab, d_exp), lambda i: (i, 0))
    return pl.pallas_call(
        _route_kernel,
        grid=(n_steps,),
        in_specs=[pl.BlockSpec((tt, d), lambda i: (i, 0)),
                  pl.BlockSpec(w_q.shape, lambda i: (0, 0)),
                  pl.BlockSpec(subkeys.shape, lambda i: (0, 0, 0)),
                  table, table],
        out_specs=[tok_major, tok_major, pl.BlockSpec((n_sel, tt), lambda i: (0, i)), tok_major,
                   pl.BlockSpec((d_exp, slab), lambda i: (0, i)), table],
        out_shape=[jax.ShapeDtypeStruct((t, n_sel), jnp.int32), jax.ShapeDtypeStruct((t, n_sel), jnp.int32),
                   jax.ShapeDtypeStruct((n_sel, t), jnp.int32), jax.ShapeDtypeStruct((t, n_sel), F32),
                   jax.ShapeDtypeStruct((d_exp, n_exp), BF16), jax.ShapeDtypeStruct(v_tab.shape, BF16)],
        compiler_params=_params("parallel"),
        name="peer_route",
    )(hb, w_q, subkeys, u_tab, v_tab)


_W_ROWS = PEER_NKEYS // 2


def _wbuild_kernel(a_ref, b_ref, bt_ref, g_ref, w_ref):
    n_tok, n_sel = a_ref.shape
    key_rows = lax.broadcasted_iota(jnp.int32, (PEER_NKEYS, n_sel), 0)
    key_cols = lax.broadcasted_iota(jnp.int32, (n_sel, PEER_NKEYS), 1)
    for t in range(n_tok):
        pa = jnp.where(key_rows == a_ref[t:t + 1, :], g_ref[t:t + 1, :], 0.0).astype(BF16)
        if t % 2:
            pb = jnp.where(key_cols == bt_ref[:, t:t + 1], 1.0, 0.0).astype(BF16)
            w = jnp.dot(pa, pb, preferred_element_type=F32)
        else:
            pb = jnp.where(key_rows == b_ref[t:t + 1, :], 1.0, 0.0).astype(BF16)
            w = lax.dot_general(pa, pb, _NT, preferred_element_type=F32)
        words = pltpu.pack_elementwise([w[:_W_ROWS, :], w[_W_ROWS:, :]], packed_dtype=BF16)
        w_ref[t // SUBLANES, pl.ds(t % SUBLANES, _W_ROWS, stride=SUBLANES), :] = (
            lax.bitcast_convert_type(words, jnp.uint32))


def _wbuild_call(a_tm, b_tm, b_sm, g_tm, tt):
    t, n_sel = a_tm.shape
    rows = _W_ROWS * SUBLANES
    tok_major = pl.BlockSpec((tt, n_sel), lambda i: (i, 0))
    return pl.pallas_call(
        _wbuild_kernel,
        grid=(t // tt,),
        in_specs=[tok_major, tok_major, pl.BlockSpec((n_sel, tt), lambda i: (0, i)), tok_major],
        out_specs=pl.BlockSpec((tt // SUBLANES, rows, PEER_NKEYS), lambda i: (i, 0, 0)),
        out_shape=jax.ShapeDtypeStruct((t // SUBLANES, rows, PEER_NKEYS), jnp.uint32),
        compiler_params=_params("parallel"),
        name="peer_gate_map",
    )(a_tm, b_tm, b_sm, g_tm)


def _gelu(x):
    return 0.5 * x * (1.0 + lax.erf(x * math.sqrt(0.5)))


def _peer_kernel(hb_ref, ul_ref, uh_ref, vl_ref, vh_ref, w_ref, h_ref, g_ref, b_ref, o_ref, acc_ref, gs_ref):
    j = pl.program_id(1)

    @pl.when(j == 0)
    def _():
        acc_ref[...] = jnp.zeros_like(acc_ref)

    tm = hb_ref.shape[0]
    half = vl_ref.shape[0]
    hb = hb_ref[...]
    for part, u_ref in enumerate((ul_ref, uh_ref)):
        act = _gelu(jnp.dot(hb, u_ref[...], preferred_element_type=F32))
        for a in range(half // PEER_NKEYS):
            word = w_ref[:, a, :, :].reshape(tm, PEER_NKEYS)
            w_a = pltpu.unpack_elementwise(word, index=part, packed_dtype=BF16, unpacked_dtype=F32)
            cols = slice(a * PEER_NKEYS, (a + 1) * PEER_NKEYS)
            gs_ref[part, :, cols] = (w_a * act[:, cols]).astype(BF16)
    acc_ref[...] += (jnp.dot(gs_ref[0], vl_ref[...], preferred_element_type=F32)
                     + jnp.dot(gs_ref[1], vh_ref[...], preferred_element_type=F32))

    @pl.when(j == pl.num_programs(1) - 1)
    def _():
        o_ref[...] = _layer_norm(DEEPNORM_ALPHA * h_ref[...] + acc_ref[...], g_ref[...], b_ref[...])


def _peer_call(hb, u_t, v_tab, w4, h, g, b, tm, tn):
    t, d = h.shape
    n_exp = v_tab.shape[0]
    half = tn // 2
    n_steps = n_exp // tn
    lo = pl.BlockSpec((half, d), lambda i, j: (j, 0))
    hi = pl.BlockSpec((half, d), lambda i, j: (j + n_steps, 0))
    tok = pl.BlockSpec((tm, d), lambda i, j: (i, 0))
    vec = pl.BlockSpec((1, d), lambda i, j: (0, 0))
    return pl.pallas_call(
        _peer_kernel,
        grid=(t // tm, n_steps),
        in_specs=[tok, pl.BlockSpec((d, half), lambda i, j: (0, j)),
                  pl.BlockSpec((d, half), lambda i, j: (0, j + n_steps)), lo, hi,
                  pl.BlockSpec((tm // SUBLANES, half // PEER_NKEYS, SUBLANES, PEER_NKEYS),
                               lambda i, j: (i, j, 0, 0)),
                  tok, vec, vec],
        out_specs=tok,
        out_shape=jax.ShapeDtypeStruct((t, d), F32),
        scratch_shapes=[pltpu.VMEM((tm, d), F32), pltpu.VMEM((2, tm, half), BF16)],
        compiler_params=_params("parallel", "arbitrary"),
        name="peer_dense",
    )(hb, u_t, u_t, v_tab, v_tab, w4, h, g, b)


def _layer(x2, batch, seq, w_in, w_gate_up, b_gate, gla_g, ret_g, w_out, ln1_g, ln1_b,
           w_q, subkeys, u_tab, v_tab, ln2_g, ln2_b):
    t, d = x2.shape
    row = lambda p: p.reshape(1, -1).astype(F32)

    w_up = jnp.pad(w_gate_up, ((0, LANES - GLA_GATE_RANK), (0, 0))).astype(BF16)
    bc, xb, w_all = _gate_call(x2, w_in, w_up, row(b_gate), min(256, t))
    w_ret = w_all[:, _C_RQ + GLA_GATE_RANK:]
    proj = _proj_call(xb, w_all, w_ret, min(1024, t), 1024)

    half = RET_DK // 2
    inv = ROPE_BASE ** (-jnp.arange(half, dtype=F32) / half)
    ang = jnp.arange(seq).astype(F32)[:, None] * inv[None, :]
    n_chunks = min(8, seq // CHUNK)
    mix = _mixer_call(proj, bc, jnp.cos(ang), jnp.sin(ang), row(gla_g), row(ret_g), batch, seq, n_chunks)

    h, hb = _out_ln_call(mix, w_out.astype(BF16), x2, row(ln1_g), row(ln1_b), min(512, t))

    sk = subkeys.reshape(PEER_HEADS * 2, PEER_NKEYS, -1).astype(BF16)
    a_sel, b_sel, bt_sel, gates, u_b, v_b = _route_call(hb, w_q.astype(BF16), sk, u_tab, v_tab, min(256, t))
    w3 = _wbuild_call(a_sel, b_sel, bt_sel, gates, min(128, t))
    w4 = w3.reshape(t // SUBLANES, _W_ROWS, SUBLANES, PEER_NKEYS)

    return _peer_call(hb, u_b, v_b, w4, h, row(ln2_g), row(ln2_b),
                      min(512, t), 1024)


def kernel(x, w_in, w_gla_gate_up, b_gla_gate, gla_norm_g, ret_norm_g, w_out, ln1_g, ln1_b,
           w_peer_q, peer_subkeys, peer_u, peer_v, ln2_g, ln2_b):
    batch, seq, d = x.shape
    h = x.reshape(batch * seq, d)
    for l in range(DEPTH):
        h = _layer(h, batch, seq, w_in[l], w_gla_gate_up[l], b_gla_gate[l], gla_norm_g[l], ret_norm_g[l],
                   w_out[l], ln1_g[l], ln1_b[l], w_peer_q[l], peer_subkeys[l], peer_u[l], peer_v[l],
                   ln2_g[l], ln2_b[l])
    return h.reshape(batch, seq, d)
```

```python
import functools
import math

import jax
import jax.numpy as jnp
from jax import lax
from jax.experimental import pallas as pl
from jax.experimental.pallas import tpu as pltpu

F32 = jnp.float32
BF16 = jnp.bfloat16

DEPTH = 1
CHUNK = 64
GLA_HEADS = 4
GLA_DK = 128
GLA_DV = 256
GLA_GATE_RANK = 16
GLA_TAU = 16.0
RET_HEADS = 4
RET_DK = 256
RET_DV = 256
ROPE_BASE = 10000.0
PEER_HEADS = 8
PEER_NKEYS = 128
PEER_TOPK = 16
LN_EPS = 1e-5
DEEPNORM_ALPHA = (2 * DEPTH) ** 0.25

LANES = 128
SUBLANES = 8
VMEM_LIMIT = 56 * 1024 * 1024

_C_GQ = 0
_C_GK = _C_GQ + GLA_HEADS * GLA_DK
_C_GV = _C_GK + GLA_HEADS * GLA_DK
_C_GR = _C_GV + GLA_HEADS * GLA_DV
_C_RQ = _C_GR + GLA_HEADS * GLA_DV
_C_RK = _C_RQ + RET_HEADS * RET_DK
_C_RV = _C_RK + RET_HEADS * RET_DK
_C_RR = _C_RV + RET_HEADS * RET_DV
_C_END = _C_RR + RET_HEADS * RET_DV

_NT = (((1,), (1,)), ((), ()))
_TN = (((0,), (0,)), ((), ()))


def _params(*sem):
    return pltpu.CompilerParams(dimension_semantics=sem, vmem_limit_bytes=VMEM_LIMIT)


def _gate_kernel(x_ref, wl_ref, ws_ref, wu_ref, b_ref, o_ref, xb_ref, wb_ref, *, steps_per_slab):
    tm = x_ref.shape[0]
    xb = x_ref[...].astype(BF16)
    xb_ref[...] = xb

    @pl.when(pl.program_id(0) % steps_per_slab == 0)
    def _():
        wb_ref[...] = ws_ref[...].astype(BF16)

    glr = lax.dot_general(xb, wl_ref[...].astype(BF16), _NT, preferred_element_type=F32)
    lane = lax.broadcasted_iota(jnp.int32, glr.shape, 1)
    glr = jnp.where(lane < GLA_GATE_RANK, glr, 0.0)
    z = jnp.dot(glr.astype(BF16), wu_ref[...], preferred_element_type=F32) + b_ref[...]
    log_a = (jnp.minimum(z, 0.0) - jnp.log1p(jnp.exp(-jnp.abs(z)))) / GLA_TAU
    r = lax.broadcasted_iota(jnp.int32, (tm, tm), 0)
    c = lax.broadcasted_iota(jnp.int32, (tm, tm), 1)
    tri = jnp.where((r >= c) & ((r // CHUNK) == (c // CHUNK)), 1.0, 0.0).astype(F32)
    o_ref[...] = jnp.dot(tri, log_a, preferred_element_type=F32, precision=lax.Precision.HIGHEST)


def _gate_call(x, w_t, w_up, b_gate, tm):
    t, d = x.shape
    n = w_up.shape[1]
    n_steps = t // tm
    n_cols = w_t.shape[0]
    n_slabs = max(n_steps // 2, 1)
    slab = -(-n_cols // (n_slabs * 2 * SUBLANES)) * 2 * SUBLANES
    steps_per_slab = n_steps // n_slabs
    assert (n_slabs - 1) * slab < n_cols <= n_slabs * slab
    w_slab = pl.BlockSpec((slab, d), lambda i: (i // steps_per_slab, 0))
    return pl.pallas_call(
        functools.partial(_gate_kernel, steps_per_slab=steps_per_slab),
        grid=(n_steps,),
        in_specs=[pl.BlockSpec((tm, d), lambda i: (i, 0)),
                  pl.BlockSpec((LANES, d), lambda i: (_C_RQ // LANES, 0)),
                  w_slab,
                  pl.BlockSpec(w_up.shape, lambda i: (0, 0)),
                  pl.BlockSpec((1, n), lambda i: (0, 0))],
        out_specs=[pl.BlockSpec((tm, n), lambda i: (i, 0)),
                   pl.BlockSpec((tm, d), lambda i: (i, 0)),
                   w_slab],
        out_shape=[jax.ShapeDtypeStruct((t, n), F32), jax.ShapeDtypeStruct((t, d), BF16),
                   jax.ShapeDtypeStruct(w_t.shape, BF16)],
        compiler_params=_params("arbitrary"),
        name="gla_gate",
    )(x, w_t, w_t, w_up, b_gate)


def _proj_kernel(x_ref, w_ref, o_ref):
    o_ref[...] = lax.dot_general(x_ref[...], w_ref[...], _NT, preferred_element_type=F32).astype(o_ref.dtype)


def _proj_call(xb, w_tb, tm, tn):
    m, k = xb.shape
    n_gla_tiles = _C_RQ // tn
    assert n_gla_tiles * tn == _C_RQ
    w_row = lambda i, j: (pl.multiple_of(j * tn + jnp.where(j >= n_gla_tiles, GLA_GATE_RANK, 0), GLA_GATE_RANK), 0)
    return pl.pallas_call(
        _proj_kernel,
        grid=(m // tm, _C_END // tn),
        in_specs=[pl.BlockSpec((tm, k), lambda i, j: (i, 0)),
                  pl.BlockSpec((pl.Element(tn), pl.Element(k)), w_row)],
        out_specs=pl.BlockSpec((tm, tn), lambda i, j: (i, j)),
        out_shape=jax.ShapeDtypeStruct((m, _C_END), BF16),
        compiler_params=_params("parallel", "parallel"),
        name="proj_matmul",
    )(xb, w_tb)


def _silu(x):
    return x * jax.nn.sigmoid(x)


def _mixer_kernel(p_ref, bc_ref, cos_ref, sin_ref, gg_ref, rg_ref, o_ref, sg_ref, sr_ref, *, n_chunks):
    @pl.when(pl.program_id(1) == 0)
    def _():
        sg_ref[...] = jnp.zeros_like(sg_ref)
        sr_ref[...] = jnp.zeros_like(sr_ref)

    ri = lax.broadcasted_iota(jnp.int32, (CHUNK, CHUNK), 0)
    ci = lax.broadcasted_iota(jnp.int32, (CHUNK, CHUNK), 1)
    lower = ri >= ci
    dist = jnp.abs(ri - ci).astype(F32)
    row = lax.broadcasted_iota(jnp.int32, (CHUNK, RET_DK), 0).astype(F32)
    mid = CHUNK // 2

    def chunk_body(c, carry):
        rows = pl.ds(pl.multiple_of(c * CHUNK, CHUNK), CHUNK)

        def col(ref, base, h, width):
            return ref[rows, base + h * width: base + (h + 1) * width]

        for h in range(GLA_HEADS):
            q = col(p_ref, _C_GQ, h, GLA_DK).astype(F32) * (GLA_DK ** -0.5)
            k = col(p_ref, _C_GK, h, GLA_DK).astype(F32)
            v = col(p_ref, _C_GV, h, GLA_DV)
            gate = col(p_ref, _C_GR, h, GLA_DV).astype(F32)
            bc = col(bc_ref, 0, h, GLA_DK)
            b_last = bc[CHUNK - 1:CHUNK, :]
            b_mid = bc[mid:mid + 1, :]
            e_fwd = jnp.exp(bc - b_mid)
            e_bwd = jnp.exp(b_mid - bc)
            st = sg_ref[h]
            o = lax.dot_general((q * jnp.exp(bc)).astype(BF16), st.astype(BF16), _NT,
                                preferred_element_type=F32)
            a_lo = lax.dot_general((q * e_fwd).astype(BF16), (k * e_bwd).astype(BF16), _NT,
                                   preferred_element_type=F32)
            a_up = lax.dot_general((q * e_bwd).astype(BF16), (k * e_fwd).astype(BF16), _NT,
                                   preferred_element_type=F32)
            a = jnp.where(lower, a_lo, a_up)
            o = o + jnp.dot(a.astype(BF16), v, preferred_element_type=F32)
            u_t = lax.dot_general(v, (k * jnp.exp(b_last - bc)).astype(BF16), _TN,
                                  preferred_element_type=F32)
            sg_ref[h] = st * jnp.exp(b_last) + u_t
            y = o * lax.rsqrt(jnp.mean(o * o, axis=-1, keepdims=True) + LN_EPS)
            y = y * gg_ref[:, h * GLA_DV:(h + 1) * GLA_DV] * _silu(gate)
            o_ref[rows, h * GLA_DV:(h + 1) * GLA_DV] = y.astype(o_ref.dtype)

        cos = cos_ref[rows, :]
        sin = sin_ref[rows, :]
        half = RET_DK // 2

        def rotary(t):
            t1, t2 = t[:, :half], t[:, half:]
            return jnp.concatenate([t1 * cos - t2 * sin, t1 * sin + t2 * cos], axis=-1)

        for h in range(RET_HEADS):
            log_gamma = math.log(1.0 - 2.0 ** (-5.0 - h))
            q = rotary(col(p_ref, _C_RQ, h, RET_DK).astype(F32))
            k = rotary(col(p_ref, _C_RK, h, RET_DK).astype(F32)) * (RET_DK ** -0.5)
            v = col(p_ref, _C_RV, h, RET_DV)
            gate = col(p_ref, _C_RR, h, RET_DV).astype(F32)
            qb = q.astype(BF16)
            st = sr_ref[h]
            o = lax.dot_general(qb, st.astype(BF16), _NT, preferred_element_type=F32)
            o = o * jnp.exp(log_gamma * (row + 1.0))
            a = lax.dot_general(qb, k.astype(BF16), _NT, preferred_element_type=F32)
            a = a * jnp.exp(log_gamma * dist)
            o = o + jnp.dot(a.astype(BF16), v, preferred_element_type=F32)
            k_dec = jnp.exp(log_gamma * (CHUNK - 1.0 - row))
            u_t = lax.dot_general(v, (k * k_dec).astype(BF16), _TN, preferred_element_type=F32)
            sr_ref[h] = st * math.exp(log_gamma * CHUNK) + u_t
            oc = o - jnp.mean(o, axis=-1, keepdims=True)
            y = oc * lax.rsqrt(jnp.mean(oc * oc, axis=-1, keepdims=True) + LN_EPS)
            y = y * rg_ref[:, h * RET_DV:(h + 1) * RET_DV] * _silu(gate)
            c0 = GLA_HEADS * GLA_DV + h * RET_DV
            o_ref[rows, c0:c0 + RET_DV] = y.astype(o_ref.dtype)
        return carry

    lax.fori_loop(0, n_chunks, chunk_body, 0)


def _mixer_call(proj, bc, cos, sin, gla_g, ret_g, batch, seq, n_chunks):
    t = proj.shape[0]
    tb = n_chunks * CHUNK
    groups = seq // tb
    width = GLA_HEADS * GLA_DV + RET_HEADS * RET_DV
    tok = lambda b, g: (b * groups + g, 0)
    return pl.pallas_call(
        functools.partial(_mixer_kernel, n_chunks=n_chunks),
        grid=(batch, groups),
        in_specs=[pl.BlockSpec((tb, proj.shape[1]), tok),
                  pl.BlockSpec((tb, bc.shape[1]), tok),
                  pl.BlockSpec((tb, cos.shape[1]), lambda b, g: (g, 0)),
                  pl.BlockSpec((tb, sin.shape[1]), lambda b, g: (g, 0)),
                  pl.BlockSpec(gla_g.shape, lambda b, g: (0, 0)),
                  pl.BlockSpec(ret_g.shape, lambda b, g: (0, 0))],
        out_specs=pl.BlockSpec((tb, width), tok),
        out_shape=jax.ShapeDtypeStruct((t, width), BF16),
        scratch_shapes=[pltpu.VMEM((GLA_HEADS, GLA_DV, GLA_DK), F32),
                        pltpu.VMEM((RET_HEADS, RET_DV, RET_DK), F32)],
        compiler_params=_params("parallel", "arbitrary"),
        name="mixer",
    )(proj, bc, cos, sin, gla_g, ret_g)


def _layer_norm(y, g, b):
    mu = jnp.mean(y, axis=-1, keepdims=True)
    yc = y - mu
    var = jnp.mean(yc * yc, axis=-1, keepdims=True)
    return yc * lax.rsqrt(var + LN_EPS) * g + b


def _out_ln_kernel(m_ref, w_ref, x_ref, g_ref, b_ref, h_ref, hb_ref):
    half = m_ref.shape[0] // 2
    for r in range(2):
        rows = slice(r * half, (r + 1) * half)
        mix = jnp.dot(m_ref[rows, :], w_ref[...], preferred_element_type=F32)
        h = _layer_norm(DEEPNORM_ALPHA * x_ref[rows, :] + mix, g_ref[...], b_ref[...])
        h_ref[rows, :] = h
        hb_ref[rows, :] = h.astype(BF16)


def _out_ln_call(mix, w_out, x, g, b, tm):
    t, d = x.shape
    return pl.pallas_call(
        _out_ln_kernel,
        grid=(t // tm,),
        in_specs=[pl.BlockSpec((tm, mix.shape[1]), lambda i: (i, 0)),
                  pl.BlockSpec(w_out.shape, lambda i: (0, 0)),
                  pl.BlockSpec((tm, d), lambda i: (i, 0)),
                  pl.BlockSpec((1, d), lambda i: (0, 0)),
                  pl.BlockSpec((1, d), lambda i: (0, 0))],
        out_specs=[pl.BlockSpec((tm, d), lambda i: (i, 0)),
                   pl.BlockSpec((tm, d), lambda i: (i, 0))],
        out_shape=[jax.ShapeDtypeStruct((t, d), F32), jax.ShapeDtypeStruct((t, d), BF16)],
        compiler_params=_params("parallel"),
        name="out_proj_ln",
    )(mix, w_out, x, g, b)


def _topk_paired(s, key, k):
    h = s.shape[0] // 2
    if key is None:
        key_lo = lax.broadcasted_iota(jnp.int32, (h, s.shape[1]), 0).astype(F32)
        key_hi = key_lo + float(h)
    else:
        key_lo, key_hi = key[:h, :], key[h:, :]
    swap = s[h:, :] > s[:h, :]
    front = jnp.where(swap, s[h:, :], s[:h, :])
    back = jnp.where(swap, s[:h, :], s[h:, :])
    front_key = jnp.where(swap, key_hi, key_lo)
    back_key = jnp.where(swap, key_lo, key_hi)
    vals, keys = [], []
    for _ in range(k):
        m = jnp.max(front, axis=0, keepdims=True)
        pick = jnp.min(jnp.where(front == m, front_key, jnp.inf), axis=0, keepdims=True)
        hit = front_key == pick
        vals.append(m)
        keys.append(pick)
        front = jnp.where(hit, back, front)
        front_key = jnp.where(hit, back_key, front_key)
        back = jnp.where(hit, -jnp.inf, back)
    return jnp.concatenate(vals, axis=0), jnp.concatenate(keys, axis=0)


def _route_kernel(h_ref, wq_ref, sk_ref, u_ref, v_ref, a_ref, b_ref, bt_ref, g_ref, ub_ref, vb_ref):
    ub_ref[...] = u_ref[...].T.astype(BF16)
    vb_ref[...] = v_ref[...].astype(BF16)
    kk = PEER_TOPK
    q = jnp.dot(h_ref[...], wq_ref[...], preferred_element_type=F32).astype(BF16)
    n_tok = q.shape[0]
    n_exp = PEER_NKEYS * PEER_NKEYS
    n_cand = sum(kk // (r1 + 1) for r1 in range(kk))
    pad = (-n_cand) % (2 * SUBLANES)
    cand_row = lax.broadcasted_iota(jnp.int32, (n_cand + pad, n_tok), 0).astype(F32) * float(n_exp)
    experts, gates = [], []
    for h in range(PEER_HEADS):
        vals, idxs = [], []
        for p in range(2):
            j = 2 * h + p
            s_t = lax.dot_general(sk_ref[j], q[:, j * LANES:(j + 1) * LANES], _NT,
                                  preferred_element_type=F32)
            v_p, i_p = _topk_paired(s_t, None, kk)
            vals.append(v_p)
            idxs.append(i_p)
        cand, expert = [], []
        for r1 in range(kk):
            n2 = kk // (r1 + 1)
            cand.append(vals[0][r1:r1 + 1, :] + vals[1][:n2, :])
            expert.append(idxs[0][r1:r1 + 1, :] * float(PEER_NKEYS) + idxs[1][:n2, :])
        cand.append(jnp.full((pad, n_tok), -jnp.inf, F32))
        expert.append(jnp.zeros((pad, n_tok), F32))
        top_s, top_key = _topk_paired(jnp.concatenate(cand, axis=0),
                                      cand_row + jnp.concatenate(expert, axis=0), kk)
        e = jnp.exp(top_s - top_s[0:1, :])
        experts.append(top_key)
        gates.append(e / jnp.sum(e, axis=0, keepdims=True))
    top_e = jnp.concatenate(experts, axis=0).astype(jnp.int32) & (n_exp - 1)
    a_ref[...] = (top_e >> (PEER_NKEYS.bit_length() - 1)).T
    second = top_e & (PEER_NKEYS - 1)
    b_ref[...] = second.T
    bt_ref[...] = second
    g_ref[...] = jnp.concatenate(gates, axis=0).T


def _route_call(hb, w_q, subkeys, u_tab, v_tab, tt):
    t, d = hb.shape
    n_sel = PEER_HEADS * PEER_TOPK
    n_steps = t // tt
    n_exp, d_exp = u_tab.shape
    slab = n_exp // n_steps
    assert slab * n_steps == n_exp and slab % (2 * SUBLANES) == 0
    tok_major = pl.BlockSpec((tt, n_sel), lambda i: (i, 0))
    table = pl.BlockSpec((slab, d_exp), lambda i: (i, 0))
    return pl.pallas_call(
        _route_kernel,
        grid=(n_steps,),
        in_specs=[pl.BlockSpec((tt, d), lambda i: (i, 0)),
                  pl.BlockSpec(w_q.shape, lambda i: (0, 0)),
                  pl.BlockSpec(subkeys.shape, lambda i: (0, 0, 0)),
                  table, table],
        out_specs=[tok_major, tok_major, pl.BlockSpec((n_sel, tt), lambda i: (0, i)), tok_major,
                   pl.BlockSpec((d_exp, slab), lambda i: (0, i)), table],
        out_shape=[jax.ShapeDtypeStruct((t, n_sel), jnp.int32), jax.ShapeDtypeStruct((t, n_sel), jnp.int32),
                   jax.ShapeDtypeStruct((n_sel, t), jnp.int32), jax.ShapeDtypeStruct((t, n_sel), F32),
                   jax.ShapeDtypeStruct((d_exp, n_exp), BF16), jax.ShapeDtypeStruct(v_tab.shape, BF16)],
        compiler_params=_params("parallel"),
        name="peer_route",
    )(hb, w_q, subkeys, u_tab, v_tab)


_W_ROWS = PEER_NKEYS // 2


def _wbuild_kernel(a_ref, b_ref, bt_ref, g_ref, w_ref):
    n_tok, n_sel = a_ref.shape
    key_rows = lax.broadcasted_iota(jnp.int32, (PEER_NKEYS, n_sel), 0)
    key_cols = lax.broadcasted_iota(jnp.int32, (n_sel, PEER_NKEYS), 1)
    for t in range(n_tok):
        pa = jnp.where(key_rows == a_ref[t:t + 1, :], g_ref[t:t + 1, :], 0.0).astype(BF16)
        if t % 2:
            pb = jnp.where(key_cols == bt_ref[:, t:t + 1], 1.0, 0.0).astype(BF16)
            w = jnp.dot(pa, pb, preferred_element_type=F32)
        else:
            pb = jnp.where(key_rows == b_ref[t:t + 1, :], 1.0, 0.0).astype(BF16)
            w = lax.dot_general(pa, pb, _NT, preferred_element_type=F32)
        words = pltpu.pack_elementwise([w[:_W_ROWS, :], w[_W_ROWS:, :]], packed_dtype=BF16)
        w_ref[t // SUBLANES, pl.ds(t % SUBLANES, _W_ROWS, stride=SUBLANES), :] = (
            lax.bitcast_convert_type(words, jnp.uint32))


def _wbuild_call(a_tm, b_tm, b_sm, g_tm, tt):
    t, n_sel = a_tm.shape
    rows = _W_ROWS * SUBLANES
    tok_major = pl.BlockSpec((tt, n_sel), lambda i: (i, 0))
    return pl.pallas_call(
        _wbuild_kernel,
        grid=(t // tt,),
        in_specs=[tok_major, tok_major, pl.BlockSpec((n_sel, tt), lambda i: (0, i)), tok_major],
        out_specs=pl.BlockSpec((tt // SUBLANES, rows, PEER_NKEYS), lambda i: (i, 0, 0)),
        out_shape=jax.ShapeDtypeStruct((t // SUBLANES, rows, PEER_NKEYS), jnp.uint32),
        compiler_params=_params("parallel"),
        name="peer_gate_map",
    )(a_tm, b_tm, b_sm, g_tm)


def _gelu(x):
    return 0.5 * x * (1.0 + lax.erf(x * math.sqrt(0.5)))


def _peer_kernel(hb_ref, ul_ref, uh_ref, vl_ref, vh_ref, w_ref, h_ref, g_ref, b_ref, o_ref, acc_ref, gs_ref):
    j = pl.program_id(1)

    @pl.when(j == 0)
    def _():
        acc_ref[...] = jnp.zeros_like(acc_ref)

    tm = hb_ref.shape[0]
    half = vl_ref.shape[0]
    hb = hb_ref[...]
    for part, u_ref in enumerate((ul_ref, uh_ref)):
        act = _gelu(jnp.dot(hb, u_ref[...], preferred_element_type=F32))
        for a in range(half // PEER_NKEYS):
            word = w_ref[:, a, :, :].reshape(tm, PEER_NKEYS)
            w_a = pltpu.unpack_elementwise(word, index=part, packed_dtype=BF16, unpacked_dtype=F32)
            cols = slice(a * PEER_NKEYS, (a + 1) * PEER_NKEYS)
            gs_ref[part, :, cols] = (w_a * act[:, cols]).astype(BF16)
    acc_ref[...] += (jnp.dot(gs_ref[0], vl_ref[...], preferred_element_type=F32)
                     + jnp.dot(gs_ref[1], vh_ref[...], preferred_element_type=F32))

    @pl.when(j == pl.num_programs(1) - 1)
    def _():
        o_ref[...] = _layer_norm(DEEPNORM_ALPHA * h_ref[...] + acc_ref[...], g_ref[...], b_ref[...])


def _peer_call(hb, u_t, v_tab, w4, h, g, b, tm, tn):
    t, d = h.shape
    n_exp = v_tab.shape[0]
    half = tn // 2
    n_steps = n_exp // tn
    lo = pl.BlockSpec((half, d), lambda i, j: (j, 0))
    hi = pl.BlockSpec((half, d), lambda i, j: (j + n_steps, 0))
    tok = pl.BlockSpec((tm, d), lambda i, j: (i, 0))
    vec = pl.BlockSpec((1, d), lambda i, j: (0, 0))
    return pl.pallas_call(
        _peer_kernel,
        grid=(t // tm, n_steps),
        in_specs=[tok, pl.BlockSpec((d, half), lambda i, j: (0, j)),
                  pl.BlockSpec((d, half), lambda i, j: (0, j + n_steps)), lo, hi,
                  pl.BlockSpec((tm // SUBLANES, half // PEER_NKEYS, SUBLANES, PEER_NKEYS),
                               lambda i, j: (i, j, 0, 0)),
                  tok, vec, vec],
        out_specs=tok,
        out_shape=jax.ShapeDtypeStruct((t, d), F32),
        scratch_shapes=[pltpu.VMEM((tm, d), F32), pltpu.VMEM((2, tm, half), BF16)],
        compiler_params=_params("parallel", "arbitrary"),
        name="peer_dense",
    )(hb, u_t, u_t, v_tab, v_tab, w4, h, g, b)


def _layer(x2, batch, seq, w_in, w_gate_up, b_gate, gla_g, ret_g, w_out, ln1_g, ln1_b,
           w_q, subkeys, u_tab, v_tab, ln2_g, ln2_b):
    t, d = x2.shape
    row = lambda p: p.reshape(1, -1).astype(F32)

    w_up = jnp.pad(w_gate_up, ((0, LANES - GLA_GATE_RANK), (0, 0))).astype(BF16)
    bc, xb, w_tb = _gate_call(x2, jnp.swapaxes(w_in, 0, 1), w_up, row(b_gate), min(256, t))
    proj = _proj_call(xb, w_tb, min(1024, t), 1024)

    half = RET_DK // 2
    inv = ROPE_BASE ** (-jnp.arange(half, dtype=F32) / half)
    ang = jnp.arange(seq).astype(F32)[:, None] * inv[None, :]
    n_chunks = min(8, seq // CHUNK)
    mix = _mixer_call(proj, bc, jnp.cos(ang), jnp.sin(ang), row(gla_g), row(ret_g), batch, seq, n_chunks)

    h, hb = _out_ln_call(mix, w_out.astype(BF16), x2, row(ln1_g), row(ln1_b), min(512, t))

    sk = subkeys.reshape(PEER_HEADS * 2, PEER_NKEYS, -1).astype(BF16)
    a_sel, b_sel, bt_sel, gates, u_b, v_b = _route_call(hb, w_q.astype(BF16), sk, u_tab, v_tab, min(256, t))
    w3 = _wbuild_call(a_sel, b_sel, bt_sel, gates, min(128, t))
    w4 = w3.reshape(t // SUBLANES, _W_ROWS, SUBLANES, PEER_NKEYS)

    return _peer_call(hb, u_b, v_b, w4, h, row(ln2_g), row(ln2_b),
                      min(512, t), 1024)


def kernel(x, w_in, w_gla_gate_up, b_gla_gate, gla_norm_g, ret_norm_g, w_out, ln1_g, ln1_b,
           w_peer_q, peer_subkeys, peer_u, peer_v, ln2_g, ln2_b):
    batch, seq, d = x.shape
    h = x.reshape(batch * seq, d)
    for l in range(DEPTH):
        h = _layer(h, batch, seq, w_in[l], w_gla_gate_up[l], b_gla_gate[l], gla_norm_g[l], ret_norm_g[l],
                   w_out[l], ln1_g[l], ln1_b[l], w_peer_q[l], peer_subkeys[l], peer_u[l], peer_v[l],
                   ln2_g[l], ln2_b[l])
    return h.reshape(batch, seq, d)
```

```python
import functools
import math

import jax
import jax.numpy as jnp
from jax import lax
from jax.experimental import pallas as pl
from jax.experimental.pallas import tpu as pltpu

F32 = jnp.float32
BF16 = jnp.bfloat16

DEPTH = 1
CHUNK = 64
GLA_HEADS = 4
GLA_DK = 128
GLA_DV = 256
GLA_GATE_RANK = 16
GLA_TAU = 16.0
RET_HEADS = 4
RET_DK = 256
RET_DV = 256
ROPE_BASE = 10000.0
PEER_HEADS = 8
PEER_NKEYS = 128
PEER_TOPK = 16
LN_EPS = 1e-5
DEEPNORM_ALPHA = (2 * DEPTH) ** 0.25

LANES = 128
SUBLANES = 8
VMEM_LIMIT = 56 * 1024 * 1024
VMEM_LIMIT_DENSE = 61 * 1024 * 1024

_C_GQ = 0
_C_GK = _C_GQ + GLA_HEADS * GLA_DK
_C_GV = _C_GK + GLA_HEADS * GLA_DK
_C_GR = _C_GV + GLA_HEADS * GLA_DV
_C_RQ = _C_GR + GLA_HEADS * GLA_DV
_C_RK = _C_RQ + RET_HEADS * RET_DK
_C_RV = _C_RK + RET_HEADS * RET_DK
_C_RR = _C_RV + RET_HEADS * RET_DV
_C_END = _C_RR + RET_HEADS * RET_DV

_NT = (((1,), (1,)), ((), ()))
_TN = (((0,), (0,)), ((), ()))


def _params(*sem):
    return pltpu.CompilerParams(dimension_semantics=sem, vmem_limit_bytes=VMEM_LIMIT)


def _gate_kernel(x_ref, wl_ref, ws_ref, wu_ref, b_ref, o_ref, xb_ref, wb_ref, *, steps_per_slab):
    tm = x_ref.shape[0]
    xb = x_ref[...].astype(BF16)
    xb_ref[...] = xb

    @pl.when(pl.program_id(0) % steps_per_slab == 0)
    def _():
        wb_ref[...] = ws_ref[...].astype(BF16)

    glr = lax.dot_general(xb, wl_ref[...].astype(BF16), _NT, preferred_element_type=F32)
    lane = lax.broadcasted_iota(jnp.int32, glr.shape, 1)
    glr = jnp.where(lane < GLA_GATE_RANK, glr, 0.0)
    z = jnp.dot(glr.astype(BF16), wu_ref[...], preferred_element_type=F32) + b_ref[...]
    log_a = (jnp.minimum(z, 0.0) - jnp.log1p(jnp.exp(-jnp.abs(z)))) / GLA_TAU
    r = lax.broadcasted_iota(jnp.int32, (tm, tm), 0)
    c = lax.broadcasted_iota(jnp.int32, (tm, tm), 1)
    tri = jnp.where((r >= c) & ((r // CHUNK) == (c // CHUNK)), 1.0, 0.0).astype(F32)
    o_ref[...] = jnp.dot(tri, log_a, preferred_element_type=F32, precision=lax.Precision.HIGHEST)


def _gate_call(x, w_t, w_up, b_gate, tm):
    t, d = x.shape
    n = w_up.shape[1]
    n_steps = t // tm
    n_cols = w_t.shape[0]
    n_slabs = max(n_steps // 2, 1)
    slab = -(-n_cols // (n_slabs * 2 * SUBLANES)) * 2 * SUBLANES
    steps_per_slab = n_steps // n_slabs
    assert (n_slabs - 1) * slab < n_cols <= n_slabs * slab
    w_slab = pl.BlockSpec((slab, d), lambda i: (i // steps_per_slab, 0))
    return pl.pallas_call(
        functools.partial(_gate_kernel, steps_per_slab=steps_per_slab),
        grid=(n_steps,),
        in_specs=[pl.BlockSpec((tm, d), lambda i: (i, 0)),
                  pl.BlockSpec((LANES, d), lambda i: (_C_RQ // LANES, 0)),
                  w_slab,
                  pl.BlockSpec(w_up.shape, lambda i: (0, 0)),
                  pl.BlockSpec((1, n), lambda i: (0, 0))],
        out_specs=[pl.BlockSpec((tm, n), lambda i: (i, 0)),
                   pl.BlockSpec((tm, d), lambda i: (i, 0)),
                   w_slab],
        out_shape=[jax.ShapeDtypeStruct((t, n), F32), jax.ShapeDtypeStruct((t, d), BF16),
                   jax.ShapeDtypeStruct(w_t.shape, BF16)],
        compiler_params=_params("arbitrary"),
        name="gla_gate",
    )(x, w_t, w_t, w_up, b_gate)


def _proj_kernel(x_ref, w_ref, o_ref):
    o_ref[...] = lax.dot_general(x_ref[...], w_ref[...], _NT, preferred_element_type=F32).astype(o_ref.dtype)


def _proj_call(xb, w_tb, tm, tn):
    m, k = xb.shape
    n_gla_tiles = _C_RQ // tn
    assert n_gla_tiles * tn == _C_RQ
    w_row = lambda i, j: (pl.multiple_of(j * tn + jnp.where(j >= n_gla_tiles, GLA_GATE_RANK, 0), GLA_GATE_RANK), 0)
    return pl.pallas_call(
        _proj_kernel,
        grid=(m // tm, _C_END // tn),
        in_specs=[pl.BlockSpec((tm, k), lambda i, j: (i, 0)),
                  pl.BlockSpec((pl.Element(tn), pl.Element(k)), w_row)],
        out_specs=pl.BlockSpec((tm, tn), lambda i, j: (i, j)),
        out_shape=jax.ShapeDtypeStruct((m, _C_END), BF16),
        compiler_params=_params("parallel", "parallel"),
        name="proj_matmul",
    )(xb, w_tb)


def _silu(x):
    return x * jax.nn.sigmoid(x)


def _mixer_kernel(p_ref, bc_ref, cos_ref, sin_ref, gg_ref, rg_ref, o_ref, sg_ref, sr_ref, *, n_chunks):
    @pl.when(pl.program_id(1) == 0)
    def _():
        sg_ref[...] = jnp.zeros_like(sg_ref)
        sr_ref[...] = jnp.zeros_like(sr_ref)

    ri = lax.broadcasted_iota(jnp.int32, (CHUNK, CHUNK), 0)
    ci = lax.broadcasted_iota(jnp.int32, (CHUNK, CHUNK), 1)
    lower = ri >= ci
    dist = jnp.abs(ri - ci).astype(F32)
    row = lax.broadcasted_iota(jnp.int32, (CHUNK, RET_DK), 0).astype(F32)
    mid = CHUNK // 2

    def chunk_body(c, carry):
        rows = pl.ds(pl.multiple_of(c * CHUNK, CHUNK), CHUNK)

        def col(ref, base, h, width):
            return ref[rows, base + h * width: base + (h + 1) * width]

        for h in range(GLA_HEADS):
            q = col(p_ref, _C_GQ, h, GLA_DK).astype(F32) * (GLA_DK ** -0.5)
            k = col(p_ref, _C_GK, h, GLA_DK).astype(F32)
            v = col(p_ref, _C_GV, h, GLA_DV)
            gate = col(p_ref, _C_GR, h, GLA_DV).astype(F32)
            bc = col(bc_ref, 0, h, GLA_DK)
            b_last = bc[CHUNK - 1:CHUNK, :]
            b_mid = bc[mid:mid + 1, :]
            e_fwd = jnp.exp(bc - b_mid)
            e_bwd = jnp.exp(b_mid - bc)
            st = sg_ref[h]
            o = lax.dot_general((q * jnp.exp(bc)).astype(BF16), st.astype(BF16), _NT,
                                preferred_element_type=F32)
            a_lo = lax.dot_general((q * e_fwd).astype(BF16), (k * e_bwd).astype(BF16), _NT,
                                   preferred_element_type=F32)
            a_up = lax.dot_general((q * e_bwd).astype(BF16), (k * e_fwd).astype(BF16), _NT,
                                   preferred_element_type=F32)
            a = jnp.where(lower, a_lo, a_up)
            o = o + jnp.dot(a.astype(BF16), v, preferred_element_type=F32)
            u_t = lax.dot_general(v, (k * jnp.exp(b_last - bc)).astype(BF16), _TN,
                                  preferred_element_type=F32)
            sg_ref[h] = st * jnp.exp(b_last) + u_t
            y = o * lax.rsqrt(jnp.mean(o * o, axis=-1, keepdims=True) + LN_EPS)
            y = y * gg_ref[:, h * GLA_DV:(h + 1) * GLA_DV] * _silu(gate)
            o_ref[rows, h * GLA_DV:(h + 1) * GLA_DV] = y.astype(o_ref.dtype)

        cos = cos_ref[rows, :]
        sin = sin_ref[rows, :]
        half = RET_DK // 2

        def rotary(t):
            t1, t2 = t[:, :half], t[:, half:]
            return jnp.concatenate([t1 * cos - t2 * sin, t1 * sin + t2 * cos], axis=-1)

        for h in range(RET_HEADS):
            log_gamma = math.log(1.0 - 2.0 ** (-5.0 - h))
            q = rotary(col(p_ref, _C_RQ, h, RET_DK).astype(F32))
            k = rotary(col(p_ref, _C_RK, h, RET_DK).astype(F32)) * (RET_DK ** -0.5)
            v = col(p_ref, _C_RV, h, RET_DV)
            gate = col(p_ref, _C_RR, h, RET_DV).astype(F32)
            qb = q.astype(BF16)
            st = sr_ref[h]
            o = lax.dot_general(qb, st.astype(BF16), _NT, preferred_element_type=F32)
            o = o * jnp.exp(log_gamma * (row + 1.0))
            a = lax.dot_general(qb, k.astype(BF16), _NT, preferred_element_type=F32)
            a = a * jnp.exp(log_gamma * dist)
            o = o + jnp.dot(a.astype(BF16), v, preferred_element_type=F32)
            k_dec = jnp.exp(log_gamma * (CHUNK - 1.0 - row))
            u_t = lax.dot_general(v, (k * k_dec).astype(BF16), _TN, preferred_element_type=F32)
            sr_ref[h] = st * math.exp(log_gamma * CHUNK) + u_t
            oc = o - jnp.mean(o, axis=-1, keepdims=True)
            y = oc * lax.rsqrt(jnp.mean(oc * oc, axis=-1, keepdims=True) + LN_EPS)
            y = y * rg_ref[:, h * RET_DV:(h + 1) * RET_DV] * _silu(gate)
            c0 = GLA_HEADS * GLA_DV + h * RET_DV
            o_ref[rows, c0:c0 + RET_DV] = y.astype(o_ref.dtype)
        return carry

    lax.fori_loop(0, n_chunks, chunk_body, 0, unroll=2)


def _mixer_call(proj, bc, cos, sin, gla_g, ret_g, batch, seq, n_chunks):
    t = proj.shape[0]
    tb = n_chunks * CHUNK
    groups = seq // tb
    width = GLA_HEADS * GLA_DV + RET_HEADS * RET_DV
    tok = lambda b, g: (b * groups + g, 0)
    return pl.pallas_call(
        functools.partial(_mixer_kernel, n_chunks=n_chunks),
        grid=(batch, groups),
        in_specs=[pl.BlockSpec((tb, proj.shape[1]), tok),
                  pl.BlockSpec((tb, bc.shape[1]), tok),
                  pl.BlockSpec((tb, cos.shape[1]), lambda b, g: (g, 0)),
                  pl.BlockSpec((tb, sin.shape[1]), lambda b, g: (g, 0)),
                  pl.BlockSpec(gla_g.shape, lambda b, g: (0, 0)),
                  pl.BlockSpec(ret_g.shape, lambda b, g: (0, 0))],
        out_specs=pl.BlockSpec((tb, width), tok),
        out_shape=jax.ShapeDtypeStruct((t, width), BF16),
        scratch_shapes=[pltpu.VMEM((GLA_HEADS, GLA_DV, GLA_DK), F32),
                        pltpu.VMEM((RET_HEADS, RET_DV, RET_DK), F32)],
        compiler_params=_params("parallel", "arbitrary"),
        name="mixer",
    )(proj, bc, cos, sin, gla_g, ret_g)


def _layer_norm(y, g, b):
    mu = jnp.mean(y, axis=-1, keepdims=True)
    yc = y - mu
    var = jnp.mean(yc * yc, axis=-1, keepdims=True)
    return yc * lax.rsqrt(var + LN_EPS) * g + b


def _out_ln_kernel(m_ref, w_ref, x_ref, g_ref, b_ref, h_ref, hb_ref):
    half = m_ref.shape[0] // 2
    for r in range(2):
        rows = slice(r * half, (r + 1) * half)
        mix = jnp.dot(m_ref[rows, :], w_ref[...], preferred_element_type=F32)
        h = _layer_norm(DEEPNORM_ALPHA * x_ref[rows, :] + mix, g_ref[...], b_ref[...])
        h_ref[rows, :] = h
        hb_ref[rows, :] = h.astype(BF16)


def _out_ln_call(mix, w_out, x, g, b, tm):
    t, d = x.shape
    return pl.pallas_call(
        _out_ln_kernel,
        grid=(t // tm,),
        in_specs=[pl.BlockSpec((tm, mix.shape[1]), lambda i: (i, 0)),
                  pl.BlockSpec(w_out.shape, lambda i: (0, 0)),
                  pl.BlockSpec((tm, d), lambda i: (i, 0)),
                  pl.BlockSpec((1, d), lambda i: (0, 0)),
                  pl.BlockSpec((1, d), lambda i: (0, 0))],
        out_specs=[pl.BlockSpec((tm, d), lambda i: (i, 0)),
                   pl.BlockSpec((tm, d), lambda i: (i, 0))],
        out_shape=[jax.ShapeDtypeStruct((t, d), F32), jax.ShapeDtypeStruct((t, d), BF16)],
        compiler_params=_params("parallel"),
        name="out_proj_ln",
    )(mix, w_out, x, g, b)


def _topk_paired(s, key, k):
    h = s.shape[0] // 2
    if key is None:
        key_lo = lax.broadcasted_iota(jnp.int32, (h, s.shape[1]), 0).astype(F32)
        key_hi = key_lo + float(h)
    else:
        key_lo, key_hi = key[:h, :], key[h:, :]
    swap = s[h:, :] > s[:h, :]
    front = jnp.where(swap, s[h:, :], s[:h, :])
    back = jnp.where(swap, s[:h, :], s[h:, :])
    front_key = jnp.where(swap, key_hi, key_lo)
    back_key = jnp.where(swap, key_lo, key_hi)
    vals, keys = [], []
    for _ in range(k):
        m = jnp.max(front, axis=0, keepdims=True)
        pick = jnp.min(jnp.where(front == m, front_key, jnp.inf), axis=0, keepdims=True)
        hit = front_key == pick
        vals.append(m)
        keys.append(pick)
        front = jnp.where(hit, back, front)
        front_key = jnp.where(hit, back_key, front_key)
        back = jnp.where(hit, -jnp.inf, back)
    return jnp.concatenate(vals, axis=0), jnp.concatenate(keys, axis=0)


def _route_kernel(h_ref, wq_ref, sk_ref, u_ref, v_ref, a_ref, b_ref, bt_ref, g_ref, ub_ref, vb_ref):
    ub_ref[...] = u_ref[...].T.astype(BF16)
    vb_ref[...] = v_ref[...].astype(BF16)
    kk = PEER_TOPK
    q = jnp.dot(h_ref[...], wq_ref[...], preferred_element_type=F32).astype(BF16)
    n_tok = q.shape[0]
    n_exp = PEER_NKEYS * PEER_NKEYS
    n_cand = sum(kk // (r1 + 1) for r1 in range(kk))
    pad = (-n_cand) % (2 * SUBLANES)
    cand_row = lax.broadcasted_iota(jnp.int32, (n_cand + pad, n_tok), 0).astype(F32) * float(n_exp)
    experts, gates = [], []
    for h in range(PEER_HEADS):
        vals, idxs = [], []
        for p in range(2):
            j = 2 * h + p
            s_t = lax.dot_general(sk_ref[j], q[:, j * LANES:(j + 1) * LANES], _NT,
                                  preferred_element_type=F32)
            v_p, i_p = _topk_paired(s_t, None, kk)
            vals.append(v_p)
            idxs.append(i_p)
        cand, expert = [], []
        for r1 in range(kk):
            n2 = kk // (r1 + 1)
            cand.append(vals[0][r1:r1 + 1, :] + vals[1][:n2, :])
            expert.append(idxs[0][r1:r1 + 1, :] * float(PEER_NKEYS) + idxs[1][:n2, :])
        cand.append(jnp.full((pad, n_tok), -jnp.inf, F32))
        expert.append(jnp.zeros((pad, n_tok), F32))
        top_s, top_key = _topk_paired(jnp.concatenate(cand, axis=0),
                                      cand_row + jnp.concatenate(expert, axis=0), kk)
        e = jnp.exp(top_s - top_s[0:1, :])
        experts.append(top_key)
        gates.append(e / jnp.sum(e, axis=0, keepdims=True))
    top_e = jnp.concatenate(experts, axis=0).astype(jnp.int32) & (n_exp - 1)
    a_ref[...] = (top_e >> (PEER_NKEYS.bit_length() - 1)).T
    second = top_e & (PEER_NKEYS - 1)
    b_ref[...] = second.T
    bt_ref[...] = second
    g_ref[...] = jnp.concatenate(gates, axis=0).T


def _route_call(hb, w_q, subkeys, u_tab, v_tab, tt):
    t, d = hb.shape
    n_sel = PEER_HEADS * PEER_TOPK
    n_steps = t // tt
    n_exp, d_exp = u_tab.shape
    slab = n_exp // n_steps
    assert slab * n_steps == n_exp and slab % (2 * SUBLANES) == 0
    tok_major = pl.BlockSpec((tt, n_sel), lambda i: (i, 0))
    table = pl.BlockSpec((slab, d_exp), lambda i: (i, 0))
    return pl.pallas_call(
        _route_kernel,
        grid=(n_steps,),
        in_specs=[pl.BlockSpec((tt, d), lambda i: (i, 0)),
                  pl.BlockSpec(w_q.shape, lambda i: (0, 0)),
                  pl.BlockSpec(subkeys.shape, lambda i: (0, 0, 0)),
                  table, table],
        out_specs=[tok_major, tok_major, pl.BlockSpec((n_sel, tt), lambda i: (0, i)), tok_major,
                   pl.BlockSpec((d_exp, slab), lambda i: (0, i)), table],
        out_shape=[jax.ShapeDtypeStruct((t, n_sel), jnp.int32), jax.ShapeDtypeStruct((t, n_sel), jnp.int32),
                   jax.ShapeDtypeStruct((n_sel, t), jnp.int32), jax.ShapeDtypeStruct((t, n_sel), F32),
                   jax.ShapeDtypeStruct((d_exp, n_exp), BF16), jax.ShapeDtypeStruct(v_tab.shape, BF16)],
        compiler_params=_params("parallel"),
        name="peer_route",
    )(hb, w_q, subkeys, u_tab, v_tab)


_W_ROWS = PEER_NKEYS // 2


def _wbuild_kernel(a_ref, b_ref, bt_ref, g_ref, w_ref):
    n_tok, n_sel = a_ref.shape
    key_rows = lax.broadcasted_iota(jnp.int32, (PEER_NKEYS, n_sel), 0)
    key_cols = lax.broadcasted_iota(jnp.int32, (n_sel, PEER_NKEYS), 1)
    for t in range(n_tok):
        pa = jnp.where(key_rows == a_ref[t:t + 1, :], g_ref[t:t + 1, :], 0.0).astype(BF16)
        if t % 2:
            pb = jnp.where(key_cols == bt_ref[:, t:t + 1], 1.0, 0.0).astype(BF16)
            w = jnp.dot(pa, pb, preferred_element_type=F32)
        else:
            pb = jnp.where(key_rows == b_ref[t:t + 1, :], 1.0, 0.0).astype(BF16)
            w = lax.dot_general(pa, pb, _NT, preferred_element_type=F32)
        words = pltpu.pack_elementwise([w[:_W_ROWS, :], w[_W_ROWS:, :]], packed_dtype=BF16)
        w_ref[t // SUBLANES, pl.ds(t % SUBLANES, _W_ROWS, stride=SUBLANES), :] = (
            lax.bitcast_convert_type(words, jnp.uint32))


def _wbuild_call(a_tm, b_tm, b_sm, g_tm, tt):
    t, n_sel = a_tm.shape
    rows = _W_ROWS * SUBLANES
    tok_major = pl.BlockSpec((tt, n_sel), lambda i: (i, 0))
    return pl.pallas_call(
        _wbuild_kernel,
        grid=(t // tt,),
        in_specs=[tok_major, tok_major, pl.BlockSpec((n_sel, tt), lambda i: (0, i)), tok_major],
        out_specs=pl.BlockSpec((tt // SUBLANES, rows, PEER_NKEYS), lambda i: (i, 0, 0)),
        out_shape=jax.ShapeDtypeStruct((t // SUBLANES, rows, PEER_NKEYS), jnp.uint32),
        compiler_params=_params("parallel"),
        name="peer_gate_map",
    )(a_tm, b_tm, b_sm, g_tm)


def _gelu(x):
    return 0.5 * x * (1.0 + lax.erf(x * math.sqrt(0.5)))


def _peer_kernel(h_ref, ul_ref, uh_ref, vl_ref, vh_ref, w_ref, g_ref, b_ref, o_ref, hb_ref, gs_ref):
    j = pl.program_id(1)

    @pl.when(j == 0)
    def _():
        hb_ref[...] = h_ref[...].astype(BF16)
        o_ref[...] = jnp.zeros_like(o_ref)

    tm = hb_ref.shape[0]
    half = vl_ref.shape[0]
    hb = hb_ref[...]
    for part, u_ref in enumerate((ul_ref, uh_ref)):
        act = _gelu(jnp.dot(hb, u_ref[...], preferred_element_type=F32))
        for a in range(half // PEER_NKEYS):
            word = w_ref[:, a, :, :].reshape(tm, PEER_NKEYS)
            w_a = pltpu.unpack_elementwise(word, index=part, packed_dtype=BF16, unpacked_dtype=F32)
            cols = slice(a * PEER_NKEYS, (a + 1) * PEER_NKEYS)
            gs_ref[part, :, cols] = (w_a * act[:, cols]).astype(BF16)
    o_ref[...] += (jnp.dot(gs_ref[0], vl_ref[...], preferred_element_type=F32)
                   + jnp.dot(gs_ref[1], vh_ref[...], preferred_element_type=F32))

    @pl.when(j == pl.num_programs(1) - 1)
    def _():
        o_ref[...] = _layer_norm(DEEPNORM_ALPHA * h_ref[...] + o_ref[...], g_ref[...], b_ref[...])


def _peer_call(u_t, v_tab, w4, h, g, b, tm, tn):
    t, d = h.shape
    n_exp = v_tab.shape[0]
    half = tn // 2
    n_steps = n_exp // tn
    lo = pl.BlockSpec((half, d), lambda i, j: (j, 0))
    hi = pl.BlockSpec((half, d), lambda i, j: (j + n_steps, 0))
    tok = pl.BlockSpec((tm, d), lambda i, j: (i, 0))
    vec = pl.BlockSpec((1, d), lambda i, j: (0, 0))
    return pl.pallas_call(
        _peer_kernel,
        grid=(t // tm, n_steps),
        in_specs=[pl.BlockSpec((tm, d), lambda i, j: (i, 0), pipeline_mode=pl.Buffered(1)),
                  pl.BlockSpec((d, half), lambda i, j: (0, j)),
                  pl.BlockSpec((d, half), lambda i, j: (0, j + n_steps)), lo, hi,
                  pl.BlockSpec((tm // SUBLANES, half // PEER_NKEYS, SUBLANES, PEER_NKEYS),
                               lambda i, j: (i, j, 0, 0)),
                  vec, vec],
        out_specs=tok,
        out_shape=jax.ShapeDtypeStruct((t, d), F32),
        scratch_shapes=[pltpu.VMEM((tm, d), BF16), pltpu.VMEM((2, tm, half), BF16)],
        compiler_params=pltpu.CompilerParams(dimension_semantics=("parallel", "arbitrary"),
                                             vmem_limit_bytes=VMEM_LIMIT_DENSE),
        name="peer_dense",
    )(h, u_t, u_t, v_tab, v_tab, w4, g, b)


def _layer(x2, batch, seq, w_in, w_gate_up, b_gate, gla_g, ret_g, w_out, ln1_g, ln1_b,
           w_q, subkeys, u_tab, v_tab, ln2_g, ln2_b):
    t, d = x2.shape
    row = lambda p: p.reshape(1, -1).astype(F32)

    w_up = jnp.pad(w_gate_up, ((0, LANES - GLA_GATE_RANK), (0, 0))).astype(BF16)
    bc, xb, w_tb = _gate_call(x2, jnp.swapaxes(w_in, 0, 1), w_up, row(b_gate), min(256, t))
    proj = _proj_call(xb, w_tb, min(1024, t), 1024)

    half = RET_DK // 2
    inv = ROPE_BASE ** (-jnp.arange(half, dtype=F32) / half)
    ang = jnp.arange(seq).astype(F32)[:, None] * inv[None, :]
    n_chunks = min(8, seq // CHUNK)
    mix = _mixer_call(proj, bc, jnp.cos(ang), jnp.sin(ang), row(gla_g), row(ret_g), batch, seq, n_chunks)

    h, hb = _out_ln_call(mix, w_out.astype(BF16), x2, row(ln1_g), row(ln1_b), min(512, t))

    sk = subkeys.reshape(PEER_HEADS * 2, PEER_NKEYS, -1).astype(BF16)
    a_sel, b_sel, bt_sel, gates, u_b, v_b = _route_call(hb, w_q.astype(BF16), sk, u_tab, v_tab, min(256, t))
    w3 = _wbuild_call(a_sel, b_sel, bt_sel, gates, min(128, t))
    w4 = w3.reshape(t // SUBLANES, _W_ROWS, SUBLANES, PEER_NKEYS)

    return _peer_call(u_b, v_b, w4, h, row(ln2_g), row(ln2_b), min(512, t), 2048)


def kernel(x, w_in, w_gla_gate_up, b_gla_gate, gla_norm_g, ret_norm_g, w_out, ln1_g, ln1_b,
           w_peer_q, peer_subkeys, peer_u, peer_v, ln2_g, ln2_b):
    batch, seq, d = x.shape
    h = x.reshape(batch * seq, d)
    for l in range(DEPTH):
        h = _layer(h, batch, seq, w_in[l], w_gla_gate_up[l], b_gla_gate[l], gla_norm_g[l], ret_norm_g[l],
                   w_out[l], ln1_g[l], ln1_b[l], w_peer_q[l], peer_subkeys[l], peer_u[l], peer_v[l],
                   ln2_g[l], ln2_b[l])
    return h.reshape(batch, seq, d)
```

```python
import functools
import math
from typing import NamedTuple

import jax
import jax.numpy as jnp
from jax import lax
from jax.experimental import pallas as pl
from jax.experimental.pallas import tpu as pltpu

F32 = jnp.float32
BF16 = jnp.bfloat16

DEPTH = 1
CHUNK = 64
GLA_HEADS = 4
GLA_DK = 128
GLA_DV = 256
GLA_GATE_RANK = 16
GLA_TAU = 16.0
RET_HEADS = 4
RET_DK = 256
RET_DV = 256
ROPE_BASE = 10000.0
PEER_HEADS = 8
PEER_NKEYS = 128
PEER_TOPK = 16
LN_EPS = 1e-5
DEEPNORM_ALPHA = (2 * DEPTH) ** 0.25

LANES = 128
SUBLANES = 8
VMEM_LIMIT = 56 * 1024 * 1024

_C_GQ = 0
_C_GK = _C_GQ + GLA_HEADS * GLA_DK
_C_GV = _C_GK + GLA_HEADS * GLA_DK
_C_GR = _C_GV + GLA_HEADS * GLA_DV
_C_RQ = _C_GR + GLA_HEADS * GLA_DV
_C_RK = _C_RQ + RET_HEADS * RET_DK
_C_RV = _C_RK + RET_HEADS * RET_DK
_C_RR = _C_RV + RET_HEADS * RET_DV
_C_END = _C_RR + RET_HEADS * RET_DV

_NT = (((1,), (1,)), ((), ()))
_TN = (((0,), (0,)), ((), ()))


class _Tiles(NamedTuple):
    gate_tokens: int
    proj_tokens: int
    proj_cols: int
    mixer_chunks: int
    out_tokens: int
    route_tokens: int
    map_tokens: int
    dense_tokens: int
    dense_experts: int


def _tiles(t, seq):
    return _Tiles(gate_tokens=min(256, t), proj_tokens=min(1024, t), proj_cols=1024,
                  mixer_chunks=min(8, seq // CHUNK), out_tokens=min(512, t), route_tokens=min(256, t),
                  map_tokens=min(256, t), dense_tokens=min(512, t), dense_experts=1024)


def _params(*sem):
    return pltpu.CompilerParams(dimension_semantics=sem, vmem_limit_bytes=VMEM_LIMIT)


def _gate_kernel(x_ref, wl_ref, wu_ref, b_ref, o_ref, xb_ref):
    tm = x_ref.shape[0]
    xb = x_ref[...].astype(BF16)
    xb_ref[...] = xb
    glr = lax.dot_general(xb, wl_ref[...].astype(BF16), _NT, preferred_element_type=F32)
    lane = lax.broadcasted_iota(jnp.int32, glr.shape, 1)
    glr = jnp.where(lane < GLA_GATE_RANK, glr, 0.0)
    z = jnp.dot(glr.astype(BF16), wu_ref[...], preferred_element_type=F32) + b_ref[...]
    log_a = (jnp.minimum(z, 0.0) - jnp.log1p(jnp.exp(-jnp.abs(z)))) / GLA_TAU
    r = lax.broadcasted_iota(jnp.int32, (tm, tm), 0)
    c = lax.broadcasted_iota(jnp.int32, (tm, tm), 1)
    tri = jnp.where((r >= c) & ((r // CHUNK) == (c // CHUNK)), 1.0, 0.0).astype(F32)
    o_ref[...] = jnp.dot(tri, log_a, preferred_element_type=F32, precision=lax.Precision.HIGHEST)


def _gate_call(x, w_t, w_up, b_gate, tm):
    t, d = x.shape
    n = w_up.shape[1]
    return pl.pallas_call(
        _gate_kernel,
        grid=(t // tm,),
        in_specs=[pl.BlockSpec((tm, d), lambda i: (i, 0)),
                  pl.BlockSpec((LANES, d), lambda i: (_C_RQ // LANES, 0)),
                  pl.BlockSpec(w_up.shape, lambda i: (0, 0)),
                  pl.BlockSpec((1, n), lambda i: (0, 0))],
        out_specs=[pl.BlockSpec((tm, n), lambda i: (i, 0)),
                   pl.BlockSpec((tm, d), lambda i: (i, 0))],
        out_shape=[jax.ShapeDtypeStruct((t, n), F32), jax.ShapeDtypeStruct((t, d), BF16)],
        compiler_params=_params("parallel"),
        name="gla_gate",
    )(x, w_t, w_up, b_gate)


def _proj_kernel(x_ref, w_ref, o_ref):
    o_ref[...] = lax.dot_general(x_ref[...], w_ref[...].astype(BF16), _NT,
                                 preferred_element_type=F32).astype(o_ref.dtype)


def _proj_call(xb, w_t, tm, tn):
    m, k = xb.shape
    n_gla_tiles = _C_RQ // tn
    assert n_gla_tiles * tn == _C_RQ
    w_row = lambda i, j: (pl.multiple_of(j * tn + jnp.where(j >= n_gla_tiles, GLA_GATE_RANK, 0), GLA_GATE_RANK), 0)
    return pl.pallas_call(
        _proj_kernel,
        grid=(m // tm, _C_END // tn),
        in_specs=[pl.BlockSpec((tm, k), lambda i, j: (i, 0)),
                  pl.BlockSpec((pl.Element(tn), pl.Element(k)), w_row)],
        out_specs=pl.BlockSpec((tm, tn), lambda i, j: (i, j)),
        out_shape=jax.ShapeDtypeStruct((m, _C_END), BF16),
        compiler_params=_params("parallel", "parallel"),
        name="proj_matmul",
    )(xb, w_t)


def _silu(x):
    return x * jax.nn.sigmoid(x)


def _mixer_kernel(p_ref, bc_ref, cos_ref, sin_ref, gg_ref, rg_ref, o_ref, sg_ref, sr_ref, *, n_chunks):
    @pl.when(pl.program_id(1) == 0)
    def _():
        sg_ref[...] = jnp.zeros_like(sg_ref)
        sr_ref[...] = jnp.zeros_like(sr_ref)

    ri = lax.broadcasted_iota(jnp.int32, (CHUNK, CHUNK), 0)
    ci = lax.broadcasted_iota(jnp.int32, (CHUNK, CHUNK), 1)
    lower = ri >= ci
    dist = jnp.abs(ri - ci).astype(F32)
    row = lax.broadcasted_iota(jnp.int32, (CHUNK, RET_DK), 0).astype(F32)
    mid = CHUNK // 2

    def chunk_body(c, carry):
        rows = pl.ds(pl.multiple_of(c * CHUNK, CHUNK), CHUNK)

        def col(ref, base, h, width):
            return ref[rows, base + h * width: base + (h + 1) * width]

        for h in range(GLA_HEADS):
            q = col(p_ref, _C_GQ, h, GLA_DK).astype(F32) * (GLA_DK ** -0.5)
            k = col(p_ref, _C_GK, h, GLA_DK).astype(F32)
            v = col(p_ref, _C_GV, h, GLA_DV)
            gate = col(p_ref, _C_GR, h, GLA_DV).astype(F32)
            bc = col(bc_ref, 0, h, GLA_DK)
            b_last = bc[CHUNK - 1:CHUNK, :]
            b_mid = bc[mid:mid + 1, :]
            e_fwd = jnp.exp(bc - b_mid)
            e_bwd = jnp.exp(b_mid - bc)
            st = sg_ref[h]
            o = lax.dot_general((q * jnp.exp(bc)).astype(BF16), st.astype(BF16), _NT,
                                preferred_element_type=F32)
            a_lo = lax.dot_general((q * e_fwd).astype(BF16), (k * e_bwd).astype(BF16), _NT,
                                   preferred_element_type=F32)
            a_up = lax.dot_general((q * e_bwd).astype(BF16), (k * e_fwd).astype(BF16), _NT,
                                   preferred_element_type=F32)
            a = jnp.where(lower, a_lo, a_up)
            o = o + jnp.dot(a.astype(BF16), v, preferred_element_type=F32)
            u_t = lax.dot_general(v, (k * jnp.exp(b_last - bc)).astype(BF16), _TN,
                                  preferred_element_type=F32)
            sg_ref[h] = st * jnp.exp(b_last) + u_t
            y = o * lax.rsqrt(jnp.mean(o * o, axis=-1, keepdims=True) + LN_EPS)
            y = y * gg_ref[:, h * GLA_DV:(h + 1) * GLA_DV] * _silu(gate)
            o_ref[rows, h * GLA_DV:(h + 1) * GLA_DV] = y.astype(o_ref.dtype)

        cos = cos_ref[rows, :]
        sin = sin_ref[rows, :]
        half = RET_DK // 2

        def rotary(t):
            t1, t2 = t[:, :half], t[:, half:]
            return jnp.concatenate([t1 * cos - t2 * sin, t1 * sin + t2 * cos], axis=-1)

        for h in range(RET_HEADS):
            log_gamma = math.log(1.0 - 2.0 ** (-5.0 - h))
            q = rotary(col(p_ref, _C_RQ, h, RET_DK).astype(F32))
            k = rotary(col(p_ref, _C_RK, h, RET_DK).astype(F32)) * (RET_DK ** -0.5)
            v = col(p_ref, _C_RV, h, RET_DV)
            gate = col(p_ref, _C_RR, h, RET_DV).astype(F32)
            qb = q.astype(BF16)
            st = sr_ref[h]
            o = lax.dot_general(qb, st.astype(BF16), _NT, preferred_element_type=F32)
            o = o * jnp.exp(log_gamma * (row + 1.0))
            a = lax.dot_general(qb, k.astype(BF16), _NT, preferred_element_type=F32)
            a = a * jnp.exp(log_gamma * dist)
            o = o + jnp.dot(a.astype(BF16), v, preferred_element_type=F32)
            k_dec = jnp.exp(log_gamma * (CHUNK - 1.0 - row))
            u_t = lax.dot_general(v, (k * k_dec).astype(BF16), _TN, preferred_element_type=F32)
            sr_ref[h] = st * math.exp(log_gamma * CHUNK) + u_t
            oc = o - jnp.mean(o, axis=-1, keepdims=True)
            y = oc * lax.rsqrt(jnp.mean(oc * oc, axis=-1, keepdims=True) + LN_EPS)
            y = y * rg_ref[:, h * RET_DV:(h + 1) * RET_DV] * _silu(gate)
            c0 = GLA_HEADS * GLA_DV + h * RET_DV
            o_ref[rows, c0:c0 + RET_DV] = y.astype(o_ref.dtype)
        return carry

    lax.fori_loop(0, n_chunks, chunk_body, 0, unroll=2)


def _mixer_call(proj, bc, cos, sin, gla_g, ret_g, batch, seq, n_chunks):
    t = proj.shape[0]
    tb = n_chunks * CHUNK
    groups = seq // tb
    width = GLA_HEADS * GLA_DV + RET_HEADS * RET_DV
    tok = lambda b, g: (b * groups + g, 0)
    return pl.pallas_call(
        functools.partial(_mixer_kernel, n_chunks=n_chunks),
        grid=(batch, groups),
        in_specs=[pl.BlockSpec((tb, proj.shape[1]), tok),
                  pl.BlockSpec((tb, bc.shape[1]), tok),
                  pl.BlockSpec((tb, cos.shape[1]), lambda b, g: (g, 0)),
                  pl.BlockSpec((tb, sin.shape[1]), lambda b, g: (g, 0)),
                  pl.BlockSpec(gla_g.shape, lambda b, g: (0, 0)),
                  pl.BlockSpec(ret_g.shape, lambda b, g: (0, 0))],
        out_specs=pl.BlockSpec((tb, width), tok),
        out_shape=jax.ShapeDtypeStruct((t, width), BF16),
        scratch_shapes=[pltpu.VMEM((GLA_HEADS, GLA_DV, GLA_DK), F32),
                        pltpu.VMEM((RET_HEADS, RET_DV, RET_DK), F32)],
        compiler_params=_params("parallel", "arbitrary"),
        name="mixer",
    )(proj, bc, cos, sin, gla_g, ret_g)


def _layer_norm(y, g, b):
    mu = jnp.mean(y, axis=-1, keepdims=True)
    yc = y - mu
    var = jnp.mean(yc * yc, axis=-1, keepdims=True)
    return yc * lax.rsqrt(var + LN_EPS) * g + b


def _out_ln_kernel(m_ref, w_ref, x_ref, g_ref, b_ref, h_ref, hb_ref):
    half = m_ref.shape[0] // 2
    for r in range(2):
        rows = slice(r * half, (r + 1) * half)
        mix = jnp.dot(m_ref[rows, :], w_ref[...], preferred_element_type=F32)
        h = _layer_norm(DEEPNORM_ALPHA * x_ref[rows, :] + mix, g_ref[...], b_ref[...])
        h_ref[rows, :] = h
        hb_ref[rows, :] = h.astype(BF16)


def _out_ln_call(mix, w_out, x, g, b, tm):
    t, d = x.shape
    return pl.pallas_call(
        _out_ln_kernel,
        grid=(t // tm,),
        in_specs=[pl.BlockSpec((tm, mix.shape[1]), lambda i: (i, 0)),
                  pl.BlockSpec(w_out.shape, lambda i: (0, 0)),
                  pl.BlockSpec((tm, d), lambda i: (i, 0)),
                  pl.BlockSpec((1, d), lambda i: (0, 0)),
                  pl.BlockSpec((1, d), lambda i: (0, 0))],
        out_specs=[pl.BlockSpec((tm, d), lambda i: (i, 0)),
                   pl.BlockSpec((tm, d), lambda i: (i, 0))],
        out_shape=[jax.ShapeDtypeStruct((t, d), F32), jax.ShapeDtypeStruct((t, d), BF16)],
        compiler_params=_params("parallel"),
        name="out_proj_ln",
    )(mix, w_out, x, g, b)


def _topk_paired(s, key, k):
    h = s.shape[0] // 2
    if key is None:
        key_lo = lax.broadcasted_iota(jnp.int32, (h, s.shape[1]), 0).astype(F32)
        key_hi = key_lo + float(h)
    else:
        key_lo, key_hi = key[:h, :], key[h:, :]
    swap = s[h:, :] > s[:h, :]
    front = jnp.where(swap, s[h:, :], s[:h, :])
    back = jnp.where(swap, s[:h, :], s[h:, :])
    front_key = jnp.where(swap, key_hi, key_lo)
    back_key = jnp.where(swap, key_lo, key_hi)
    vals, keys = [], []
    for _ in range(k):
        m = jnp.max(front, axis=0, keepdims=True)
        pick = jnp.min(jnp.where(front == m, front_key, jnp.inf), axis=0, keepdims=True)
        hit = front_key == pick
        vals.append(m)
        keys.append(pick)
        front = jnp.where(hit, back, front)
        front_key = jnp.where(hit, back_key, front_key)
        back = jnp.where(hit, -jnp.inf, back)
    return jnp.concatenate(vals, axis=0), jnp.concatenate(keys, axis=0)


def _route_kernel(h_ref, wq_ref, sk_ref, u_ref, v_ref, a_ref, b_ref, bt_ref, g_ref, ub_ref, vb_ref):
    ub_ref[...] = u_ref[...].T.astype(BF16)
    vb_ref[...] = v_ref[...].astype(BF16)
    kk = PEER_TOPK
    q = jnp.dot(h_ref[...], wq_ref[...], preferred_element_type=F32).astype(BF16)
    n_tok = q.shape[0]
    n_exp = PEER_NKEYS * PEER_NKEYS
    n_cand = sum(kk // (r1 + 1) for r1 in range(kk))
    pad = (-n_cand) % (2 * SUBLANES)
    cand_row = lax.broadcasted_iota(jnp.int32, (n_cand + pad, n_tok), 0).astype(F32) * float(n_exp)
    experts, gates = [], []
    for h in range(PEER_HEADS):
        vals, idxs = [], []
        for p in range(2):
            j = 2 * h + p
            s_t = lax.dot_general(sk_ref[j], q[:, j * LANES:(j + 1) * LANES], _NT,
                                  preferred_element_type=F32)
            v_p, i_p = _topk_paired(s_t, None, kk)
            vals.append(v_p)
            idxs.append(i_p)
        cand, expert = [], []
        for r1 in range(kk):
            n2 = kk // (r1 + 1)
            cand.append(vals[0][r1:r1 + 1, :] + vals[1][:n2, :])
            expert.append(idxs[0][r1:r1 + 1, :] * float(PEER_NKEYS) + idxs[1][:n2, :])
        cand.append(jnp.full((pad, n_tok), -jnp.inf, F32))
        expert.append(jnp.zeros((pad, n_tok), F32))
        top_s, top_key = _topk_paired(jnp.concatenate(cand, axis=0),
                                      cand_row + jnp.concatenate(expert, axis=0), kk)
        e = jnp.exp(top_s - top_s[0:1, :])
        experts.append(top_key)
        gates.append(e / jnp.sum(e, axis=0, keepdims=True))
    top_e = jnp.concatenate(experts, axis=0).astype(jnp.int32) & (n_exp - 1)
    a_ref[...] = (top_e >> (PEER_NKEYS.bit_length() - 1)).T
    second = top_e & (PEER_NKEYS - 1)
    b_ref[...] = second.T
    bt_ref[...] = second
    g_ref[...] = jnp.concatenate(gates, axis=0).T


def _route_call(hb, w_q, subkeys, u_tab, v_tab, tt):
    t, d = hb.shape
    n_sel = PEER_HEADS * PEER_TOPK
    n_steps = t // tt
    n_exp, d_exp = u_tab.shape
    slab = n_exp // n_steps
    assert slab * n_steps == n_exp and slab % (2 * SUBLANES) == 0
    tok_major = pl.BlockSpec((tt, n_sel), lambda i: (i, 0))
    table = pl.BlockSpec((slab, d_exp), lambda i: (i, 0))
    return pl.pallas_call(
        _route_kernel,
        grid=(n_steps,),
        in_specs=[pl.BlockSpec((tt, d), lambda i: (i, 0)),
                  pl.BlockSpec(w_q.shape, lambda i: (0, 0)),
                  pl.BlockSpec(subkeys.shape, lambda i: (0, 0, 0)),
                  table, table],
        out_specs=[tok_major, tok_major, pl.BlockSpec((n_sel, tt), lambda i: (0, i)), tok_major,
                   pl.BlockSpec((d_exp, slab), lambda i: (0, i)), table],
        out_shape=[jax.ShapeDtypeStruct((t, n_sel), jnp.int32), jax.ShapeDtypeStruct((t, n_sel), jnp.int32),
                   jax.ShapeDtypeStruct((n_sel, t), jnp.int32), jax.ShapeDtypeStruct((t, n_sel), F32),
                   jax.ShapeDtypeStruct((d_exp, n_exp), BF16), jax.ShapeDtypeStruct(v_tab.shape, BF16)],
        compiler_params=_params("parallel"),
        name="peer_route",
    )(hb, w_q, subkeys, u_tab, v_tab)


_W_ROWS = PEER_NKEYS // 2


def _wbuild_kernel(a_ref, b_ref, bt_ref, g_ref, w_ref):
    n_tok, n_sel = a_ref.shape
    key_rows = lax.broadcasted_iota(jnp.int32, (PEER_NKEYS, n_sel), 0)
    key_cols = lax.broadcasted_iota(jnp.int32, (n_sel, PEER_NKEYS), 1)
    for t in range(n_tok):
        pa = jnp.where(key_rows == a_ref[t:t + 1, :], g_ref[t:t + 1, :], 0.0).astype(BF16)
        if t % 2:
            pb = jnp.where(key_cols == bt_ref[:, t:t + 1], 1.0, 0.0).astype(BF16)
            w = jnp.dot(pa, pb, preferred_element_type=F32)
        else:
            pb = jnp.where(key_rows == b_ref[t:t + 1, :], 1.0, 0.0).astype(BF16)
            w = lax.dot_general(pa, pb, _NT, preferred_element_type=F32)
        words = pltpu.pack_elementwise([w[:_W_ROWS, :], w[_W_ROWS:, :]], packed_dtype=BF16)
        w_ref[t // SUBLANES, pl.ds(t % SUBLANES, _W_ROWS, stride=SUBLANES), :] = (
            lax.bitcast_convert_type(words, jnp.uint32))


def _wbuild_call(a_tm, b_tm, b_sm, g_tm, tt):
    t, n_sel = a_tm.shape
    rows = _W_ROWS * SUBLANES
    tok_major = pl.BlockSpec((tt, n_sel), lambda i: (i, 0))
    return pl.pallas_call(
        _wbuild_kernel,
        grid=(t // tt,),
        in_specs=[tok_major, tok_major, pl.BlockSpec((n_sel, tt), lambda i: (0, i)), tok_major],
        out_specs=pl.BlockSpec((tt // SUBLANES, rows, PEER_NKEYS), lambda i: (i, 0, 0)),
        out_shape=jax.ShapeDtypeStruct((t // SUBLANES, rows, PEER_NKEYS), jnp.uint32),
        compiler_params=_params("parallel"),
        name="peer_gate_map",
    )(a_tm, b_tm, b_sm, g_tm)


def _gelu(x):
    return 0.5 * x * (1.0 + lax.erf(x * math.sqrt(0.5)))


def _peer_kernel(hb_ref, ul_ref, uh_ref, vl_ref, vh_ref, w_ref, h_ref, g_ref, b_ref, o_ref, acc_ref, gs_ref):
    j = pl.program_id(1)

    @pl.when(j == 0)
    def _():
        acc_ref[...] = jnp.zeros_like(acc_ref)

    tm = hb_ref.shape[0]
    half = vl_ref.shape[0]
    hb = hb_ref[...]
    for part, u_ref in enumerate((ul_ref, uh_ref)):
        act = _gelu(jnp.dot(hb, u_ref[...], preferred_element_type=F32))
        for a in range(half // PEER_NKEYS):
            word = w_ref[:, a, :, :].reshape(tm, PEER_NKEYS)
            w_a = pltpu.unpack_elementwise(word, index=part, packed_dtype=BF16, unpacked_dtype=F32)
            cols = slice(a * PEER_NKEYS, (a + 1) * PEER_NKEYS)
            gs_ref[part, :, cols] = (w_a * act[:, cols]).astype(BF16)
    acc_ref[...] += (jnp.dot(gs_ref[0], vl_ref[...], preferred_element_type=F32)
                     + jnp.dot(gs_ref[1], vh_ref[...], preferred_element_type=F32))

    @pl.when(j == pl.num_programs(1) - 1)
    def _():
        o_ref[...] = _layer_norm(DEEPNORM_ALPHA * h_ref[...] + acc_ref[...], g_ref[...], b_ref[...])


def _peer_call(hb, u_t, v_tab, w4, h, g, b, tm, tn):
    t, d = h.shape
    n_exp = v_tab.shape[0]
    half = tn // 2
    n_steps = n_exp // tn
    lo = pl.BlockSpec((half, d), lambda i, j: (j, 0))
    hi = pl.BlockSpec((half, d), lambda i, j: (j + n_steps, 0))
    tok = pl.BlockSpec((tm, d), lambda i, j: (i, 0))
    vec = pl.BlockSpec((1, d), lambda i, j: (0, 0))
    return pl.pallas_call(
        _peer_kernel,
        grid=(t // tm, n_steps),
        in_specs=[tok, pl.BlockSpec((d, half), lambda i, j: (0, j)),
                  pl.BlockSpec((d, half), lambda i, j: (0, j + n_steps)), lo, hi,
                  pl.BlockSpec((tm // SUBLANES, half // PEER_NKEYS, SUBLANES, PEER_NKEYS),
                               lambda i, j: (i, j, 0, 0)),
                  tok, vec, vec],
        out_specs=tok,
        out_shape=jax.ShapeDtypeStruct((t, d), F32),
        scratch_shapes=[pltpu.VMEM((tm, d), F32), pltpu.VMEM((2, tm, half), BF16)],
        compiler_params=_params("parallel", "arbitrary"),
        name="peer_dense",
    )(hb, u_t, u_t, v_tab, v_tab, w4, h, g, b)


def _layer(x2, batch, seq, w_in, w_gate_up, b_gate, gla_g, ret_g, w_out, ln1_g, ln1_b,
           w_q, subkeys, u_tab, v_tab, ln2_g, ln2_b):
    t, d = x2.shape
    tiles = _tiles(t, seq)
    row = lambda p: p.reshape(1, -1).astype(F32)

    w_up = jnp.pad(w_gate_up, ((0, LANES - GLA_GATE_RANK), (0, 0))).astype(BF16)
    w_t = jnp.swapaxes(w_in, 0, 1)
    bc, xb = _gate_call(x2, w_t, w_up, row(b_gate), tiles.gate_tokens)
    proj = _proj_call(xb, w_t, tiles.proj_tokens, tiles.proj_cols)

    half = RET_DK // 2
    inv = ROPE_BASE ** (-jnp.arange(half, dtype=F32) / half)
    ang = jnp.arange(seq).astype(F32)[:, None] * inv[None, :]
    mix = _mixer_call(proj, bc, jnp.cos(ang), jnp.sin(ang), row(gla_g), row(ret_g), batch, seq,
                      tiles.mixer_chunks)

    h, hb = _out_ln_call(mix, w_out.astype(BF16), x2, row(ln1_g), row(ln1_b), tiles.out_tokens)

    sk = subkeys.reshape(PEER_HEADS * 2, PEER_NKEYS, -1).astype(BF16)
    a_sel, b_sel, bt_sel, gates, u_b, v_b = _route_call(hb, w_q.astype(BF16), sk, u_tab, v_tab,
                                                        tiles.route_tokens)
    w3 = _wbuild_call(a_sel, b_sel, bt_sel, gates, tiles.map_tokens)
    w4 = w3.reshape(t // SUBLANES, _W_ROWS, SUBLANES, PEER_NKEYS)

    return _peer_call(hb, u_b, v_b, w4, h, row(ln2_g), row(ln2_b), tiles.dense_tokens, tiles.dense_experts)


def kernel(x, w_in, w_gla_gate_up, b_gla_gate, gla_norm_g, ret_norm_g, w_out, ln1_g, ln1_b,
           w_peer_q, peer_subkeys, peer_u, peer_v, ln2_g, ln2_b):
    batch, seq, d = x.shape
    h = x.reshape(batch * seq, d)
    for l in range(DEPTH):
        h = _layer(h, batch, seq, w_in[l], w_gla_gate_up[l], b_gla_gate[l], gla_norm_g[l], ret_norm_g[l],
                   w_out[l], ln1_g[l], ln1_b[l], w_peer_q[l], peer_subkeys[l], peer_u[l], peer_v[l],
                   ln2_g[l], ln2_b[l])
    return h.reshape(batch, seq, d)
```

```python
import functools
import math
from typing import NamedTuple

import jax
import jax.numpy as jnp
from jax import lax
from jax.experimental import pallas as pl
from jax.experimental.pallas import tpu as pltpu

F32 = jnp.float32
BF16 = jnp.bfloat16

DEPTH = 1
CHUNK = 64
GLA_HEADS = 4
GLA_DK = 128
GLA_DV = 256
GLA_GATE_RANK = 16
GLA_TAU = 16.0
RET_HEADS = 4
RET_DK = 256
RET_DV = 256
ROPE_BASE = 10000.0
PEER_HEADS = 8
PEER_NKEYS = 128
PEER_TOPK = 16
LN_EPS = 1e-5
DEEPNORM_ALPHA = (2 * DEPTH) ** 0.25

LANES = 128
SUBLANES = 8
VMEM_LIMIT = 56 * 1024 * 1024

_C_GQ = 0
_C_GK = _C_GQ + GLA_HEADS * GLA_DK
_C_GV = _C_GK + GLA_HEADS * GLA_DK
_C_GR = _C_GV + GLA_HEADS * GLA_DV
_C_RQ = _C_GR + GLA_HEADS * GLA_DV
_C_RK = _C_RQ + RET_HEADS * RET_DK
_C_RV = _C_RK + RET_HEADS * RET_DK
_C_RR = _C_RV + RET_HEADS * RET_DV
_C_END = _C_RR + RET_HEADS * RET_DV

_NT = (((1,), (1,)), ((), ()))
_TN = (((0,), (0,)), ((), ()))


class _Tiles(NamedTuple):
    gate_tokens: int
    proj_tokens: int
    proj_cols: int
    mixer_chunks: int
    out_tokens: int
    route_tokens: int
    map_tokens: int
    dense_tokens: int
    dense_experts: int


def _tiles(t, seq):
    return _Tiles(gate_tokens=min(256, t), proj_tokens=min(1024, t), proj_cols=1024,
                  mixer_chunks=min(8, seq // CHUNK), out_tokens=min(512, t), route_tokens=min(256, t),
                  map_tokens=min(256, t), dense_tokens=min(512, t), dense_experts=1024)


def _params(*sem):
    return pltpu.CompilerParams(dimension_semantics=sem, vmem_limit_bytes=VMEM_LIMIT)


def _gate_kernel(x_ref, wl_ref, wu_ref, b_ref, o_ref):
    tm = x_ref.shape[0]
    glr = lax.dot_general(x_ref[...].astype(BF16), wl_ref[...].astype(BF16), _NT, preferred_element_type=F32)
    lane = lax.broadcasted_iota(jnp.int32, glr.shape, 1)
    glr = jnp.where(lane < GLA_GATE_RANK, glr, 0.0)
    z = jnp.dot(glr.astype(BF16), wu_ref[...], preferred_element_type=F32) + b_ref[...]
    log_a = (jnp.minimum(z, 0.0) - jnp.log1p(jnp.exp(-jnp.abs(z)))) / GLA_TAU
    r = lax.broadcasted_iota(jnp.int32, (tm, tm), 0)
    c = lax.broadcasted_iota(jnp.int32, (tm, tm), 1)
    tri = jnp.where((r >= c) & ((r // CHUNK) == (c // CHUNK)), 1.0, 0.0).astype(F32)
    o_ref[...] = jnp.dot(tri, log_a, preferred_element_type=F32, precision=lax.Precision.HIGHEST)


def _gate_call(x, w_t, w_up, b_gate, tm):
    t, d = x.shape
    n = w_up.shape[1]
    return pl.pallas_call(
        _gate_kernel,
        grid=(t // tm,),
        in_specs=[pl.BlockSpec((tm, d), lambda i: (i, 0)),
                  pl.BlockSpec((LANES, d), lambda i: (_C_RQ // LANES, 0)),
                  pl.BlockSpec(w_up.shape, lambda i: (0, 0)),
                  pl.BlockSpec((1, n), lambda i: (0, 0))],
        out_specs=pl.BlockSpec((tm, n), lambda i: (i, 0)),
        out_shape=jax.ShapeDtypeStruct((t, n), F32),
        compiler_params=_params("parallel"),
        name="gla_gate",
    )(x, w_t, w_up, b_gate)


def _proj_kernel(x_ref, w_ref, o_ref):
    o_ref[...] = lax.dot_general(x_ref[...].astype(BF16), w_ref[...].astype(BF16), _NT,
                                 preferred_element_type=F32).astype(o_ref.dtype)


def _proj_call(x, w_t, tm, tn):
    m, k = x.shape
    n_gla_tiles = _C_RQ // tn
    assert n_gla_tiles * tn == _C_RQ
    w_row = lambda i, j: (pl.multiple_of(j * tn + jnp.where(j >= n_gla_tiles, GLA_GATE_RANK, 0), GLA_GATE_RANK), 0)
    return pl.pallas_call(
        _proj_kernel,
        grid=(m // tm, _C_END // tn),
        in_specs=[pl.BlockSpec((tm, k), lambda i, j: (i, 0)),
                  pl.BlockSpec((pl.Element(tn), pl.Element(k)), w_row)],
        out_specs=pl.BlockSpec((tm, tn), lambda i, j: (i, j)),
        out_shape=jax.ShapeDtypeStruct((m, _C_END), BF16),
        compiler_params=_params("parallel", "parallel"),
        name="proj_matmul",
    )(x, w_t)


def _silu(x):
    return x * jax.nn.sigmoid(x)


def _mixer_kernel(p_ref, bc_ref, cos_ref, sin_ref, gg_ref, rg_ref, o_ref, sg_ref, sr_ref, *, n_chunks):
    @pl.when(pl.program_id(1) == 0)
    def _():
        sg_ref[...] = jnp.zeros_like(sg_ref)
        sr_ref[...] = jnp.zeros_like(sr_ref)

    ri = lax.broadcasted_iota(jnp.int32, (CHUNK, CHUNK), 0)
    ci = lax.broadcasted_iota(jnp.int32, (CHUNK, CHUNK), 1)
    lower = ri >= ci
    dist = jnp.abs(ri - ci).astype(F32)
    row = lax.broadcasted_iota(jnp.int32, (CHUNK, RET_DK), 0).astype(F32)
    mid = CHUNK // 2

    def chunk_body(c, carry):
        rows = pl.ds(pl.multiple_of(c * CHUNK, CHUNK), CHUNK)

        def col(ref, base, h, width):
            return ref[rows, base + h * width: base + (h + 1) * width]

        for h in range(GLA_HEADS):
            q = col(p_ref, _C_GQ, h, GLA_DK).astype(F32) * (GLA_DK ** -0.5)
            k = col(p_ref, _C_GK, h, GLA_DK).astype(F32)
            v = col(p_ref, _C_GV, h, GLA_DV)
            gate = col(p_ref, _C_GR, h, GLA_DV).astype(F32)
            bc = col(bc_ref, 0, h, GLA_DK)
            b_last = bc[CHUNK - 1:CHUNK, :]
            b_mid = bc[mid:mid + 1, :]
            e_fwd = jnp.exp(bc - b_mid)
            e_bwd = jnp.exp(b_mid - bc)
            st = sg_ref[h]
            o = lax.dot_general((q * jnp.exp(bc)).astype(BF16), st.astype(BF16), _NT,
                                preferred_element_type=F32)
            a_lo = lax.dot_general((q * e_fwd).astype(BF16), (k * e_bwd).astype(BF16), _NT,
                                   preferred_element_type=F32)
            a_up = lax.dot_general((q * e_bwd).astype(BF16), (k * e_fwd).astype(BF16), _NT,
                                   preferred_element_type=F32)
            a = jnp.where(lower, a_lo, a_up)
            o = o + jnp.dot(a.astype(BF16), v, preferred_element_type=F32)
            u_t = lax.dot_general(v, (k * jnp.exp(b_last - bc)).astype(BF16), _TN,
                                  preferred_element_type=F32)
            sg_ref[h] = st * jnp.exp(b_last) + u_t
            y = o * lax.rsqrt(jnp.mean(o * o, axis=-1, keepdims=True) + LN_EPS)
            y = y * gg_ref[:, h * GLA_DV:(h + 1) * GLA_DV] * _silu(gate)
            o_ref[rows, h * GLA_DV:(h + 1) * GLA_DV] = y.astype(o_ref.dtype)

        cos = cos_ref[rows, :]
        sin = sin_ref[rows, :]
        half = RET_DK // 2

        def rotary(t):
            t1, t2 = t[:, :half], t[:, half:]
            return jnp.concatenate([t1 * cos - t2 * sin, t1 * sin + t2 * cos], axis=-1)

        for h in range(RET_HEADS):
            log_gamma = math.log(1.0 - 2.0 ** (-5.0 - h))
            q = rotary(col(p_ref, _C_RQ, h, RET_DK).astype(F32))
            k = rotary(col(p_ref, _C_RK, h, RET_DK).astype(F32)) * (RET_DK ** -0.5)
            v = col(p_ref, _C_RV, h, RET_DV)
            gate = col(p_ref, _C_RR, h, RET_DV).astype(F32)
            qb = q.astype(BF16)
            st = sr_ref[h]
            o = lax.dot_general(qb, st.astype(BF16), _NT, preferred_element_type=F32)
            o = o * jnp.exp(log_gamma * (row + 1.0))
            a = lax.dot_general(qb, k.astype(BF16), _NT, preferred_element_type=F32)
            a = a * jnp.exp(log_gamma * dist)
            o = o + jnp.dot(a.astype(BF16), v, preferred_element_type=F32)
            k_dec = jnp.exp(log_gamma * (CHUNK - 1.0 - row))
            u_t = lax.dot_general(v, (k * k_dec).astype(BF16), _TN, preferred_element_type=F32)
            sr_ref[h] = st * math.exp(log_gamma * CHUNK) + u_t
            oc = o - jnp.mean(o, axis=-1, keepdims=True)
            y = oc * lax.rsqrt(jnp.mean(oc * oc, axis=-1, keepdims=True) + LN_EPS)
            y = y * rg_ref[:, h * RET_DV:(h + 1) * RET_DV] * _silu(gate)
            c0 = GLA_HEADS * GLA_DV + h * RET_DV
            o_ref[rows, c0:c0 + RET_DV] = y.astype(o_ref.dtype)
        return carry

    lax.fori_loop(0, n_chunks, chunk_body, 0, unroll=2)


def _mixer_call(proj, bc, cos, sin, gla_g, ret_g, batch, seq, n_chunks):
    t = proj.shape[0]
    tb = n_chunks * CHUNK
    groups = seq // tb
    width = GLA_HEADS * GLA_DV + RET_HEADS * RET_DV
    tok = lambda b, g: (b * groups + g, 0)
    return pl.pallas_call(
        functools.partial(_mixer_kernel, n_chunks=n_chunks),
        grid=(batch, groups),
        in_specs=[pl.BlockSpec((tb, proj.shape[1]), tok),
                  pl.BlockSpec((tb, bc.shape[1]), tok),
                  pl.BlockSpec((tb, cos.shape[1]), lambda b, g: (g, 0)),
                  pl.BlockSpec((tb, sin.shape[1]), lambda b, g: (g, 0)),
                  pl.BlockSpec(gla_g.shape, lambda b, g: (0, 0)),
                  pl.BlockSpec(ret_g.shape, lambda b, g: (0, 0))],
        out_specs=pl.BlockSpec((tb, width), tok),
        out_shape=jax.ShapeDtypeStruct((t, width), BF16),
        scratch_shapes=[pltpu.VMEM((GLA_HEADS, GLA_DV, GLA_DK), F32),
                        pltpu.VMEM((RET_HEADS, RET_DV, RET_DK), F32)],
        compiler_params=_params("parallel", "arbitrary"),
        name="mixer",
    )(proj, bc, cos, sin, gla_g, ret_g)


def _layer_norm(y, g, b):
    mu = jnp.mean(y, axis=-1, keepdims=True)
    yc = y - mu
    var = jnp.mean(yc * yc, axis=-1, keepdims=True)
    return yc * lax.rsqrt(var + LN_EPS) * g + b


def _out_ln_kernel(m_ref, w_ref, x_ref, g_ref, b_ref, h_ref, hb_ref):
    half = m_ref.shape[0] // 2
    for r in range(2):
        rows = slice(r * half, (r + 1) * half)
        mix = jnp.dot(m_ref[rows, :], w_ref[...], preferred_element_type=F32)
        h = _layer_norm(DEEPNORM_ALPHA * x_ref[rows, :] + mix, g_ref[...], b_ref[...])
        h_ref[rows, :] = h
        hb_ref[rows, :] = h.astype(BF16)


def _out_ln_call(mix, w_out, x, g, b, tm):
    t, d = x.shape
    return pl.pallas_call(
        _out_ln_kernel,
        grid=(t // tm,),
        in_specs=[pl.BlockSpec((tm, mix.shape[1]), lambda i: (i, 0)),
                  pl.BlockSpec(w_out.shape, lambda i: (0, 0)),
                  pl.BlockSpec((tm, d), lambda i: (i, 0)),
                  pl.BlockSpec((1, d), lambda i: (0, 0)),
                  pl.BlockSpec((1, d), lambda i: (0, 0))],
        out_specs=[pl.BlockSpec((tm, d), lambda i: (i, 0)),
                   pl.BlockSpec((tm, d), lambda i: (i, 0))],
        out_shape=[jax.ShapeDtypeStruct((t, d), F32), jax.ShapeDtypeStruct((t, d), BF16)],
        compiler_params=_params("parallel"),
        name="out_proj_ln",
    )(mix, w_out, x, g, b)


def _topk_paired(s, key, k):
    h = s.shape[0] // 2
    if key is None:
        key_lo = lax.broadcasted_iota(jnp.int32, (h, s.shape[1]), 0).astype(F32)
        key_hi = key_lo + float(h)
    else:
        key_lo, key_hi = key[:h, :], key[h:, :]
    swap = s[h:, :] > s[:h, :]
    front = jnp.where(swap, s[h:, :], s[:h, :])
    back = jnp.where(swap, s[:h, :], s[h:, :])
    front_key = jnp.where(swap, key_hi, key_lo)
    back_key = jnp.where(swap, key_lo, key_hi)
    vals, keys = [], []
    for _ in range(k):
        m = jnp.max(front, axis=0, keepdims=True)
        pick = jnp.min(jnp.where(front == m, front_key, jnp.inf), axis=0, keepdims=True)
        hit = front_key == pick
        vals.append(m)
        keys.append(pick)
        front = jnp.where(hit, back, front)
        front_key = jnp.where(hit, back_key, front_key)
        back = jnp.where(hit, -jnp.inf, back)
    return jnp.concatenate(vals, axis=0), jnp.concatenate(keys, axis=0)


def _route_kernel(h_ref, wq_ref, sk_ref, u_ref, v_ref, a_ref, b_ref, bt_ref, g_ref, ub_ref, vb_ref):
    ub_ref[...] = u_ref[...].T.astype(BF16)
    vb_ref[...] = v_ref[...].astype(BF16)
    kk = PEER_TOPK
    q = jnp.dot(h_ref[...], wq_ref[...], preferred_element_type=F32).astype(BF16)
    n_tok = q.shape[0]
    n_exp = PEER_NKEYS * PEER_NKEYS
    n_cand = sum(kk // (r1 + 1) for r1 in range(kk))
    pad = (-n_cand) % (2 * SUBLANES)
    cand_row = lax.broadcasted_iota(jnp.int32, (n_cand + pad, n_tok), 0).astype(F32) * float(n_exp)
    experts, gates = [], []
    for h in range(PEER_HEADS):
        vals, idxs = [], []
        for p in range(2):
            j = 2 * h + p
            s_t = lax.dot_general(sk_ref[j], q[:, j * LANES:(j + 1) * LANES], _NT,
                                  preferred_element_type=F32)
            v_p, i_p = _topk_paired(s_t, None, kk)
            vals.append(v_p)
            idxs.append(i_p)
        cand, expert = [], []
        for r1 in range(kk):
            n2 = kk // (r1 + 1)
            cand.append(vals[0][r1:r1 + 1, :] + vals[1][:n2, :])
            expert.append(idxs[0][r1:r1 + 1, :] * float(PEER_NKEYS) + idxs[1][:n2, :])
        cand.append(jnp.full((pad, n_tok), -jnp.inf, F32))
        expert.append(jnp.zeros((pad, n_tok), F32))
        top_s, top_key = _topk_paired(jnp.concatenate(cand, axis=0),
                                      cand_row + jnp.concatenate(expert, axis=0), kk)
        e = jnp.exp(top_s - top_s[0:1, :])
        experts.append(top_key)
        gates.append(e / jnp.sum(e, axis=0, keepdims=True))
    top_e = jnp.concatenate(experts, axis=0).astype(jnp.int32) & (n_exp - 1)
    a_ref[...] = (top_e >> (PEER_NKEYS.bit_length() - 1)).T
    second = top_e & (PEER_NKEYS - 1)
    b_ref[...] = second.T
    bt_ref[...] = second
    g_ref[...] = jnp.concatenate(gates, axis=0).T


def _route_call(hb, w_q, subkeys, u_tab, v_tab, tt):
    t, d = hb.shape
    n_sel = PEER_HEADS * PEER_TOPK
    n_steps = t // tt
    n_exp, d_exp = u_tab.shape
    slab = n_exp // n_steps
    assert slab * n_steps == n_exp and slab % (2 * SUBLANES) == 0
    tok_major = pl.BlockSpec((tt, n_sel), lambda i: (i, 0))
    table = pl.BlockSpec((slab, d_exp), lambda i: (i, 0))
    return pl.pallas_call(
        _route_kernel,
        grid=(n_steps,),
        in_specs=[pl.BlockSpec((tt, d), lambda i: (i, 0)),
                  pl.BlockSpec(w_q.shape, lambda i: (0, 0)),
                  pl.BlockSpec(subkeys.shape, lambda i: (0, 0, 0)),
                  table, table],
        out_specs=[tok_major, tok_major, pl.BlockSpec((n_sel, tt), lambda i: (0, i)), tok_major,
                   pl.BlockSpec((d_exp, slab), lambda i: (0, i)), table],
        out_shape=[jax.ShapeDtypeStruct((t, n_sel), jnp.int32), jax.ShapeDtypeStruct((t, n_sel), jnp.int32),
                   jax.ShapeDtypeStruct((n_sel, t), jnp.int32), jax.ShapeDtypeStruct((t, n_sel), F32),
                   jax.ShapeDtypeStruct((d_exp, n_exp), BF16), jax.ShapeDtypeStruct(v_tab.shape, BF16)],
        compiler_params=_params("parallel"),
        name="peer_route",
    )(hb, w_q, subkeys, u_tab, v_tab)


_W_ROWS = PEER_NKEYS // 2


def _wbuild_kernel(a_ref, b_ref, bt_ref, g_ref, w_ref):
    n_tok, n_sel = a_ref.shape
    key_rows = lax.broadcasted_iota(jnp.int32, (PEER_NKEYS, n_sel), 0)
    key_cols = lax.broadcasted_iota(jnp.int32, (n_sel, PEER_NKEYS), 1)
    for t in range(n_tok):
        pa = jnp.where(key_rows == a_ref[t:t + 1, :], g_ref[t:t + 1, :], 0.0).astype(BF16)
        if t % 2:
            pb = jnp.where(key_cols == bt_ref[:, t:t + 1], 1.0, 0.0).astype(BF16)
            w = jnp.dot(pa, pb, preferred_element_type=F32)
        else:
            pb = jnp.where(key_rows == b_ref[t:t + 1, :], 1.0, 0.0).astype(BF16)
            w = lax.dot_general(pa, pb, _NT, preferred_element_type=F32)
        words = pltpu.pack_elementwise([w[:_W_ROWS, :], w[_W_ROWS:, :]], packed_dtype=BF16)
        w_ref[t // SUBLANES, pl.ds(t % SUBLANES, _W_ROWS, stride=SUBLANES), :] = (
            lax.bitcast_convert_type(words, jnp.uint32))


def _wbuild_call(a_tm, b_tm, b_sm, g_tm, tt):
    t, n_sel = a_tm.shape
    rows = _W_ROWS * SUBLANES
    tok_major = pl.BlockSpec((tt, n_sel), lambda i: (i, 0))
    return pl.pallas_call(
        _wbuild_kernel,
        grid=(t // tt,),
        in_specs=[tok_major, tok_major, pl.BlockSpec((n_sel, tt), lambda i: (0, i)), tok_major],
        out_specs=pl.BlockSpec((tt // SUBLANES, rows, PEER_NKEYS), lambda i: (i, 0, 0)),
        out_shape=jax.ShapeDtypeStruct((t // SUBLANES, rows, PEER_NKEYS), jnp.uint32),
        compiler_params=_params("parallel"),
        name="peer_gate_map",
    )(a_tm, b_tm, b_sm, g_tm)


def _gelu(x):
    return 0.5 * x * (1.0 + lax.erf(x * math.sqrt(0.5)))


def _peer_kernel(hb_ref, ul_ref, uh_ref, vl_ref, vh_ref, w_ref, h_ref, g_ref, b_ref, o_ref, acc_ref, gs_ref):
    j = pl.program_id(1)

    @pl.when(j == 0)
    def _():
        acc_ref[...] = jnp.zeros_like(acc_ref)

    tm = hb_ref.shape[0]
    half = vl_ref.shape[0]
    hb = hb_ref[...]
    for part, u_ref in enumerate((ul_ref, uh_ref)):
        act = _gelu(jnp.dot(hb, u_ref[...], preferred_element_type=F32))
        for a in range(half // PEER_NKEYS):
            word = w_ref[:, a, :, :].reshape(tm, PEER_NKEYS)
            w_a = pltpu.unpack_elementwise(word, index=part, packed_dtype=BF16, unpacked_dtype=F32)
            cols = slice(a * PEER_NKEYS, (a + 1) * PEER_NKEYS)
            gs_ref[part, :, cols] = (w_a * act[:, cols]).astype(BF16)
    acc_ref[...] += (jnp.dot(gs_ref[0], vl_ref[...], preferred_element_type=F32)
                     + jnp.dot(gs_ref[1], vh_ref[...], preferred_element_type=F32))

    @pl.when(j == pl.num_programs(1) - 1)
    def _():
        o_ref[...] = _layer_norm(DEEPNORM_ALPHA * h_ref[...] + acc_ref[...], g_ref[...], b_ref[...])


def _peer_call(hb, u_t, v_tab, w4, h, g, b, tm, tn):
    t, d = h.shape
    n_exp = v_tab.shape[0]
    half = tn // 2
    n_steps = n_exp // tn
    lo = pl.BlockSpec((half, d), lambda i, j: (j, 0))
    hi = pl.BlockSpec((half, d), lambda i, j: (j + n_steps, 0))
    tok = pl.BlockSpec((tm, d), lambda i, j: (i, 0))
    vec = pl.BlockSpec((1, d), lambda i, j: (0, 0))
    return pl.pallas_call(
        _peer_kernel,
        grid=(t // tm, n_steps),
        in_specs=[tok, pl.BlockSpec((d, half), lambda i, j: (0, j)),
                  pl.BlockSpec((d, half), lambda i, j: (0, j + n_steps)), lo, hi,
                  pl.BlockSpec((tm // SUBLANES, half // PEER_NKEYS, SUBLANES, PEER_NKEYS),
                               lambda i, j: (i, j, 0, 0)),
                  tok, vec, vec],
        out_specs=tok,
        out_shape=jax.ShapeDtypeStruct((t, d), F32),
        scratch_shapes=[pltpu.VMEM((tm, d), F32), pltpu.VMEM((2, tm, half), BF16)],
        compiler_params=_params("parallel", "arbitrary"),
        name="peer_dense",
    )(hb, u_t, u_t, v_tab, v_tab, w4, h, g, b)


def _layer(x2, batch, seq, w_in, w_gate_up, b_gate, gla_g, ret_g, w_out, ln1_g, ln1_b,
           w_q, subkeys, u_tab, v_tab, ln2_g, ln2_b):
    t, d = x2.shape
    tiles = _tiles(t, seq)
    row = lambda p: p.reshape(1, -1).astype(F32)

    w_up = jnp.pad(w_gate_up, ((0, LANES - GLA_GATE_RANK), (0, 0))).astype(BF16)
    w_t = jnp.swapaxes(w_in, 0, 1)
    bc = _gate_call(x2, w_t, w_up, row(b_gate), tiles.gate_tokens)
    proj = _proj_call(x2, w_t, tiles.proj_tokens, tiles.proj_cols)

    half = RET_DK // 2
    inv = ROPE_BASE ** (-jnp.arange(half, dtype=F32) / half)
    ang = jnp.arange(seq).astype(F32)[:, None] * inv[None, :]
    mix = _mixer_call(proj, bc, jnp.cos(ang), jnp.sin(ang), row(gla_g), row(ret_g), batch, seq,
                      tiles.mixer_chunks)

    h, hb = _out_ln_call(mix, w_out.astype(BF16), x2, row(ln1_g), row(ln1_b), tiles.out_tokens)

    sk = subkeys.reshape(PEER_HEADS * 2, PEER_NKEYS, -1).astype(BF16)
    a_sel, b_sel, bt_sel, gates, u_b, v_b = _route_call(hb, w_q.astype(BF16), sk, u_tab, v_tab,
                                                        tiles.route_tokens)
    w3 = _wbuild_call(a_sel, b_sel, bt_sel, gates, tiles.map_tokens)
    w4 = w3.reshape(t // SUBLANES, _W_ROWS, SUBLANES, PEER_NKEYS)

    return _peer_call(hb, u_b, v_b, w4, h, row(ln2_g), row(ln2_b), tiles.dense_tokens, tiles.dense_experts)


def kernel(x, w_in, w_gla_gate_up, b_gla_gate, gla_norm_g, ret_norm_g, w_out, ln1_g, ln1_b,
           w_peer_q, peer_subkeys, peer_u, peer_v, ln2_g, ln2_b):
    batch, seq, d = x.shape
    h = x.reshape(batch * seq, d)
    for l in range(DEPTH):
        h = _layer(h, batch, seq, w_in[l], w_gla_gate_up[l], b_gla_gate[l], gla_norm_g[l], ret_norm_g[l],
                   w_out[l], ln1_g[l], ln1_b[l], w_peer_q[l], peer_subkeys[l], peer_u[l], peer_v[l],
                   ln2_g[l], ln2_b[l])
    return h.reshape(batch, seq, d)
```

```python
import functools
import math
from typing import NamedTuple

import jax
import jax.numpy as jnp
from jax import lax
from jax.experimental import pallas as pl
from jax.experimental.pallas import tpu as pltpu

F32 = jnp.float32
BF16 = jnp.bfloat16

DEPTH = 1
CHUNK = 64
GLA_HEADS = 4
GLA_DK = 128
GLA_DV = 256
GLA_GATE_RANK = 16
GLA_TAU = 16.0
RET_HEADS = 4
RET_DK = 256
RET_DV = 256
ROPE_BASE = 10000.0
PEER_HEADS = 8
PEER_NKEYS = 128
PEER_TOPK = 16
LN_EPS = 1e-5
DEEPNORM_ALPHA = (2 * DEPTH) ** 0.25

LANES = 128
SUBLANES = 8
VMEM_LIMIT = 56 * 1024 * 1024

_C_GQ = 0
_C_GK = _C_GQ + GLA_HEADS * GLA_DK
_C_GV = _C_GK + GLA_HEADS * GLA_DK
_C_GR = _C_GV + GLA_HEADS * GLA_DV
_C_RQ = _C_GR + GLA_HEADS * GLA_DV
_C_RK = _C_RQ + RET_HEADS * RET_DK
_C_RV = _C_RK + RET_HEADS * RET_DK
_C_RR = _C_RV + RET_HEADS * RET_DV
_C_END = _C_RR + RET_HEADS * RET_DV

_NT = (((1,), (1,)), ((), ()))
_TN = (((0,), (0,)), ((), ()))


class _Tiles(NamedTuple):
    gate_tokens: int
    proj_tokens: int
    proj_cols: int
    mixer_chunks: int
    out_tokens: int
    route_tokens: int
    map_tokens: int
    dense_tokens: int
    dense_experts: int


def _tiles(t, seq):
    return _Tiles(gate_tokens=min(1024, t), proj_tokens=min(1024, t), proj_cols=1024,
                  mixer_chunks=min(8, seq // CHUNK), out_tokens=min(512, t), route_tokens=min(256, t),
                  map_tokens=min(256, t), dense_tokens=min(512, t), dense_experts=1024)


def _params(*sem):
    return pltpu.CompilerParams(dimension_semantics=sem, vmem_limit_bytes=VMEM_LIMIT)


_GATE_ROWS = 256


def _gate_kernel(x_ref, wl_ref, wu_ref, b_ref, o_ref, xb_ref, *, n_split):
    tm = x_ref.shape[0] // n_split
    r = lax.broadcasted_iota(jnp.int32, (tm, tm), 0)
    c = lax.broadcasted_iota(jnp.int32, (tm, tm), 1)
    tri = jnp.where((r >= c) & ((r // CHUNK) == (c // CHUNK)), 1.0, 0.0).astype(F32)
    wl = wl_ref[...].astype(BF16)
    for s in range(n_split):
        rows = slice(s * tm, (s + 1) * tm)
        xb = x_ref[rows, :].astype(BF16)
        xb_ref[rows, :] = xb
        glr = lax.dot_general(xb, wl, _NT, preferred_element_type=F32)
        lane = lax.broadcasted_iota(jnp.int32, glr.shape, 1)
        glr = jnp.where(lane < GLA_GATE_RANK, glr, 0.0)
        z = jnp.dot(glr.astype(BF16), wu_ref[...], preferred_element_type=F32) + b_ref[...]
        log_a = (jnp.minimum(z, 0.0) - jnp.log1p(jnp.exp(-jnp.abs(z)))) / GLA_TAU
        o_ref[rows, :] = jnp.dot(tri, log_a, preferred_element_type=F32, precision=lax.Precision.HIGHEST)


def _gate_call(x, w_t, w_up, b_gate, tm):
    t, d = x.shape
    n = w_up.shape[1]
    return pl.pallas_call(
        functools.partial(_gate_kernel, n_split=max(tm // _GATE_ROWS, 1)),
        grid=(t // tm,),
        in_specs=[pl.BlockSpec((tm, d), lambda i: (i, 0)),
                  pl.BlockSpec((LANES, d), lambda i: (_C_RQ // LANES, 0)),
                  pl.BlockSpec(w_up.shape, lambda i: (0, 0)),
                  pl.BlockSpec((1, n), lambda i: (0, 0))],
        out_specs=[pl.BlockSpec((tm, n), lambda i: (i, 0)),
                   pl.BlockSpec((tm, d), lambda i: (i, 0))],
        out_shape=[jax.ShapeDtypeStruct((t, n), F32), jax.ShapeDtypeStruct((t, d), BF16)],
        compiler_params=_params("parallel"),
        name="gla_gate",
    )(x, w_t, w_up, b_gate)


def _proj_kernel(x_ref, w_ref, o_ref):
    o_ref[...] = lax.dot_general(x_ref[...], w_ref[...].astype(BF16), _NT,
                                 preferred_element_type=F32).astype(o_ref.dtype)


def _proj_call(xb, w_t, tm, tn):
    m, k = xb.shape
    n_gla_tiles = _C_RQ // tn
    assert n_gla_tiles * tn == _C_RQ
    w_row = lambda i, j: (pl.multiple_of(j * tn + jnp.where(j >= n_gla_tiles, GLA_GATE_RANK, 0), GLA_GATE_RANK), 0)
    return pl.pallas_call(
        _proj_kernel,
        grid=(m // tm, _C_END // tn),
        in_specs=[pl.BlockSpec((tm, k), lambda i, j: (i, 0)),
                  pl.BlockSpec((pl.Element(tn), pl.Element(k)), w_row)],
        out_specs=pl.BlockSpec((tm, tn), lambda i, j: (i, j)),
        out_shape=jax.ShapeDtypeStruct((m, _C_END), BF16),
        compiler_params=_params("parallel", "parallel"),
        name="proj_matmul",
    )(xb, w_t)


def _silu(x):
    return x * jax.nn.sigmoid(x)


def _mixer_kernel(p_ref, bc_ref, cos_ref, sin_ref, gg_ref, rg_ref, o_ref, sg_ref, sr_ref, *, n_chunks):
    @pl.when(pl.program_id(1) == 0)
    def _():
        sg_ref[...] = jnp.zeros_like(sg_ref)
        sr_ref[...] = jnp.zeros_like(sr_ref)

    ri = lax.broadcasted_iota(jnp.int32, (CHUNK, CHUNK), 0)
    ci = lax.broadcasted_iota(jnp.int32, (CHUNK, CHUNK), 1)
    lower = ri >= ci
    dist = jnp.abs(ri - ci).astype(F32)
    row = lax.broadcasted_iota(jnp.int32, (CHUNK, RET_DK), 0).astype(F32)
    mid = CHUNK // 2

    def chunk_body(c, carry):
        rows = pl.ds(pl.multiple_of(c * CHUNK, CHUNK), CHUNK)

        def col(ref, base, h, width):
            return ref[rows, base + h * width: base + (h + 1) * width]

        for h in range(GLA_HEADS):
            q = col(p_ref, _C_GQ, h, GLA_DK).astype(F32) * (GLA_DK ** -0.5)
            k = col(p_ref, _C_GK, h, GLA_DK).astype(F32)
            v = col(p_ref, _C_GV, h, GLA_DV)
            gate = col(p_ref, _C_GR, h, GLA_DV).astype(F32)
            bc = col(bc_ref, 0, h, GLA_DK)
            b_last = bc[CHUNK - 1:CHUNK, :]
            b_mid = bc[mid:mid + 1, :]
            e_fwd = jnp.exp(bc - b_mid)
            e_bwd = jnp.exp(b_mid - bc)
            st = sg_ref[h]
            o = lax.dot_general((q * jnp.exp(bc)).astype(BF16), st.astype(BF16), _NT,
                                preferred_element_type=F32)
            a_lo = lax.dot_general((q * e_fwd).astype(BF16), (k * e_bwd).astype(BF16), _NT,
                                   preferred_element_type=F32)
            a_up = lax.dot_general((q * e_bwd).astype(BF16), (k * e_fwd).astype(BF16), _NT,
                                   preferred_element_type=F32)
            a = jnp.where(lower, a_lo, a_up)
            o = o + jnp.dot(a.astype(BF16), v, preferred_element_type=F32)
            u_t = lax.dot_general(v, (k * jnp.exp(b_last - bc)).astype(BF16), _TN,
                                  preferred_element_type=F32)
            sg_ref[h] = st * jnp.exp(b_last) + u_t
            y = o * lax.rsqrt(jnp.mean(o * o, axis=-1, keepdims=True) + LN_EPS)
            y = y * gg_ref[:, h * GLA_DV:(h + 1) * GLA_DV] * _silu(gate)
            o_ref[rows, h * GLA_DV:(h + 1) * GLA_DV] = y.astype(o_ref.dtype)

        cos = cos_ref[rows, :]
        sin = sin_ref[rows, :]
        half = RET_DK // 2

        def rotary(t):
            t1, t2 = t[:, :half], t[:, half:]
            return jnp.concatenate([t1 * cos - t2 * sin, t1 * sin + t2 * cos], axis=-1)

        for h in range(RET_HEADS):
            log_gamma = math.log(1.0 - 2.0 ** (-5.0 - h))
            q = rotary(col(p_ref, _C_RQ, h, RET_DK).astype(F32))
            k = rotary(col(p_ref, _C_RK, h, RET_DK).astype(F32)) * (RET_DK ** -0.5)
            v = col(p_ref, _C_RV, h, RET_DV)
            gate = col(p_ref, _C_RR, h, RET_DV).astype(F32)
            qb = q.astype(BF16)
            st = sr_ref[h]
            o = lax.dot_general(qb, st.astype(BF16), _NT, preferred_element_type=F32)
            o = o * jnp.exp(log_gamma * (row + 1.0))
            a = lax.dot_general(qb, k.astype(BF16), _NT, preferred_element_type=F32)
            a = a * jnp.exp(log_gamma * dist)
            o = o + jnp.dot(a.astype(BF16), v, preferred_element_type=F32)
            k_dec = jnp.exp(log_gamma * (CHUNK - 1.0 - row))
            u_t = lax.dot_general(v, (k * k_dec).astype(BF16), _TN, preferred_element_type=F32)
            sr_ref[h] = st * math.exp(log_gamma * CHUNK) + u_t
            oc = o - jnp.mean(o, axis=-1, keepdims=True)
            y = oc * lax.rsqrt(jnp.mean(oc * oc, axis=-1, keepdims=True) + LN_EPS)
            y = y * rg_ref[:, h * RET_DV:(h + 1) * RET_DV] * _silu(gate)
            c0 = GLA_HEADS * GLA_DV + h * RET_DV
            o_ref[rows, c0:c0 + RET_DV] = y.astype(o_ref.dtype)
        return carry

    lax.fori_loop(0, n_chunks, chunk_body, 0, unroll=4)


def _mixer_call(proj, bc, cos, sin, gla_g, ret_g, batch, seq, n_chunks):
    t = proj.shape[0]
    tb = n_chunks * CHUNK
    groups = seq // tb
    width = GLA_HEADS * GLA_DV + RET_HEADS * RET_DV
    tok = lambda b, g: (b * groups + g, 0)
    return pl.pallas_call(
        functools.partial(_mixer_kernel, n_chunks=n_chunks),
        grid=(batch, groups),
        in_specs=[pl.BlockSpec((tb, proj.shape[1]), tok),
                  pl.BlockSpec((tb, bc.shape[1]), tok),
                  pl.BlockSpec((tb, cos.shape[1]), lambda b, g: (g, 0)),
                  pl.BlockSpec((tb, sin.shape[1]), lambda b, g: (g, 0)),
                  pl.BlockSpec(gla_g.shape, lambda b, g: (0, 0)),
                  pl.BlockSpec(ret_g.shape, lambda b, g: (0, 0))],
        out_specs=pl.BlockSpec((tb, width), tok),
        out_shape=jax.ShapeDtypeStruct((t, width), BF16),
        scratch_shapes=[pltpu.VMEM((GLA_HEADS, GLA_DV, GLA_DK), F32),
                        pltpu.VMEM((RET_HEADS, RET_DV, RET_DK), F32)],
        compiler_params=_params("parallel", "arbitrary"),
        name="mixer",
    )(proj, bc, cos, sin, gla_g, ret_g)


def _layer_norm(y, g, b):
    mu = jnp.mean(y, axis=-1, keepdims=True)
    yc = y - mu
    var = jnp.mean(yc * yc, axis=-1, keepdims=True)
    return yc * lax.rsqrt(var + LN_EPS) * g + b


def _out_ln_kernel(m_ref, w_ref, x_ref, g_ref, b_ref, h_ref, hb_ref):
    half = m_ref.shape[0] // 2
    for r in range(2):
        rows = slice(r * half, (r + 1) * half)
        mix = jnp.dot(m_ref[rows, :], w_ref[...], preferred_element_type=F32)
        h = _layer_norm(DEEPNORM_ALPHA * x_ref[rows, :] + mix, g_ref[...], b_ref[...])
        h_ref[rows, :] = h
        hb_ref[rows, :] = h.astype(BF16)


def _out_ln_call(mix, w_out, x, g, b, tm):
    t, d = x.shape
    return pl.pallas_call(
        _out_ln_kernel,
        grid=(t // tm,),
        in_specs=[pl.BlockSpec((tm, mix.shape[1]), lambda i: (i, 0)),
                  pl.BlockSpec(w_out.shape, lambda i: (0, 0)),
                  pl.BlockSpec((tm, d), lambda i: (i, 0)),
                  pl.BlockSpec((1, d), lambda i: (0, 0)),
                  pl.BlockSpec((1, d), lambda i: (0, 0))],
        out_specs=[pl.BlockSpec((tm, d), lambda i: (i, 0)),
                   pl.BlockSpec((tm, d), lambda i: (i, 0))],
        out_shape=[jax.ShapeDtypeStruct((t, d), F32), jax.ShapeDtypeStruct((t, d), BF16)],
        compiler_params=_params("parallel"),
        name="out_proj_ln",
    )(mix, w_out, x, g, b)


def _topk_paired(s, key, k):
    h = s.shape[0] // 2
    if key is None:
        key_lo = lax.broadcasted_iota(jnp.int32, (h, s.shape[1]), 0).astype(F32)
        key_hi = key_lo + float(h)
    else:
        key_lo, key_hi = key[:h, :], key[h:, :]
    swap = s[h:, :] > s[:h, :]
    front = jnp.where(swap, s[h:, :], s[:h, :])
    back = jnp.where(swap, s[:h, :], s[h:, :])
    front_key = jnp.where(swap, key_hi, key_lo)
    back_key = jnp.where(swap, key_lo, key_hi)
    vals, keys = [], []
    for _ in range(k):
        m = jnp.max(front, axis=0, keepdims=True)
        pick = jnp.min(jnp.where(front == m, front_key, jnp.inf), axis=0, keepdims=True)
        hit = front_key == pick
        vals.append(m)
        keys.append(pick)
        front = jnp.where(hit, back, front)
        front_key = jnp.where(hit, back_key, front_key)
        back = jnp.where(hit, -jnp.inf, back)
    return jnp.concatenate(vals, axis=0), jnp.concatenate(keys, axis=0)


def _route_kernel(h_ref, wq_ref, sk_ref, u_ref, v_ref, a_ref, b_ref, bt_ref, g_ref, ub_ref, vb_ref):
    ub_ref[...] = u_ref[...].T.astype(BF16)
    vb_ref[...] = v_ref[...].astype(BF16)
    kk = PEER_TOPK
    q = jnp.dot(h_ref[...], wq_ref[...], preferred_element_type=F32).astype(BF16)
    n_tok = q.shape[0]
    n_exp = PEER_NKEYS * PEER_NKEYS
    n_cand = sum(kk // (r1 + 1) for r1 in range(kk))
    pad = (-n_cand) % (2 * SUBLANES)
    cand_row = lax.broadcasted_iota(jnp.int32, (n_cand + pad, n_tok), 0).astype(F32) * float(n_exp)
    experts, gates = [], []
    for h in range(PEER_HEADS):
        vals, idxs = [], []
        for p in range(2):
            j = 2 * h + p
            s_t = lax.dot_general(sk_ref[j], q[:, j * LANES:(j + 1) * LANES], _NT,
                                  preferred_element_type=F32)
            v_p, i_p = _topk_paired(s_t, None, kk)
            vals.append(v_p)
            idxs.append(i_p)
        cand, expert = [], []
        for r1 in range(kk):
            n2 = kk // (r1 + 1)
            cand.append(vals[0][r1:r1 + 1, :] + vals[1][:n2, :])
            expert.append(idxs[0][r1:r1 + 1, :] * float(PEER_NKEYS) + idxs[1][:n2, :])
        cand.append(jnp.full((pad, n_tok), -jnp.inf, F32))
        expert.append(jnp.zeros((pad, n_tok), F32))
        top_s, top_key = _topk_paired(jnp.concatenate(cand, axis=0),
                                      cand_row + jnp.concatenate(expert, axis=0), kk)
        e = jnp.exp(top_s - top_s[0:1, :])
        experts.append(top_key)
        gates.append(e / jnp.sum(e, axis=0, keepdims=True))
    top_e = jnp.concatenate(experts, axis=0).astype(jnp.int32) & (n_exp - 1)
    a_ref[...] = (top_e >> (PEER_NKEYS.bit_length() - 1)).T
    second = top_e & (PEER_NKEYS - 1)
    b_ref[...] = second.T
    bt_ref[...] = second
    g_ref[...] = jnp.concatenate(gates, axis=0).T


def _route_call(hb, w_q, subkeys, u_tab, v_tab, tt):
    t, d = hb.shape
    n_sel = PEER_HEADS * PEER_TOPK
    n_steps = t // tt
    n_exp, d_exp = u_tab.shape
    slab = n_exp // n_steps
    assert slab * n_steps == n_exp and slab % (2 * SUBLANES) == 0
    tok_major = pl.BlockSpec((tt, n_sel), lambda i: (i, 0))
    table = pl.BlockSpec((slab, d_exp), lambda i: (i, 0))
    return pl.pallas_call(
        _route_kernel,
        grid=(n_steps,),
        in_specs=[pl.BlockSpec((tt, d), lambda i: (i, 0)),
                  pl.BlockSpec(w_q.shape, lambda i: (0, 0)),
                  pl.BlockSpec(subkeys.shape, lambda i: (0, 0, 0)),
                  table, table],
        out_specs=[tok_major, tok_major, pl.BlockSpec((n_sel, tt), lambda i: (0, i)), tok_major,
                   pl.BlockSpec((d_exp, slab), lambda i: (0, i)), table],
        out_shape=[jax.ShapeDtypeStruct((t, n_sel), jnp.int32), jax.ShapeDtypeStruct((t, n_sel), jnp.int32),
                   jax.ShapeDtypeStruct((n_sel, t), jnp.int32), jax.ShapeDtypeStruct((t, n_sel), F32),
                   jax.ShapeDtypeStruct((d_exp, n_exp), BF16), jax.ShapeDtypeStruct(v_tab.shape, BF16)],
        compiler_params=_params("parallel"),
        name="peer_route",
    )(hb, w_q, subkeys, u_tab, v_tab)


_W_ROWS = PEER_NKEYS // 2


def _wbuild_kernel(a_ref, b_ref, bt_ref, g_ref, w_ref):
    n_tok, n_sel = a_ref.shape
    key_rows = lax.broadcasted_iota(jnp.int32, (PEER_NKEYS, n_sel), 0)
    key_cols = lax.broadcasted_iota(jnp.int32, (n_sel, PEER_NKEYS), 1)
    for t in range(n_tok):
        pa = jnp.where(key_rows == a_ref[t:t + 1, :], g_ref[t:t + 1, :], 0.0).astype(BF16)
        if t % 3:
            pb = jnp.where(key_cols == bt_ref[:, t:t + 1], 1.0, 0.0).astype(BF16)
            w = jnp.dot(pa, pb, preferred_element_type=F32)
        else:
            pb = jnp.where(key_rows == b_ref[t:t + 1, :], 1.0, 0.0).astype(BF16)
            w = lax.dot_general(pa, pb, _NT, preferred_element_type=F32)
        words = pltpu.pack_elementwise([w[:_W_ROWS, :], w[_W_ROWS:, :]], packed_dtype=BF16)
        w_ref[t // SUBLANES, pl.ds(t % SUBLANES, _W_ROWS, stride=SUBLANES), :] = (
            lax.bitcast_convert_type(words, jnp.uint32))


def _wbuild_call(a_tm, b_tm, b_sm, g_tm, tt):
    t, n_sel = a_tm.shape
    rows = _W_ROWS * SUBLANES
    tok_major = pl.BlockSpec((tt, n_sel), lambda i: (i, 0))
    return pl.pallas_call(
        _wbuild_kernel,
        grid=(t // tt,),
        in_specs=[tok_major, tok_major, pl.BlockSpec((n_sel, tt), lambda i: (0, i)), tok_major],
        out_specs=pl.BlockSpec((tt // SUBLANES, rows, PEER_NKEYS), lambda i: (i, 0, 0)),
        out_shape=jax.ShapeDtypeStruct((t // SUBLANES, rows, PEER_NKEYS), jnp.uint32),
        compiler_params=_params("parallel"),
        name="peer_gate_map",
    )(a_tm, b_tm, b_sm, g_tm)


def _gelu(x):
    return 0.5 * x * (1.0 + lax.erf(x * math.sqrt(0.5)))


def _peer_kernel(hb_ref, ul_ref, uh_ref, vl_ref, vh_ref, w_ref, h_ref, g_ref, b_ref, o_ref, acc_ref, gs_ref):
    j = pl.program_id(1)

    @pl.when(j == 0)
    def _():
        acc_ref[...] = jnp.zeros_like(acc_ref)

    tm = hb_ref.shape[0]
    half = vl_ref.shape[0]
    hb = hb_ref[...]
    for part, u_ref in enumerate((ul_ref, uh_ref)):
        act = _gelu(jnp.dot(hb, u_ref[...], preferred_element_type=F32))
        for a in range(half // PEER_NKEYS):
            word = w_ref[:, a, :, :].reshape(tm, PEER_NKEYS)
            w_a = pltpu.unpack_elementwise(word, index=part, packed_dtype=BF16, unpacked_dtype=F32)
            cols = slice(a * PEER_NKEYS, (a + 1) * PEER_NKEYS)
            gs_ref[part, :, cols] = (w_a * act[:, cols]).astype(BF16)
    acc_ref[...] += (jnp.dot(gs_ref[0], vl_ref[...], preferred_element_type=F32)
                     + jnp.dot(gs_ref[1], vh_ref[...], preferred_element_type=F32))

    @pl.when(j == pl.num_programs(1) - 1)
    def _():
        o_ref[...] = _layer_norm(DEEPNORM_ALPHA * h_ref[...] + acc_ref[...], g_ref[...], b_ref[...])


def _peer_call(hb, u_t, v_tab, w4, h, g, b, tm, tn):
    t, d = h.shape
    n_exp = v_tab.shape[0]
    half = tn // 2
    n_steps = n_exp // tn
    lo = pl.BlockSpec((half, d), lambda i, j: (j, 0))
    hi = pl.BlockSpec((half, d), lambda i, j: (j + n_steps, 0))
    tok = pl.BlockSpec((tm, d), lambda i, j: (i, 0))
    vec = pl.BlockSpec((1, d), lambda i, j: (0, 0))
    return pl.pallas_call(
        _peer_kernel,
        grid=(t // tm, n_steps),
        in_specs=[tok, pl.BlockSpec((d, half), lambda i, j: (0, j)),
                  pl.BlockSpec((d, half), lambda i, j: (0, j + n_steps)), lo, hi,
                  pl.BlockSpec((tm // SUBLANES, half // PEER_NKEYS, SUBLANES, PEER_NKEYS),
                               lambda i, j: (i, j, 0, 0)),
                  tok, vec, vec],
        out_specs=tok,
        out_shape=jax.ShapeDtypeStruct((t, d), F32),
        scratch_shapes=[pltpu.VMEM((tm, d), F32), pltpu.VMEM((2, tm, half), BF16)],
        compiler_params=_params("parallel", "arbitrary"),
        name="peer_dense",
    )(hb, u_t, u_t, v_tab, v_tab, w4, h, g, b)


def _layer(x2, batch, seq, w_in, w_gate_up, b_gate, gla_g, ret_g, w_out, ln1_g, ln1_b,
           w_q, subkeys, u_tab, v_tab, ln2_g, ln2_b):
    t, d = x2.shape
    tiles = _tiles(t, seq)
    row = lambda p: p.reshape(1, -1).astype(F32)

    w_up = jnp.pad(w_gate_up, ((0, LANES - GLA_GATE_RANK), (0, 0))).astype(BF16)
    w_t = jnp.swapaxes(w_in, 0, 1)
    bc, xb = _gate_call(x2, w_t, w_up, row(b_gate), tiles.gate_tokens)
    proj = _proj_call(xb, w_t, tiles.proj_tokens, tiles.proj_cols)

    half = RET_DK // 2
    inv = ROPE_BASE ** (-jnp.arange(half, dtype=F32) / half)
    ang = jnp.arange(seq).astype(F32)[:, None] * inv[None, :]
    mix = _mixer_call(proj, bc, jnp.cos(ang), jnp.sin(ang), row(gla_g), row(ret_g), batch, seq,
                      tiles.mixer_chunks)

    h, hb = _out_ln_call(mix, w_out.astype(BF16), x2, row(ln1_g), row(ln1_b), tiles.out_tokens)

    sk = subkeys.reshape(PEER_HEADS * 2, PEER_NKEYS, -1).astype(BF16)
    a_sel, b_sel, bt_sel, gates, u_b, v_b = _route_call(hb, w_q.astype(BF16), sk, u_tab, v_tab,
                                                        tiles.route_tokens)
    w3 = _wbuild_call(a_sel, b_sel, bt_sel, gates, tiles.map_tokens)
    w4 = w3.reshape(t // SUBLANES, _W_ROWS, SUBLANES, PEER_NKEYS)

    return _peer_call(hb, u_b, v_b, w4, h, row(ln2_g), row(ln2_b), tiles.dense_tokens, tiles.dense_experts)


def kernel(x, w_in, w_gla_gate_up, b_gla_gate, gla_norm_g, ret_norm_g, w_out, ln1_g, ln1_b,
           w_peer_q, peer_subkeys, peer_u, peer_v, ln2_g, ln2_b):
    batch, seq, d = x.shape
    h = x.reshape(batch * seq, d)
    for l in range(DEPTH):
        h = _layer(h, batch, seq, w_in[l], w_gla_gate_up[l], b_gla_gate[l], gla_norm_g[l], ret_norm_g[l],
                   w_out[l], ln1_g[l], ln1_b[l], w_peer_q[l], peer_subkeys[l], peer_u[l], peer_v[l],
                   ln2_g[l], ln2_b[l])
    return h.reshape(batch, seq, d)
```

```python
import functools
import math
from typing import NamedTuple

import jax
import jax.numpy as jnp
from jax import lax
from jax.experimental import pallas as pl
from jax.experimental.pallas import tpu as pltpu

F32 = jnp.float32
BF16 = jnp.bfloat16

DEPTH = 1
CHUNK = 64
GLA_HEADS = 4
GLA_DK = 128
GLA_DV = 256
GLA_GATE_RANK = 16
GLA_TAU = 16.0
RET_HEADS = 4
RET_DK = 256
RET_DV = 256
ROPE_BASE = 10000.0
PEER_HEADS = 8
PEER_NKEYS = 128
PEER_TOPK = 16
LN_EPS = 1e-5
DEEPNORM_ALPHA = (2 * DEPTH) ** 0.25

LANES = 128
SUBLANES = 8
VMEM_LIMIT = 56 * 1024 * 1024

_C_GQ = 0
_C_GK = _C_GQ + GLA_HEADS * GLA_DK
_C_GV = _C_GK + GLA_HEADS * GLA_DK
_C_GR = _C_GV + GLA_HEADS * GLA_DV
_C_RQ = _C_GR + GLA_HEADS * GLA_DV
_C_RK = _C_RQ + RET_HEADS * RET_DK
_C_RV = _C_RK + RET_HEADS * RET_DK
_C_RR = _C_RV + RET_HEADS * RET_DV
_C_END = _C_RR + RET_HEADS * RET_DV

_NT = (((1,), (1,)), ((), ()))
_TN = (((0,), (0,)), ((), ()))


class _Tiles(NamedTuple):
    gate_tokens: int
    proj_tokens: int
    proj_cols: int
    mixer_chunks: int
    out_tokens: int
    route_tokens: int
    map_tokens: int
    dense_tokens: int
    dense_experts: int


def _tiles(t, seq):
    return _Tiles(gate_tokens=min(1024, t), proj_tokens=min(2048, t), proj_cols=1024,
                  mixer_chunks=min(8, seq // CHUNK), out_tokens=min(512, t), route_tokens=min(256, t),
                  map_tokens=min(256, t), dense_tokens=min(512, t), dense_experts=1024)


def _params(*sem):
    return pltpu.CompilerParams(dimension_semantics=sem, vmem_limit_bytes=VMEM_LIMIT)


_GATE_ROWS = 256


def _gate_kernel(x_ref, wl_ref, wu_ref, b_ref, o_ref, xb_ref, *, n_split):
    tm = x_ref.shape[0] // n_split
    r = lax.broadcasted_iota(jnp.int32, (tm, tm), 0)
    c = lax.broadcasted_iota(jnp.int32, (tm, tm), 1)
    tri = jnp.where((r >= c) & ((r // CHUNK) == (c // CHUNK)), 1.0, 0.0).astype(F32)
    wl = wl_ref[...].astype(BF16)
    for s in range(n_split):
        rows = slice(s * tm, (s + 1) * tm)
        xb = x_ref[rows, :].astype(BF16)
        xb_ref[rows, :] = xb
        glr = lax.dot_general(xb, wl, _NT, preferred_element_type=F32)
        lane = lax.broadcasted_iota(jnp.int32, glr.shape, 1)
        glr = jnp.where(lane < GLA_GATE_RANK, glr, 0.0)
        z = jnp.dot(glr.astype(BF16), wu_ref[...], preferred_element_type=F32) + b_ref[...]
        log_a = (jnp.minimum(z, 0.0) - jnp.log1p(jnp.exp(-jnp.abs(z)))) / GLA_TAU
        o_ref[rows, :] = jnp.dot(tri, log_a, preferred_element_type=F32, precision=lax.Precision.HIGHEST)


def _gate_call(x, w_t, w_up, b_gate, tm):
    t, d = x.shape
    n = w_up.shape[1]
    return pl.pallas_call(
        functools.partial(_gate_kernel, n_split=max(tm // _GATE_ROWS, 1)),
        grid=(t // tm,),
        in_specs=[pl.BlockSpec((tm, d), lambda i: (i, 0)),
                  pl.BlockSpec((LANES, d), lambda i: (_C_RQ // LANES, 0)),
                  pl.BlockSpec(w_up.shape, lambda i: (0, 0)),
                  pl.BlockSpec((1, n), lambda i: (0, 0))],
        out_specs=[pl.BlockSpec((tm, n), lambda i: (i, 0)),
                   pl.BlockSpec((tm, d), lambda i: (i, 0))],
        out_shape=[jax.ShapeDtypeStruct((t, n), F32), jax.ShapeDtypeStruct((t, d), BF16)],
        compiler_params=_params("parallel"),
        name="gla_gate",
    )(x, w_t, w_up, b_gate)


def _proj_kernel(x_ref, w_ref, o_ref):
    o_ref[...] = lax.dot_general(x_ref[...], w_ref[...].astype(BF16), _NT,
                                 preferred_element_type=F32).astype(o_ref.dtype)


def _proj_call(xb, w_t, tm, tn):
    m, k = xb.shape
    n_gla_tiles = _C_RQ // tn
    assert n_gla_tiles * tn == _C_RQ
    w_row = lambda i, j: (pl.multiple_of(j * tn + jnp.where(j >= n_gla_tiles, GLA_GATE_RANK, 0), GLA_GATE_RANK), 0)
    return pl.pallas_call(
        _proj_kernel,
        grid=(m // tm, _C_END // tn),
        in_specs=[pl.BlockSpec((tm, k), lambda i, j: (i, 0)),
                  pl.BlockSpec((pl.Element(tn), pl.Element(k)), w_row)],
        out_specs=pl.BlockSpec((tm, tn), lambda i, j: (i, j)),
        out_shape=jax.ShapeDtypeStruct((m, _C_END), BF16),
        compiler_params=_params("parallel", "parallel"),
        name="proj_matmul",
    )(xb, w_t)


def _silu(x):
    return x * jax.nn.sigmoid(x)


def _mixer_kernel(p_ref, bc_ref, cos_ref, sin_ref, gg_ref, rg_ref, o_ref, sg_ref, sr_ref, *, n_chunks):
    @pl.when(pl.program_id(1) == 0)
    def _():
        sg_ref[...] = jnp.zeros_like(sg_ref)
        sr_ref[...] = jnp.zeros_like(sr_ref)

    ri = lax.broadcasted_iota(jnp.int32, (CHUNK, CHUNK), 0)
    ci = lax.broadcasted_iota(jnp.int32, (CHUNK, CHUNK), 1)
    lower = ri >= ci
    dist = jnp.abs(ri - ci).astype(F32)
    row = lax.broadcasted_iota(jnp.int32, (CHUNK, RET_DK), 0).astype(F32)
    mid = CHUNK // 2

    def chunk_body(c, carry):
        rows = pl.ds(pl.multiple_of(c * CHUNK, CHUNK), CHUNK)

        def col(ref, base, h, width):
            return ref[rows, base + h * width: base + (h + 1) * width]

        for h in range(GLA_HEADS):
            q = col(p_ref, _C_GQ, h, GLA_DK).astype(F32) * (GLA_DK ** -0.5)
            k = col(p_ref, _C_GK, h, GLA_DK).astype(F32)
            v = col(p_ref, _C_GV, h, GLA_DV)
            gate = col(p_ref, _C_GR, h, GLA_DV).astype(F32)
            bc = col(bc_ref, 0, h, GLA_DK)
            b_last = bc[CHUNK - 1:CHUNK, :]
            b_mid = bc[mid:mid + 1, :]
            e_fwd = jnp.exp(bc - b_mid)
            e_bwd = jnp.exp(b_mid - bc)
            st = sg_ref[h]
            o = lax.dot_general((q * jnp.exp(bc)).astype(BF16), st.astype(BF16), _NT,
                                preferred_element_type=F32)
            a_lo = lax.dot_general((q * e_fwd).astype(BF16), (k * e_bwd).astype(BF16), _NT,
                                   preferred_element_type=F32)
            a_up = lax.dot_general((q * e_bwd).astype(BF16), (k * e_fwd).astype(BF16), _NT,
                                   preferred_element_type=F32)
            a = jnp.where(lower, a_lo, a_up)
            o = o + jnp.dot(a.astype(BF16), v, preferred_element_type=F32)
            u_t = lax.dot_general(v, (k * jnp.exp(b_last - bc)).astype(BF16), _TN,
                                  preferred_element_type=F32)
            sg_ref[h] = st * jnp.exp(b_last) + u_t
            y = o * lax.rsqrt(jnp.mean(o * o, axis=-1, keepdims=True) + LN_EPS)
            y = y * gg_ref[:, h * GLA_DV:(h + 1) * GLA_DV] * _silu(gate)
            o_ref[rows, h * GLA_DV:(h + 1) * GLA_DV] = y.astype(o_ref.dtype)

        cos = cos_ref[rows, :]
        sin = sin_ref[rows, :]
        half = RET_DK // 2

        def rotary(t):
            t1, t2 = t[:, :half], t[:, half:]
            return jnp.concatenate([t1 * cos - t2 * sin, t1 * sin + t2 * cos], axis=-1)

        for h in range(RET_HEADS):
            log_gamma = math.log(1.0 - 2.0 ** (-5.0 - h))
            q = rotary(col(p_ref, _C_RQ, h, RET_DK).astype(F32))
            k = rotary(col(p_ref, _C_RK, h, RET_DK).astype(F32)) * (RET_DK ** -0.5)
            v = col(p_ref, _C_RV, h, RET_DV)
            gate = col(p_ref, _C_RR, h, RET_DV).astype(F32)
            qb = q.astype(BF16)
            st = sr_ref[h]
            o = lax.dot_general(qb, st.astype(BF16), _NT, preferred_element_type=F32)
            o = o * jnp.exp(log_gamma * (row + 1.0))
            a = lax.dot_general(qb, k.astype(BF16), _NT, preferred_element_type=F32)
            a = a * jnp.exp(log_gamma * dist)
            o = o + jnp.dot(a.astype(BF16), v, preferred_element_type=F32)
            k_dec = jnp.exp(log_gamma * (CHUNK - 1.0 - row))
            u_t = lax.dot_general(v, (k * k_dec).astype(BF16), _TN, preferred_element_type=F32)
            sr_ref[h] = st * math.exp(log_gamma * CHUNK) + u_t
            oc = o - jnp.mean(o, axis=-1, keepdims=True)
            y = oc * lax.rsqrt(jnp.mean(oc * oc, axis=-1, keepdims=True) + LN_EPS)
            y = y * rg_ref[:, h * RET_DV:(h + 1) * RET_DV] * _silu(gate)
            c0 = GLA_HEADS * GLA_DV + h * RET_DV
            o_ref[rows, c0:c0 + RET_DV] = y.astype(o_ref.dtype)
        return carry

    lax.fori_loop(0, n_chunks, chunk_body, 0, unroll=n_chunks)


def _mixer_call(proj, bc, cos, sin, gla_g, ret_g, batch, seq, n_chunks):
    t = proj.shape[0]
    tb = n_chunks * CHUNK
    groups = seq // tb
    width = GLA_HEADS * GLA_DV + RET_HEADS * RET_DV
    tok = lambda b, g: (b * groups + g, 0)
    return pl.pallas_call(
        functools.partial(_mixer_kernel, n_chunks=n_chunks),
        grid=(batch, groups),
        in_specs=[pl.BlockSpec((tb, proj.shape[1]), tok),
                  pl.BlockSpec((tb, bc.shape[1]), tok),
                  pl.BlockSpec((tb, cos.shape[1]), lambda b, g: (g, 0)),
                  pl.BlockSpec((tb, sin.shape[1]), lambda b, g: (g, 0)),
                  pl.BlockSpec(gla_g.shape, lambda b, g: (0, 0)),
                  pl.BlockSpec(ret_g.shape, lambda b, g: (0, 0))],
        out_specs=pl.BlockSpec((tb, width), tok),
        out_shape=jax.ShapeDtypeStruct((t, width), BF16),
        scratch_shapes=[pltpu.VMEM((GLA_HEADS, GLA_DV, GLA_DK), F32),
                        pltpu.VMEM((RET_HEADS, RET_DV, RET_DK), F32)],
        compiler_params=_params("parallel", "arbitrary"),
        name="mixer",
    )(proj, bc, cos, sin, gla_g, ret_g)


def _layer_norm(y, g, b):
    mu = jnp.mean(y, axis=-1, keepdims=True)
    yc = y - mu
    var = jnp.mean(yc * yc, axis=-1, keepdims=True)
    return yc * lax.rsqrt(var + LN_EPS) * g + b


def _out_ln_kernel(m_ref, w_ref, x_ref, g_ref, b_ref, h_ref, hb_ref):
    half = m_ref.shape[0] // 2
    for r in range(2):
        rows = slice(r * half, (r + 1) * half)
        mix = jnp.dot(m_ref[rows, :], w_ref[...], preferred_element_type=F32)
        h = _layer_norm(DEEPNORM_ALPHA * x_ref[rows, :] + mix, g_ref[...], b_ref[...])
        h_ref[rows, :] = h
        hb_ref[rows, :] = h.astype(BF16)


def _out_ln_call(mix, w_out, x, g, b, tm):
    t, d = x.shape
    return pl.pallas_call(
        _out_ln_kernel,
        grid=(t // tm,),
        in_specs=[pl.BlockSpec((tm, mix.shape[1]), lambda i: (i, 0)),
                  pl.BlockSpec(w_out.shape, lambda i: (0, 0)),
                  pl.BlockSpec((tm, d), lambda i: (i, 0)),
                  pl.BlockSpec((1, d), lambda i: (0, 0)),
                  pl.BlockSpec((1, d), lambda i: (0, 0))],
        out_specs=[pl.BlockSpec((tm, d), lambda i: (i, 0)),
                   pl.BlockSpec((tm, d), lambda i: (i, 0))],
        out_shape=[jax.ShapeDtypeStruct((t, d), F32), jax.ShapeDtypeStruct((t, d), BF16)],
        compiler_params=_params("parallel"),
        name="out_proj_ln",
    )(mix, w_out, x, g, b)


def _topk_paired(s, key, k):
    h = s.shape[0] // 2
    if key is None:
        key_lo = lax.broadcasted_iota(jnp.int32, (h, s.shape[1]), 0).astype(F32)
        key_hi = key_lo + float(h)
    else:
        key_lo, key_hi = key[:h, :], key[h:, :]
    swap = s[h:, :] > s[:h, :]
    front = jnp.where(swap, s[h:, :], s[:h, :])
    back = jnp.where(swap, s[:h, :], s[h:, :])
    front_key = jnp.where(swap, key_hi, key_lo)
    back_key = jnp.where(swap, key_lo, key_hi)
    vals, keys = [], []
    for _ in range(k):
        m = jnp.max(front, axis=0, keepdims=True)
        pick = jnp.min(jnp.where(front == m, front_key, jnp.inf), axis=0, keepdims=True)
        hit = front_key == pick
        vals.append(m)
        keys.append(pick)
        front = jnp.where(hit, back, front)
        front_key = jnp.where(hit, back_key, front_key)
        back = jnp.where(hit, -jnp.inf, back)
    return jnp.concatenate(vals, axis=0), jnp.concatenate(keys, axis=0)


def _route_kernel(h_ref, wq_ref, sk_ref, u_ref, v_ref, a_ref, b_ref, bt_ref, g_ref, ub_ref, vb_ref):
    ub_ref[...] = u_ref[...].T.astype(BF16)
    vb_ref[...] = v_ref[...].astype(BF16)
    kk = PEER_TOPK
    q = jnp.dot(h_ref[...], wq_ref[...], preferred_element_type=F32).astype(BF16)
    n_tok = q.shape[0]
    n_exp = PEER_NKEYS * PEER_NKEYS
    n_cand = sum(kk // (r1 + 1) for r1 in range(kk))
    pad = (-n_cand) % (2 * SUBLANES)
    cand_row = lax.broadcasted_iota(jnp.int32, (n_cand + pad, n_tok), 0).astype(F32) * float(n_exp)
    experts, gates = [], []
    for h in range(PEER_HEADS):
        vals, idxs = [], []
        for p in range(2):
            j = 2 * h + p
            s_t = lax.dot_general(sk_ref[j], q[:, j * LANES:(j + 1) * LANES], _NT,
                                  preferred_element_type=F32)
            v_p, i_p = _topk_paired(s_t, None, kk)
            vals.append(v_p)
            idxs.append(i_p)
        cand, expert = [], []
        for r1 in range(kk):
            n2 = kk // (r1 + 1)
            cand.append(vals[0][r1:r1 + 1, :] + vals[1][:n2, :])
            expert.append(idxs[0][r1:r1 + 1, :] * float(PEER_NKEYS) + idxs[1][:n2, :])
        cand.append(jnp.full((pad, n_tok), -jnp.inf, F32))
        expert.append(jnp.zeros((pad, n_tok), F32))
        top_s, top_key = _topk_paired(jnp.concatenate(cand, axis=0),
                                      cand_row + jnp.concatenate(expert, axis=0), kk)
        e = jnp.exp(top_s - top_s[0:1, :])
        experts.append(top_key)
        gates.append(e / jnp.sum(e, axis=0, keepdims=True))
    top_e = jnp.concatenate(experts, axis=0).astype(jnp.int32) & (n_exp - 1)
    a_ref[...] = (top_e >> (PEER_NKEYS.bit_length() - 1)).T
    second = top_e & (PEER_NKEYS - 1)
    b_ref[...] = second.T
    bt_ref[...] = second
    g_ref[...] = jnp.concatenate(gates, axis=0).T


def _route_call(hb, w_q, subkeys, u_tab, v_tab, tt):
    t, d = hb.shape
    n_sel = PEER_HEADS * PEER_TOPK
    n_steps = t // tt
    n_exp, d_exp = u_tab.shape
    slab = n_exp // n_steps
    assert slab * n_steps == n_exp and slab % (2 * SUBLANES) == 0
    tok_major = pl.BlockSpec((tt, n_sel), lambda i: (i, 0))
    table = pl.BlockSpec((slab, d_exp), lambda i: (i, 0))
    return pl.pallas_call(
        _route_kernel,
        grid=(n_steps,),
        in_specs=[pl.BlockSpec((tt, d), lambda i: (i, 0)),
                  pl.BlockSpec(w_q.shape, lambda i: (0, 0)),
                  pl.BlockSpec(subkeys.shape, lambda i: (0, 0, 0)),
                  table, table],
        out_specs=[tok_major, tok_major, pl.BlockSpec((n_sel, tt), lambda i: (0, i)), tok_major,
                   pl.BlockSpec((d_exp, slab), lambda i: (0, i)), table],
        out_shape=[jax.ShapeDtypeStruct((t, n_sel), jnp.int32), jax.ShapeDtypeStruct((t, n_sel), jnp.int32),
                   jax.ShapeDtypeStruct((n_sel, t), jnp.int32), jax.ShapeDtypeStruct((t, n_sel), F32),
                   jax.ShapeDtypeStruct((d_exp, n_exp), BF16), jax.ShapeDtypeStruct(v_tab.shape, BF16)],
        compiler_params=_params("parallel"),
        name="peer_route",
    )(hb, w_q, subkeys, u_tab, v_tab)


_W_ROWS = PEER_NKEYS // 2


def _wbuild_kernel(a_ref, b_ref, bt_ref, g_ref, w_ref):
    n_tok, n_sel = a_ref.shape
    key_rows = lax.broadcasted_iota(jnp.int32, (PEER_NKEYS, n_sel), 0)
    key_cols = lax.broadcasted_iota(jnp.int32, (n_sel, PEER_NKEYS), 1)
    for t in range(n_tok):
        pa = jnp.where(key_rows == a_ref[t:t + 1, :], g_ref[t:t + 1, :], 0.0).astype(BF16)
        if t % 3:
            pb = jnp.where(key_cols == bt_ref[:, t:t + 1], 1.0, 0.0).astype(BF16)
            w = jnp.dot(pa, pb, preferred_element_type=F32)
        else:
            pb = jnp.where(key_rows == b_ref[t:t + 1, :], 1.0, 0.0).astype(BF16)
            w = lax.dot_general(pa, pb, _NT, preferred_element_type=F32)
        words = pltpu.pack_elementwise([w[:_W_ROWS, :], w[_W_ROWS:, :]], packed_dtype=BF16)
        w_ref[t // SUBLANES, pl.ds(t % SUBLANES, _W_ROWS, stride=SUBLANES), :] = (
            lax.bitcast_convert_type(words, jnp.uint32))


def _wbuild_call(a_tm, b_tm, b_sm, g_tm, tt):
    t, n_sel = a_tm.shape
    rows = _W_ROWS * SUBLANES
    tok_major = pl.BlockSpec((tt, n_sel), lambda i: (i, 0))
    return pl.pallas_call(
        _wbuild_kernel,
        grid=(t // tt,),
        in_specs=[tok_major, tok_major, pl.BlockSpec((n_sel, tt), lambda i: (0, i)), tok_major],
        out_specs=pl.BlockSpec((tt // SUBLANES, rows, PEER_NKEYS), lambda i: (i, 0, 0)),
        out_shape=jax.ShapeDtypeStruct((t // SUBLANES, rows, PEER_NKEYS), jnp.uint32),
        compiler_params=_params("parallel"),
        name="peer_gate_map",
    )(a_tm, b_tm, b_sm, g_tm)


def _gelu(x):
    return 0.5 * x * (1.0 + lax.erf(x * math.sqrt(0.5)))


def _peer_kernel(hb_ref, ul_ref, uh_ref, vl_ref, vh_ref, w_ref, h_ref, g_ref, b_ref, o_ref, acc_ref, gs_ref):
    j = pl.program_id(1)

    @pl.when(j == 0)
    def _():
        acc_ref[...] = jnp.zeros_like(acc_ref)

    tm = hb_ref.shape[0]
    half = vl_ref.shape[0]
    hb = hb_ref[...]
    for part, u_ref in enumerate((ul_ref, uh_ref)):
        act = _gelu(jnp.dot(hb, u_ref[...], preferred_element_type=F32))
        for a in range(half // PEER_NKEYS):
            word = w_ref[:, a, :, :].reshape(tm, PEER_NKEYS)
            w_a = pltpu.unpack_elementwise(word, index=part, packed_dtype=BF16, unpacked_dtype=F32)
            cols = slice(a * PEER_NKEYS, (a + 1) * PEER_NKEYS)
            gs_ref[part, :, cols] = (w_a * act[:, cols]).astype(BF16)
    acc_ref[...] += (jnp.dot(gs_ref[0], vl_ref[...], preferred_element_type=F32)
                     + jnp.dot(gs_ref[1], vh_ref[...], preferred_element_type=F32))

    @pl.when(j == pl.num_programs(1) - 1)
    def _():
        o_ref[...] = _layer_norm(DEEPNORM_ALPHA * h_ref[...] + acc_ref[...], g_ref[...], b_ref[...])


def _peer_call(hb, u_t, v_tab, w4, h, g, b, tm, tn):
    t, d = h.shape
    n_exp = v_tab.shape[0]
    half = tn // 2
    n_steps = n_exp // tn
    lo = pl.BlockSpec((half, d), lambda i, j: (j, 0))
    hi = pl.BlockSpec((half, d), lambda i, j: (j + n_steps, 0))
    tok = pl.BlockSpec((tm, d), lambda i, j: (i, 0))
    vec = pl.BlockSpec((1, d), lambda i, j: (0, 0))
    return pl.pallas_call(
        _peer_kernel,
        grid=(t // tm, n_steps),
        in_specs=[tok, pl.BlockSpec((d, half), lambda i, j: (0, j)),
                  pl.BlockSpec((d, half), lambda i, j: (0, j + n_steps)), lo, hi,
                  pl.BlockSpec((tm // SUBLANES, half // PEER_NKEYS, SUBLANES, PEER_NKEYS),
                               lambda i, j: (i, j, 0, 0)),
                  tok, vec, vec],
        out_specs=tok,
        out_shape=jax.ShapeDtypeStruct((t, d), F32),
        scratch_shapes=[pltpu.VMEM((tm, d), F32), pltpu.VMEM((2, tm, half), BF16)],
        compiler_params=_params("parallel", "arbitrary"),
        name="peer_dense",
    )(hb, u_t, u_t, v_tab, v_tab, w4, h, g, b)


def _layer(x2, batch, seq, w_in, w_gate_up, b_gate, gla_g, ret_g, w_out, ln1_g, ln1_b,
           w_q, subkeys, u_tab, v_tab, ln2_g, ln2_b):
    t, d = x2.shape
    tiles = _tiles(t, seq)
    row = lambda p: p.reshape(1, -1).astype(F32)

    w_up = jnp.pad(w_gate_up, ((0, LANES - GLA_GATE_RANK), (0, 0))).astype(BF16)
    w_t = jnp.swapaxes(w_in, 0, 1)
    bc, xb = _gate_call(x2, w_t, w_up, row(b_gate), tiles.gate_tokens)
    proj = _proj_call(xb, w_t, tiles.proj_tokens, tiles.proj_cols)

    half = RET_DK // 2
    inv = ROPE_BASE ** (-jnp.arange(half, dtype=F32) / half)
    ang = jnp.arange(seq).astype(F32)[:, None] * inv[None, :]
    mix = _mixer_call(proj, bc, jnp.cos(ang), jnp.sin(ang), row(gla_g), row(ret_g), batch, seq,
                      tiles.mixer_chunks)

    h, hb = _out_ln_call(mix, w_out.astype(BF16), x2, row(ln1_g), row(ln1_b), tiles.out_tokens)

    sk = subkeys.reshape(PEER_HEADS * 2, PEER_NKEYS, -1).astype(BF16)
    a_sel, b_sel, bt_sel, gates, u_b, v_b = _route_call(hb, w_q.astype(BF16), sk, u_tab, v_tab,
                                                        tiles.route_tokens)
    w3 = _wbuild_call(a_sel, b_sel, bt_sel, gates, tiles.map_tokens)
    w4 = w3.reshape(t // SUBLANES, _W_ROWS, SUBLANES, PEER_NKEYS)

    return _peer_call(hb, u_b, v_b, w4, h, row(ln2_g), row(ln2_b), tiles.dense_tokens, tiles.dense_experts)


def kernel(x, w_in, w_gla_gate_up, b_gla_gate, gla_norm_g, ret_norm_g, w_out, ln1_g, ln1_b,
           w_peer_q, peer_subkeys, peer_u, peer_v, ln2_g, ln2_b):
    batch, seq, d = x.shape
    h = x.reshape(batch * seq, d)
    for l in range(DEPTH):
        h = _layer(h, batch, seq, w_in[l], w_gla_gate_up[l], b_gla_gate[l], gla_norm_g[l], ret_norm_g[l],
                   w_out[l], ln1_g[l], ln1_b[l], w_peer_q[l], peer_subkeys[l], peer_u[l], peer_v[l],
                   ln2_g[l], ln2_b[l])
    return h.reshape(batch, seq, d)
```

```python
import functools
import math
from typing import NamedTuple

import jax
import jax.numpy as jnp
from jax import lax
from jax.experimental import pallas as pl
from jax.experimental.pallas import tpu as pltpu

F32 = jnp.float32
BF16 = jnp.bfloat16

DEPTH = 1
CHUNK = 64
GLA_HEADS = 4
GLA_DK = 128
GLA_DV = 256
GLA_GATE_RANK = 16
GLA_TAU = 16.0
RET_HEADS = 4
RET_DK = 256
RET_DV = 256
ROPE_BASE = 10000.0
PEER_HEADS = 8
PEER_NKEYS = 128
PEER_TOPK = 16
LN_EPS = 1e-5
DEEPNORM_ALPHA = (2 * DEPTH) ** 0.25

LANES = 128
SUBLANES = 8
VMEM_LIMIT = 56 * 1024 * 1024
VMEM_LIMIT_DENSE = 61 * 1024 * 1024

_C_GQ = 0
_C_GK = _C_GQ + GLA_HEADS * GLA_DK
_C_GV = _C_GK + GLA_HEADS * GLA_DK
_C_GR = _C_GV + GLA_HEADS * GLA_DV
_C_RQ = _C_GR + GLA_HEADS * GLA_DV
_C_RK = _C_RQ + RET_HEADS * RET_DK
_C_RV = _C_RK + RET_HEADS * RET_DK
_C_RR = _C_RV + RET_HEADS * RET_DV
_C_END = _C_RR + RET_HEADS * RET_DV

_NT = (((1,), (1,)), ((), ()))
_TN = (((0,), (0,)), ((), ()))


class _Tiles(NamedTuple):
    gate_tokens: int
    proj_tokens: int
    proj_cols: int
    mixer_chunks: int
    out_tokens: int
    route_tokens: int
    map_tokens: int
    dense_tokens: int
    dense_experts: int


def _tiles(t, seq):
    return _Tiles(gate_tokens=min(1024, t), proj_tokens=min(2048, t), proj_cols=1024,
                  mixer_chunks=min(8, seq // CHUNK), out_tokens=min(512, t), route_tokens=min(256, t),
                  map_tokens=min(256, t), dense_tokens=min(1024, t), dense_experts=1024)


def _params(*sem):
    return pltpu.CompilerParams(dimension_semantics=sem, vmem_limit_bytes=VMEM_LIMIT)


_GATE_ROWS = 256


def _gate_kernel(x_ref, wl_ref, wu_ref, b_ref, o_ref, xb_ref, *, n_split):
    tm = x_ref.shape[0] // n_split
    r = lax.broadcasted_iota(jnp.int32, (tm, tm), 0)
    c = lax.broadcasted_iota(jnp.int32, (tm, tm), 1)
    tri = jnp.where((r >= c) & ((r // CHUNK) == (c // CHUNK)), 1.0, 0.0).astype(F32)
    wl = wl_ref[...].astype(BF16)
    for s in range(n_split):
        rows = slice(s * tm, (s + 1) * tm)
        xb = x_ref[rows, :].astype(BF16)
        xb_ref[rows, :] = xb
        glr = lax.dot_general(xb, wl, _NT, preferred_element_type=F32)
        lane = lax.broadcasted_iota(jnp.int32, glr.shape, 1)
        glr = jnp.where(lane < GLA_GATE_RANK, glr, 0.0)
        z = jnp.dot(glr.astype(BF16), wu_ref[...], preferred_element_type=F32) + b_ref[...]
        log_a = (jnp.minimum(z, 0.0) - jnp.log1p(jnp.exp(-jnp.abs(z)))) / GLA_TAU
        o_ref[rows, :] = jnp.dot(tri, log_a, preferred_element_type=F32, precision=lax.Precision.HIGHEST)


def _gate_call(x, w_t, w_up, b_gate, tm):
    t, d = x.shape
    n = w_up.shape[1]
    return pl.pallas_call(
        functools.partial(_gate_kernel, n_split=max(tm // _GATE_ROWS, 1)),
        grid=(t // tm,),
        in_specs=[pl.BlockSpec((tm, d), lambda i: (i, 0)),
                  pl.BlockSpec((LANES, d), lambda i: (_C_RQ // LANES, 0)),
                  pl.BlockSpec(w_up.shape, lambda i: (0, 0)),
                  pl.BlockSpec((1, n), lambda i: (0, 0))],
        out_specs=[pl.BlockSpec((tm, n), lambda i: (i, 0)),
                   pl.BlockSpec((tm, d), lambda i: (i, 0))],
        out_shape=[jax.ShapeDtypeStruct((t, n), F32), jax.ShapeDtypeStruct((t, d), BF16)],
        compiler_params=_params("parallel"),
        name="gla_gate",
    )(x, w_t, w_up, b_gate)


def _proj_kernel(x_ref, w_ref, o_ref):
    o_ref[...] = lax.dot_general(x_ref[...], w_ref[...].astype(BF16), _NT,
                                 preferred_element_type=F32).astype(o_ref.dtype)


def _proj_call(xb, w_t, tm, tn):
    m, k = xb.shape
    n_gla_tiles = _C_RQ // tn
    assert n_gla_tiles * tn == _C_RQ
    w_row = lambda i, j: (pl.multiple_of(j * tn + jnp.where(j >= n_gla_tiles, GLA_GATE_RANK, 0), GLA_GATE_RANK), 0)
    return pl.pallas_call(
        _proj_kernel,
        grid=(m // tm, _C_END // tn),
        in_specs=[pl.BlockSpec((tm, k), lambda i, j: (i, 0)),
                  pl.BlockSpec((pl.Element(tn), pl.Element(k)), w_row)],
        out_specs=pl.BlockSpec((tm, tn), lambda i, j: (i, j)),
        out_shape=jax.ShapeDtypeStruct((m, _C_END), BF16),
        compiler_params=_params("parallel", "parallel"),
        name="proj_matmul",
    )(xb, w_t)


def _silu(x):
    return x * jax.nn.sigmoid(x)


def _mixer_kernel(p_ref, bc_ref, cos_ref, sin_ref, gg_ref, rg_ref, o_ref, sg_ref, sr_ref, *, n_chunks):
    @pl.when(pl.program_id(1) == 0)
    def _():
        sg_ref[...] = jnp.zeros_like(sg_ref)
        sr_ref[...] = jnp.zeros_like(sr_ref)

    ri = lax.broadcasted_iota(jnp.int32, (CHUNK, CHUNK), 0)
    ci = lax.broadcasted_iota(jnp.int32, (CHUNK, CHUNK), 1)
    lower = ri >= ci
    dist = jnp.abs(ri - ci).astype(F32)
    row = lax.broadcasted_iota(jnp.int32, (CHUNK, RET_DK), 0).astype(F32)
    mid = CHUNK // 2

    def chunk_body(c, carry):
        rows = pl.ds(pl.multiple_of(c * CHUNK, CHUNK), CHUNK)

        def col(ref, base, h, width):
            return ref[rows, base + h * width: base + (h + 1) * width]

        for h in range(GLA_HEADS):
            q = col(p_ref, _C_GQ, h, GLA_DK).astype(F32) * (GLA_DK ** -0.5)
            k = col(p_ref, _C_GK, h, GLA_DK).astype(F32)
            v = col(p_ref, _C_GV, h, GLA_DV)
            gate = col(p_ref, _C_GR, h, GLA_DV).astype(F32)
            bc = col(bc_ref, 0, h, GLA_DK)
            b_last = bc[CHUNK - 1:CHUNK, :]
            b_mid = bc[mid:mid + 1, :]
            e_fwd = jnp.exp(bc - b_mid)
            e_bwd = jnp.exp(b_mid - bc)
            st = sg_ref[h]
            o = lax.dot_general((q * jnp.exp(bc)).astype(BF16), st.astype(BF16), _NT,
                                preferred_element_type=F32)
            a_lo = lax.dot_general((q * e_fwd).astype(BF16), (k * e_bwd).astype(BF16), _NT,
                                   preferred_element_type=F32)
            a_up = lax.dot_general((q * e_bwd).astype(BF16), (k * e_fwd).astype(BF16), _NT,
                                   preferred_element_type=F32)
            a = jnp.where(lower, a_lo, a_up)
            o = o + jnp.dot(a.astype(BF16), v, preferred_element_type=F32)
            u_t = lax.dot_general(v, (k * jnp.exp(b_last - bc)).astype(BF16), _TN,
                                  preferred_element_type=F32)
            sg_ref[h] = st * jnp.exp(b_last) + u_t
            y = o * lax.rsqrt(jnp.mean(o * o, axis=-1, keepdims=True) + LN_EPS)
            y = y * gg_ref[:, h * GLA_DV:(h + 1) * GLA_DV] * _silu(gate)
            o_ref[rows, h * GLA_DV:(h + 1) * GLA_DV] = y.astype(o_ref.dtype)

        cos = cos_ref[rows, :]
        sin = sin_ref[rows, :]
        half = RET_DK // 2

        def rotary(t):
            t1, t2 = t[:, :half], t[:, half:]
            return jnp.concatenate([t1 * cos - t2 * sin, t1 * sin + t2 * cos], axis=-1)

        for h in range(RET_HEADS):
            log_gamma = math.log(1.0 - 2.0 ** (-5.0 - h))
            q = rotary(col(p_ref, _C_RQ, h, RET_DK).astype(F32))
            k = rotary(col(p_ref, _C_RK, h, RET_DK).astype(F32)) * (RET_DK ** -0.5)
            v = col(p_ref, _C_RV, h, RET_DV)
            gate = col(p_ref, _C_RR, h, RET_DV).astype(F32)
            qb = q.astype(BF16)
            st = sr_ref[h]
            o = lax.dot_general(qb, st.astype(BF16), _NT, preferred_element_type=F32)
            o = o * jnp.exp(log_gamma * (row + 1.0))
            a = lax.dot_general(qb, k.astype(BF16), _NT, preferred_element_type=F32)
            a = a * jnp.exp(log_gamma * dist)
            o = o + jnp.dot(a.astype(BF16), v, preferred_element_type=F32)
            k_dec = jnp.exp(log_gamma * (CHUNK - 1.0 - row))
            u_t = lax.dot_general(v, (k * k_dec).astype(BF16), _TN, preferred_element_type=F32)
            sr_ref[h] = st * math.exp(log_gamma * CHUNK) + u_t
            oc = o - jnp.mean(o, axis=-1, keepdims=True)
            y = oc * lax.rsqrt(jnp.mean(oc * oc, axis=-1, keepdims=True) + LN_EPS)
            y = y * rg_ref[:, h * RET_DV:(h + 1) * RET_DV] * _silu(gate)
            c0 = GLA_HEADS * GLA_DV + h * RET_DV
            o_ref[rows, c0:c0 + RET_DV] = y.astype(o_ref.dtype)
        return carry

    lax.fori_loop(0, n_chunks, chunk_body, 0, unroll=n_chunks)


def _mixer_call(proj, bc, cos, sin, gla_g, ret_g, batch, seq, n_chunks):
    t = proj.shape[0]
    tb = n_chunks * CHUNK
    groups = seq // tb
    width = GLA_HEADS * GLA_DV + RET_HEADS * RET_DV
    tok = lambda b, g: (b * groups + g, 0)
    return pl.pallas_call(
        functools.partial(_mixer_kernel, n_chunks=n_chunks),
        grid=(batch, groups),
        in_specs=[pl.BlockSpec((tb, proj.shape[1]), tok),
                  pl.BlockSpec((tb, bc.shape[1]), tok),
                  pl.BlockSpec((tb, cos.shape[1]), lambda b, g: (g, 0)),
                  pl.BlockSpec((tb, sin.shape[1]), lambda b, g: (g, 0)),
                  pl.BlockSpec(gla_g.shape, lambda b, g: (0, 0)),
                  pl.BlockSpec(ret_g.shape, lambda b, g: (0, 0))],
        out_specs=pl.BlockSpec((tb, width), tok),
        out_shape=jax.ShapeDtypeStruct((t, width), BF16),
        scratch_shapes=[pltpu.VMEM((GLA_HEADS, GLA_DV, GLA_DK), F32),
                        pltpu.VMEM((RET_HEADS, RET_DV, RET_DK), F32)],
        compiler_params=_params("parallel", "arbitrary"),
        name="mixer",
    )(proj, bc, cos, sin, gla_g, ret_g)


def _layer_norm(y, g, b):
    mu = jnp.mean(y, axis=-1, keepdims=True)
    yc = y - mu
    var = jnp.mean(yc * yc, axis=-1, keepdims=True)
    return yc * lax.rsqrt(var + LN_EPS) * g + b


def _out_ln_kernel(m_ref, w_ref, x_ref, g_ref, b_ref, h_ref, hb_ref):
    half = m_ref.shape[0] // 2
    for r in range(2):
        rows = slice(r * half, (r + 1) * half)
        mix = jnp.dot(m_ref[rows, :], w_ref[...], preferred_element_type=F32)
        h = _layer_norm(DEEPNORM_ALPHA * x_ref[rows, :] + mix, g_ref[...], b_ref[...])
        h_ref[rows, :] = h
        hb_ref[rows, :] = h.astype(BF16)


def _out_ln_call(mix, w_out, x, g, b, tm):
    t, d = x.shape
    return pl.pallas_call(
        _out_ln_kernel,
        grid=(t // tm,),
        in_specs=[pl.BlockSpec((tm, mix.shape[1]), lambda i: (i, 0)),
                  pl.BlockSpec(w_out.shape, lambda i: (0, 0)),
                  pl.BlockSpec((tm, d), lambda i: (i, 0)),
                  pl.BlockSpec((1, d), lambda i: (0, 0)),
                  pl.BlockSpec((1, d), lambda i: (0, 0))],
        out_specs=[pl.BlockSpec((tm, d), lambda i: (i, 0)),
                   pl.BlockSpec((tm, d), lambda i: (i, 0))],
        out_shape=[jax.ShapeDtypeStruct((t, d), F32), jax.ShapeDtypeStruct((t, d), BF16)],
        compiler_params=_params("parallel"),
        name="out_proj_ln",
    )(mix, w_out, x, g, b)


def _topk_paired(s, key, k):
    h = s.shape[0] // 2
    if key is None:
        key_lo = lax.broadcasted_iota(jnp.int32, (h, s.shape[1]), 0).astype(F32)
        key_hi = key_lo + float(h)
    else:
        key_lo, key_hi = key[:h, :], key[h:, :]
    swap = s[h:, :] > s[:h, :]
    front = jnp.where(swap, s[h:, :], s[:h, :])
    back = jnp.where(swap, s[:h, :], s[h:, :])
    front_key = jnp.where(swap, key_hi, key_lo)
    back_key = jnp.where(swap, key_lo, key_hi)
    vals, keys = [], []
    for _ in range(k):
        m = jnp.max(front, axis=0, keepdims=True)
        pick = jnp.min(jnp.where(front == m, front_key, jnp.inf), axis=0, keepdims=True)
        hit = front_key == pick
        vals.append(m)
        keys.append(pick)
        front = jnp.where(hit, back, front)
        front_key = jnp.where(hit, back_key, front_key)
        back = jnp.where(hit, -jnp.inf, back)
    return jnp.concatenate(vals, axis=0), jnp.concatenate(keys, axis=0)


def _route_kernel(h_ref, wq_ref, sk_ref, u_ref, v_ref, a_ref, b_ref, bt_ref, g_ref, ub_ref, vb_ref):
    ub_ref[...] = u_ref[...].T.astype(BF16)
    vb_ref[...] = v_ref[...].astype(BF16)
    kk = PEER_TOPK
    q = jnp.dot(h_ref[...], wq_ref[...], preferred_element_type=F32).astype(BF16)
    n_tok = q.shape[0]
    n_exp = PEER_NKEYS * PEER_NKEYS
    n_cand = sum(kk // (r1 + 1) for r1 in range(kk))
    pad = (-n_cand) % (2 * SUBLANES)
    cand_row = lax.broadcasted_iota(jnp.int32, (n_cand + pad, n_tok), 0).astype(F32) * float(n_exp)
    experts, gates = [], []
    for h in range(PEER_HEADS):
        vals, idxs = [], []
        for p in range(2):
            j = 2 * h + p
            s_t = lax.dot_general(sk_ref[j], q[:, j * LANES:(j + 1) * LANES], _NT,
                                  preferred_element_type=F32)
            v_p, i_p = _topk_paired(s_t, None, kk)
            vals.append(v_p)
            idxs.append(i_p)
        cand, expert = [], []
        for r1 in range(kk):
            n2 = kk // (r1 + 1)
            cand.append(vals[0][r1:r1 + 1, :] + vals[1][:n2, :])
            expert.append(idxs[0][r1:r1 + 1, :] * float(PEER_NKEYS) + idxs[1][:n2, :])
        cand.append(jnp.full((pad, n_tok), -jnp.inf, F32))
        expert.append(jnp.zeros((pad, n_tok), F32))
        top_s, top_key = _topk_paired(jnp.concatenate(cand, axis=0),
                                      cand_row + jnp.concatenate(expert, axis=0), kk)
        e = jnp.exp(top_s - top_s[0:1, :])
        experts.append(top_key)
        gates.append(e / jnp.sum(e, axis=0, keepdims=True))
    top_e = jnp.concatenate(experts, axis=0).astype(jnp.int32) & (n_exp - 1)
    a_ref[...] = (top_e >> (PEER_NKEYS.bit_length() - 1)).T
    second = top_e & (PEER_NKEYS - 1)
    b_ref[...] = second.T
    bt_ref[...] = second
    g_ref[...] = jnp.concatenate(gates, axis=0).T


def _route_call(hb, w_q, subkeys, u_tab, v_tab, tt):
    t, d = hb.shape
    n_sel = PEER_HEADS * PEER_TOPK
    n_steps = t // tt
    n_exp, d_exp = u_tab.shape
    slab = n_exp // n_steps
    assert slab * n_steps == n_exp and slab % (2 * SUBLANES) == 0
    tok_major = pl.BlockSpec((tt, n_sel), lambda i: (i, 0))
    table = pl.BlockSpec((slab, d_exp), lambda i: (i, 0))
    return pl.pallas_call(
        _route_kernel,
        grid=(n_steps,),
        in_specs=[pl.BlockSpec((tt, d), lambda i: (i, 0)),
                  pl.BlockSpec(w_q.shape, lambda i: (0, 0)),
                  pl.BlockSpec(subkeys.shape, lambda i: (0, 0, 0)),
                  table, table],
        out_specs=[tok_major, tok_major, pl.BlockSpec((n_sel, tt), lambda i: (0, i)), tok_major,
                   pl.BlockSpec((d_exp, slab), lambda i: (0, i)), table],
        out_shape=[jax.ShapeDtypeStruct((t, n_sel), jnp.int32), jax.ShapeDtypeStruct((t, n_sel), jnp.int32),
                   jax.ShapeDtypeStruct((n_sel, t), jnp.int32), jax.ShapeDtypeStruct((t, n_sel), F32),
                   jax.ShapeDtypeStruct((d_exp, n_exp), BF16), jax.ShapeDtypeStruct(v_tab.shape, BF16)],
        compiler_params=_params("parallel"),
        name="peer_route",
    )(hb, w_q, subkeys, u_tab, v_tab)


_W_ROWS = PEER_NKEYS // 2


def _wbuild_kernel(a_ref, b_ref, bt_ref, g_ref, w_ref):
    n_tok, n_sel = a_ref.shape
    key_rows = lax.broadcasted_iota(jnp.int32, (PEER_NKEYS, n_sel), 0)
    key_cols = lax.broadcasted_iota(jnp.int32, (n_sel, PEER_NKEYS), 1)
    for t in range(n_tok):
        pa = jnp.where(key_rows == a_ref[t:t + 1, :], g_ref[t:t + 1, :], 0.0).astype(BF16)
        if t % 3:
            pb = jnp.where(key_cols == bt_ref[:, t:t + 1], 1.0, 0.0).astype(BF16)
            w = jnp.dot(pa, pb, preferred_element_type=F32)
        else:
            pb = jnp.where(key_rows == b_ref[t:t + 1, :], 1.0, 0.0).astype(BF16)
            w = lax.dot_general(pa, pb, _NT, preferred_element_type=F32)
        words = pltpu.pack_elementwise([w[:_W_ROWS, :], w[_W_ROWS:, :]], packed_dtype=BF16)
        w_ref[t // SUBLANES, pl.ds(t % SUBLANES, _W_ROWS, stride=SUBLANES), :] = (
            lax.bitcast_convert_type(words, jnp.uint32))


def _wbuild_call(a_tm, b_tm, b_sm, g_tm, tt):
    t, n_sel = a_tm.shape
    rows = _W_ROWS * SUBLANES
    tok_major = pl.BlockSpec((tt, n_sel), lambda i: (i, 0))
    return pl.pallas_call(
        _wbuild_kernel,
        grid=(t // tt,),
        in_specs=[tok_major, tok_major, pl.BlockSpec((n_sel, tt), lambda i: (0, i)), tok_major],
        out_specs=pl.BlockSpec((tt // SUBLANES, rows, PEER_NKEYS), lambda i: (i, 0, 0)),
        out_shape=jax.ShapeDtypeStruct((t // SUBLANES, rows, PEER_NKEYS), jnp.uint32),
        compiler_params=_params("parallel"),
        name="peer_gate_map",
    )(a_tm, b_tm, b_sm, g_tm)


def _gelu(x):
    return 0.5 * x * (1.0 + lax.erf(x * math.sqrt(0.5)))


def _peer_kernel(h_ref, ul_ref, uh_ref, vl_ref, vh_ref, w_ref, g_ref, b_ref, o_ref, hb_ref, gs_ref):
    j = pl.program_id(1)

    @pl.when(j == 0)
    def _():
        hb_ref[...] = h_ref[...].astype(BF16)
        o_ref[...] = jnp.zeros_like(o_ref)

    tm = hb_ref.shape[0]
    half = vl_ref.shape[0]
    hb = hb_ref[...]
    for part, u_ref in enumerate((ul_ref, uh_ref)):
        act = _gelu(jnp.dot(hb, u_ref[...], preferred_element_type=F32))
        for a in range(half // PEER_NKEYS):
            word = w_ref[:, a, :, :].reshape(tm, PEER_NKEYS)
            w_a = pltpu.unpack_elementwise(word, index=part, packed_dtype=BF16, unpacked_dtype=F32)
            cols = slice(a * PEER_NKEYS, (a + 1) * PEER_NKEYS)
            gs_ref[part, :, cols] = (w_a * act[:, cols]).astype(BF16)
    o_ref[...] += (jnp.dot(gs_ref[0], vl_ref[...], preferred_element_type=F32)
                   + jnp.dot(gs_ref[1], vh_ref[...], preferred_element_type=F32))

    @pl.when(j == pl.num_programs(1) - 1)
    def _():
        o_ref[...] = _layer_norm(DEEPNORM_ALPHA * h_ref[...] + o_ref[...], g_ref[...], b_ref[...])


def _peer_call(u_t, v_tab, w4, h, g, b, tm, tn):
    t, d = h.shape
    n_exp = v_tab.shape[0]
    half = tn // 2
    n_steps = n_exp // tn
    lo = pl.BlockSpec((half, d), lambda i, j: (j, 0))
    hi = pl.BlockSpec((half, d), lambda i, j: (j + n_steps, 0))
    tok = pl.BlockSpec((tm, d), lambda i, j: (i, 0))
    vec = pl.BlockSpec((1, d), lambda i, j: (0, 0))
    return pl.pallas_call(
        _peer_kernel,
        grid=(t // tm, n_steps),
        in_specs=[pl.BlockSpec((tm, d), lambda i, j: (i, 0), pipeline_mode=pl.Buffered(1)),
                  pl.BlockSpec((d, half), lambda i, j: (0, j)),
                  pl.BlockSpec((d, half), lambda i, j: (0, j + n_steps)), lo, hi,
                  pl.BlockSpec((tm // SUBLANES, half // PEER_NKEYS, SUBLANES, PEER_NKEYS),
                               lambda i, j: (i, j, 0, 0)),
                  vec, vec],
        out_specs=pl.BlockSpec((tm, d), lambda i, j: (i, 0), pipeline_mode=pl.Buffered(1)),
        out_shape=jax.ShapeDtypeStruct((t, d), F32),
        scratch_shapes=[pltpu.VMEM((tm, d), BF16), pltpu.VMEM((2, tm, half), BF16)],
        compiler_params=pltpu.CompilerParams(dimension_semantics=("parallel", "arbitrary"),
                                             vmem_limit_bytes=VMEM_LIMIT_DENSE),
        name="peer_dense",
    )(h, u_t, u_t, v_tab, v_tab, w4, g, b)


def _layer(x2, batch, seq, w_in, w_gate_up, b_gate, gla_g, ret_g, w_out, ln1_g, ln1_b,
           w_q, subkeys, u_tab, v_tab, ln2_g, ln2_b):
    t, d = x2.shape
    tiles = _tiles(t, seq)
    row = lambda p: p.reshape(1, -1).astype(F32)

    w_up = jnp.pad(w_gate_up, ((0, LANES - GLA_GATE_RANK), (0, 0))).astype(BF16)
    w_t = jnp.swapaxes(w_in, 0, 1)
    bc, xb = _gate_call(x2, w_t, w_up, row(b_gate), tiles.gate_tokens)
    proj = _proj_call(xb, w_t, tiles.proj_tokens, tiles.proj_cols)

    half = RET_DK // 2
    inv = ROPE_BASE ** (-jnp.arange(half, dtype=F32) / half)
    ang = jnp.arange(seq).astype(F32)[:, None] * inv[None, :]
    mix = _mixer_call(proj, bc, jnp.cos(ang), jnp.sin(ang), row(gla_g), row(ret_g), batch, seq,
                      tiles.mixer_chunks)

    h, hb = _out_ln_call(mix, w_out.astype(BF16), x2, row(ln1_g), row(ln1_b), tiles.out_tokens)

    sk = subkeys.reshape(PEER_HEADS * 2, PEER_NKEYS, -1).astype(BF16)
    a_sel, b_sel, bt_sel, gates, u_b, v_b = _route_call(hb, w_q.astype(BF16), sk, u_tab, v_tab,
                                                        tiles.route_tokens)
    w3 = _wbuild_call(a_sel, b_sel, bt_sel, gates, tiles.map_tokens)
    w4 = w3.reshape(t // SUBLANES, _W_ROWS, SUBLANES, PEER_NKEYS)

    return _peer_call(u_b, v_b, w4, h, row(ln2_g), row(ln2_b), tiles.dense_tokens, tiles.dense_experts)


def kernel(x, w_in, w_gla_gate_up, b_gla_gate, gla_norm_g, ret_norm_g, w_out, ln1_g, ln1_b,
           w_peer_q, peer_subkeys, peer_u, peer_v, ln2_g, ln2_b):
    batch, seq, d = x.shape
    h = x.reshape(batch * seq, d)
    for l in range(DEPTH):
        h = _layer(h, batch, seq, w_in[l], w_gla_gate_up[l], b_gla_gate[l], gla_norm_g[l], ret_norm_g[l],
                   w_out[l], ln1_g[l], ln1_b[l], w_peer_q[l], peer_subkeys[l], peer_u[l], peer_v[l],
                   ln2_g[l], ln2_b[l])
    return h.reshape(batch, seq, d)
```

```python
import functools
import math
from typing import NamedTuple

import jax
import jax.numpy as jnp
from jax import lax
from jax.experimental import pallas as pl
from jax.experimental.pallas import tpu as pltpu

F32 = jnp.float32
BF16 = jnp.bfloat16

DEPTH = 1
CHUNK = 64
GLA_HEADS = 4
GLA_DK = 128
GLA_DV = 256
GLA_GATE_RANK = 16
GLA_TAU = 16.0
RET_HEADS = 4
RET_DK = 256
RET_DV = 256
ROPE_BASE = 10000.0
PEER_HEADS = 8
PEER_NKEYS = 128
PEER_TOPK = 16
LN_EPS = 1e-5
DEEPNORM_ALPHA = (2 * DEPTH) ** 0.25

LANES = 128
SUBLANES = 8
VMEM_LIMIT = 56 * 1024 * 1024
VMEM_LIMIT_DENSE = 61 * 1024 * 1024

_C_GQ = 0
_C_GK = _C_GQ + GLA_HEADS * GLA_DK
_C_GV = _C_GK + GLA_HEADS * GLA_DK
_C_GR = _C_GV + GLA_HEADS * GLA_DV
_C_RQ = _C_GR + GLA_HEADS * GLA_DV
_C_RK = _C_RQ + RET_HEADS * RET_DK
_C_RV = _C_RK + RET_HEADS * RET_DK
_C_RR = _C_RV + RET_HEADS * RET_DV
_C_END = _C_RR + RET_HEADS * RET_DV

_NT = (((1,), (1,)), ((), ()))
_TN = (((0,), (0,)), ((), ()))


class _Tiles(NamedTuple):
    gate_tokens: int
    proj_tokens: int
    proj_cols: int
    mixer_chunks: int
    out_tokens: int
    route_tokens: int
    map_tokens: int
    dense_tokens: int
    dense_experts: int


def _tiles(t, seq):
    return _Tiles(gate_tokens=min(1024, t), proj_tokens=min(2048, t), proj_cols=1024,
                  mixer_chunks=min(8, seq // CHUNK), out_tokens=min(512, t), route_tokens=min(256, t),
                  map_tokens=min(256, t), dense_tokens=min(1024, t), dense_experts=1024)


def _params(*sem):
    return pltpu.CompilerParams(dimension_semantics=sem, vmem_limit_bytes=VMEM_LIMIT)


_GATE_ROWS = 256


def _gate_kernel(x_ref, wl_ref, wu_ref, b_ref, o_ref, xb_ref, *, n_split):
    tm = x_ref.shape[0] // n_split
    r = lax.broadcasted_iota(jnp.int32, (tm, tm), 0)
    c = lax.broadcasted_iota(jnp.int32, (tm, tm), 1)
    tri = jnp.where((r >= c) & ((r // CHUNK) == (c // CHUNK)), 1.0, 0.0).astype(F32)
    wl = wl_ref[...].astype(BF16)
    for s in range(n_split):
        rows = slice(s * tm, (s + 1) * tm)
        xb = x_ref[rows, :].astype(BF16)
        xb_ref[rows, :] = xb
        glr = lax.dot_general(xb, wl, _NT, preferred_element_type=F32)
        lane = lax.broadcasted_iota(jnp.int32, glr.shape, 1)
        glr = jnp.where(lane < GLA_GATE_RANK, glr, 0.0)
        z = jnp.dot(glr.astype(BF16), wu_ref[...], preferred_element_type=F32) + b_ref[...]
        log_a = (jnp.minimum(z, 0.0) - jnp.log1p(jnp.exp(-jnp.abs(z)))) / GLA_TAU
        o_ref[rows, :] = jnp.dot(tri, log_a, preferred_element_type=F32, precision=lax.Precision.HIGHEST)


def _gate_call(x, w_t, w_up, b_gate, tm):
    t, d = x.shape
    n = w_up.shape[1]
    return pl.pallas_call(
        functools.partial(_gate_kernel, n_split=max(tm // _GATE_ROWS, 1)),
        grid=(t // tm,),
        in_specs=[pl.BlockSpec((tm, d), lambda i: (i, 0)),
                  pl.BlockSpec((LANES, d), lambda i: (_C_RQ // LANES, 0)),
                  pl.BlockSpec(w_up.shape, lambda i: (0, 0)),
                  pl.BlockSpec((1, n), lambda i: (0, 0))],
        out_specs=[pl.BlockSpec((tm, n), lambda i: (i, 0)),
                   pl.BlockSpec((tm, d), lambda i: (i, 0))],
        out_shape=[jax.ShapeDtypeStruct((t, n), F32), jax.ShapeDtypeStruct((t, d), BF16)],
        compiler_params=_params("parallel"),
        name="gla_gate",
    )(x, w_t, w_up, b_gate)


def _proj_kernel(x_ref, w_ref, o_ref):
    o_ref[...] = lax.dot_general(x_ref[...], w_ref[...].astype(BF16), _NT,
                                 preferred_element_type=F32).astype(o_ref.dtype)


def _proj_call(xb, w_t, tm, tn):
    m, k = xb.shape
    n_gla_tiles = _C_RQ // tn
    assert n_gla_tiles * tn == _C_RQ
    w_row = lambda i, j: (pl.multiple_of(j * tn + jnp.where(j >= n_gla_tiles, GLA_GATE_RANK, 0), GLA_GATE_RANK), 0)
    return pl.pallas_call(
        _proj_kernel,
        grid=(m // tm, _C_END // tn),
        in_specs=[pl.BlockSpec((tm, k), lambda i, j: (i, 0)),
                  pl.BlockSpec((pl.Element(tn), pl.Element(k)), w_row)],
        out_specs=pl.BlockSpec((tm, tn), lambda i, j: (i, j)),
        out_shape=jax.ShapeDtypeStruct((m, _C_END), BF16),
        compiler_params=_params("parallel", "parallel"),
        name="proj_matmul",
    )(xb, w_t)


def _silu(x):
    return x * jax.nn.sigmoid(x)


def _mixer_kernel(p_ref, bc_ref, cos_ref, sin_ref, gg_ref, rg_ref, o_ref, sg_ref, sr_ref, *, n_chunks):
    @pl.when(pl.program_id(1) == 0)
    def _():
        sg_ref[...] = jnp.zeros_like(sg_ref)
        sr_ref[...] = jnp.zeros_like(sr_ref)

    ri = lax.broadcasted_iota(jnp.int32, (CHUNK, CHUNK), 0)
    ci = lax.broadcasted_iota(jnp.int32, (CHUNK, CHUNK), 1)
    lower = ri >= ci
    dist = jnp.abs(ri - ci).astype(F32)
    row = lax.broadcasted_iota(jnp.int32, (CHUNK, RET_DK), 0).astype(F32)
    mid = CHUNK // 2

    def chunk_body(c, carry):
        rows = pl.ds(pl.multiple_of(c * CHUNK, CHUNK), CHUNK)

        def col(ref, base, h, width):
            return ref[rows, base + h * width: base + (h + 1) * width]

        for h in range(GLA_HEADS):
            q = col(p_ref, _C_GQ, h, GLA_DK).astype(F32) * (GLA_DK ** -0.5)
            k = col(p_ref, _C_GK, h, GLA_DK).astype(F32)
            v = col(p_ref, _C_GV, h, GLA_DV)
            gate = col(p_ref, _C_GR, h, GLA_DV).astype(F32)
            bc = col(bc_ref, 0, h, GLA_DK)
            b_last = bc[CHUNK - 1:CHUNK, :]
            b_mid = bc[mid:mid + 1, :]
            e_fwd = jnp.exp(bc - b_mid)
            e_bwd = jnp.exp(b_mid - bc)
            st = sg_ref[h]
            o = lax.dot_general((q * jnp.exp(bc)).astype(BF16), st.astype(BF16), _NT,
                                preferred_element_type=F32)
            a_lo = lax.dot_general((q * e_fwd).astype(BF16), (k * e_bwd).astype(BF16), _NT,
                                   preferred_element_type=F32)
            a_up = lax.dot_general((q * e_bwd).astype(BF16), (k * e_fwd).astype(BF16), _NT,
                                   preferred_element_type=F32)
            a = jnp.where(lower, a_lo, a_up)
            o = o + jnp.dot(a.astype(BF16), v, preferred_element_type=F32)
            u_t = lax.dot_general(v, (k * jnp.exp(b_last - bc)).astype(BF16), _TN,
                                  preferred_element_type=F32)
            sg_ref[h] = st * jnp.exp(b_last) + u_t
            y = o * lax.rsqrt(jnp.mean(o * o, axis=-1, keepdims=True) + LN_EPS)
            y = y * gg_ref[:, h * GLA_DV:(h + 1) * GLA_DV] * _silu(gate)
            o_ref[rows, h * GLA_DV:(h + 1) * GLA_DV] = y.astype(o_ref.dtype)

        cos = cos_ref[rows, :]
        sin = sin_ref[rows, :]
        half = RET_DK // 2

        def rotary(t):
            t1, t2 = t[:, :half], t[:, half:]
            return jnp.concatenate([t1 * cos - t2 * sin, t1 * sin + t2 * cos], axis=-1)

        for h in range(RET_HEADS):
            log_gamma = math.log(1.0 - 2.0 ** (-5.0 - h))
            q = rotary(col(p_ref, _C_RQ, h, RET_DK).astype(F32))
            k = rotary(col(p_ref, _C_RK, h, RET_DK).astype(F32)) * (RET_DK ** -0.5)
            v = col(p_ref, _C_RV, h, RET_DV)
            gate = col(p_ref, _C_RR, h, RET_DV).astype(F32)
            qb = q.astype(BF16)
            st = sr_ref[h]
            o = lax.dot_general(qb, st.astype(BF16), _NT, preferred_element_type=F32)
            o = o * jnp.exp(log_gamma * (row + 1.0))
            a = lax.dot_general(qb, k.astype(BF16), _NT, preferred_element_type=F32)
            a = a * jnp.exp(log_gamma * dist)
            o = o + jnp.dot(a.astype(BF16), v, preferred_element_type=F32)
            k_dec = jnp.exp(log_gamma * (CHUNK - 1.0 - row))
            u_t = lax.dot_general(v, (k * k_dec).astype(BF16), _TN, preferred_element_type=F32)
            sr_ref[h] = st * math.exp(log_gamma * CHUNK) + u_t
            oc = o - jnp.mean(o, axis=-1, keepdims=True)
            y = oc * lax.rsqrt(jnp.mean(oc * oc, axis=-1, keepdims=True) + LN_EPS)
            y = y * rg_ref[:, h * RET_DV:(h + 1) * RET_DV] * _silu(gate)
            c0 = GLA_HEADS * GLA_DV + h * RET_DV
            o_ref[rows, c0:c0 + RET_DV] = y.astype(o_ref.dtype)
        return carry

    lax.fori_loop(0, n_chunks, chunk_body, 0, unroll=n_chunks)


def _mixer_call(proj, bc, cos, sin, gla_g, ret_g, batch, seq, n_chunks):
    t = proj.shape[0]
    tb = n_chunks * CHUNK
    groups = seq // tb
    width = GLA_HEADS * GLA_DV + RET_HEADS * RET_DV
    tok = lambda b, g: (b * groups + g, 0)
    return pl.pallas_call(
        functools.partial(_mixer_kernel, n_chunks=n_chunks),
        grid=(batch, groups),
        in_specs=[pl.BlockSpec((tb, proj.shape[1]), tok),
                  pl.BlockSpec((tb, bc.shape[1]), tok),
                  pl.BlockSpec((tb, cos.shape[1]), lambda b, g: (g, 0)),
                  pl.BlockSpec((tb, sin.shape[1]), lambda b, g: (g, 0)),
                  pl.BlockSpec(gla_g.shape, lambda b, g: (0, 0)),
                  pl.BlockSpec(ret_g.shape, lambda b, g: (0, 0))],
        out_specs=pl.BlockSpec((tb, width), tok),
        out_shape=jax.ShapeDtypeStruct((t, width), BF16),
        scratch_shapes=[pltpu.VMEM((GLA_HEADS, GLA_DV, GLA_DK), F32),
                        pltpu.VMEM((RET_HEADS, RET_DV, RET_DK), F32)],
        compiler_params=_params("parallel", "arbitrary"),
        name="mixer",
    )(proj, bc, cos, sin, gla_g, ret_g)


def _layer_norm(y, g, b):
    mu = jnp.mean(y, axis=-1, keepdims=True)
    yc = y - mu
    var = jnp.mean(yc * yc, axis=-1, keepdims=True)
    return yc * lax.rsqrt(var + LN_EPS) * g + b


def _out_ln_kernel(m_ref, w_ref, x_ref, g_ref, b_ref, h_ref, hb_ref):
    half = m_ref.shape[0] // 2
    for r in range(2):
        rows = slice(r * half, (r + 1) * half)
        mix = jnp.dot(m_ref[rows, :], w_ref[...], preferred_element_type=F32)
        h = _layer_norm(DEEPNORM_ALPHA * x_ref[rows, :] + mix, g_ref[...], b_ref[...])
        h_ref[rows, :] = h
        hb_ref[rows, :] = h.astype(BF16)


def _out_ln_call(mix, w_out, x, g, b, tm):
    t, d = x.shape
    return pl.pallas_call(
        _out_ln_kernel,
        grid=(t // tm,),
        in_specs=[pl.BlockSpec((tm, mix.shape[1]), lambda i: (i, 0)),
                  pl.BlockSpec(w_out.shape, lambda i: (0, 0)),
                  pl.BlockSpec((tm, d), lambda i: (i, 0)),
                  pl.BlockSpec((1, d), lambda i: (0, 0)),
                  pl.BlockSpec((1, d), lambda i: (0, 0))],
        out_specs=[pl.BlockSpec((tm, d), lambda i: (i, 0)),
                   pl.BlockSpec((tm, d), lambda i: (i, 0))],
        out_shape=[jax.ShapeDtypeStruct((t, d), F32), jax.ShapeDtypeStruct((t, d), BF16)],
        compiler_params=_params("parallel"),
        name="out_proj_ln",
    )(mix, w_out, x, g, b)


def _topk_paired(s, key, k):
    h = s.shape[0] // 2
    key_lo, key_hi = key[:h, :], key[h:, :]
    swap = s[h:, :] > s[:h, :]
    front = jnp.where(swap, s[h:, :], s[:h, :])
    back = jnp.where(swap, s[:h, :], s[h:, :])
    front_key = jnp.where(swap, key_hi, key_lo)
    back_key = jnp.where(swap, key_lo, key_hi)
    vals, keys = [], []
    for _ in range(k):
        m = jnp.max(front, axis=0, keepdims=True)
        pick = jnp.min(jnp.where(front == m, front_key, jnp.inf), axis=0, keepdims=True)
        hit = front_key == pick
        vals.append(m)
        keys.append(pick)
        front = jnp.where(hit, back, front)
        front_key = jnp.where(hit, back_key, front_key)
        back = jnp.where(hit, -jnp.inf, back)
    return jnp.concatenate(vals, axis=0), jnp.concatenate(keys, axis=0)


def _topk_quads(s, k):
    h = s.shape[0] // 4
    row = lax.broadcasted_iota(jnp.int32, (h, s.shape[1]), 0).astype(F32)
    v = [s[i * h:(i + 1) * h, :] for i in range(4)]
    key = [row + float(i * h) for i in range(4)]

    def exchange(i, j, static_order):
        if static_order:
            swap = v[j] > v[i]
        else:
            d = v[j] - v[i]
            swap = jnp.where(d == 0.0, key[i] - key[j], d) > 0.0
        v[i], v[j] = jnp.where(swap, v[j], v[i]), jnp.where(swap, v[i], v[j])
        key[i], key[j] = jnp.where(swap, key[j], key[i]), jnp.where(swap, key[i], key[j])

    exchange(0, 1, True)
    exchange(2, 3, True)
    exchange(0, 2, False)
    exchange(1, 3, False)
    exchange(1, 2, False)
    vals, keys = [], []
    for _ in range(k):
        m = jnp.max(v[0], axis=0, keepdims=True)
        pick = jnp.min(jnp.where(v[0] == m, key[0], jnp.inf), axis=0, keepdims=True)
        hit = key[0] == pick
        vals.append(m)
        keys.append(pick)
        for i in range(3):
            v[i] = jnp.where(hit, v[i + 1], v[i])
            key[i] = jnp.where(hit, key[i + 1], key[i])
        v[3] = jnp.where(hit, -jnp.inf, v[3])
    return jnp.concatenate(vals, axis=0), jnp.concatenate(keys, axis=0)


def _route_kernel(h_ref, wq_ref, sk_ref, u_ref, v_ref, a_ref, b_ref, bt_ref, g_ref, ub_ref, vb_ref):
    ub_ref[...] = u_ref[...].T.astype(BF16)
    vb_ref[...] = v_ref[...].astype(BF16)
    kk = PEER_TOPK
    q = jnp.dot(h_ref[...], wq_ref[...], preferred_element_type=F32).astype(BF16)
    n_tok = q.shape[0]
    n_exp = PEER_NKEYS * PEER_NKEYS
    n_cand = sum(kk // (r1 + 1) for r1 in range(kk))
    pad = (-n_cand) % (2 * SUBLANES)
    cand_row = lax.broadcasted_iota(jnp.int32, (n_cand + pad, n_tok), 0).astype(F32) * float(n_exp)
    experts, gates = [], []
    for h in range(PEER_HEADS):
        vals, idxs = [], []
        for p in range(2):
            j = 2 * h + p
            s_t = lax.dot_general(sk_ref[j], q[:, j * LANES:(j + 1) * LANES], _NT,
                                  preferred_element_type=F32)
            v_p, i_p = _topk_quads(s_t, kk)
            vals.append(v_p)
            idxs.append(i_p)
        cand, expert = [], []
        for r1 in range(kk):
            n2 = kk // (r1 + 1)
            cand.append(vals[0][r1:r1 + 1, :] + vals[1][:n2, :])
            expert.append(idxs[0][r1:r1 + 1, :] * float(PEER_NKEYS) + idxs[1][:n2, :])
        cand.append(jnp.full((pad, n_tok), -jnp.inf, F32))
        expert.append(jnp.zeros((pad, n_tok), F32))
        top_s, top_key = _topk_paired(jnp.concatenate(cand, axis=0),
                                      cand_row + jnp.concatenate(expert, axis=0), kk)
        e = jnp.exp(top_s - top_s[0:1, :])
        experts.append(top_key)
        gates.append(e / jnp.sum(e, axis=0, keepdims=True))
    top_e = jnp.concatenate(experts, axis=0).astype(jnp.int32) & (n_exp - 1)
    a_ref[...] = (top_e >> (PEER_NKEYS.bit_length() - 1)).T
    second = top_e & (PEER_NKEYS - 1)
    b_ref[...] = second.T
    bt_ref[...] = second
    g_ref[...] = jnp.concatenate(gates, axis=0).T


def _route_call(hb, w_q, subkeys, u_tab, v_tab, tt):
    t, d = hb.shape
    n_sel = PEER_HEADS * PEER_TOPK
    n_steps = t // tt
    n_exp, d_exp = u_tab.shape
    slab = n_exp // n_steps
    assert slab * n_steps == n_exp and slab % (2 * SUBLANES) == 0
    tok_major = pl.BlockSpec((tt, n_sel), lambda i: (i, 0))
    table = pl.BlockSpec((slab, d_exp), lambda i: (i, 0))
    return pl.pallas_call(
        _route_kernel,
        grid=(n_steps,),
        in_specs=[pl.BlockSpec((tt, d), lambda i: (i, 0)),
                  pl.BlockSpec(w_q.shape, lambda i: (0, 0)),
                  pl.BlockSpec(subkeys.shape, lambda i: (0, 0, 0)),
                  table, table],
        out_specs=[tok_major, tok_major, pl.BlockSpec((n_sel, tt), lambda i: (0, i)), tok_major,
                   pl.BlockSpec((d_exp, slab), lambda i: (0, i)), table],
        out_shape=[jax.ShapeDtypeStruct((t, n_sel), jnp.int32), jax.ShapeDtypeStruct((t, n_sel), jnp.int32),
                   jax.ShapeDtypeStruct((n_sel, t), jnp.int32), jax.ShapeDtypeStruct((t, n_sel), F32),
                   jax.ShapeDtypeStruct((d_exp, n_exp), BF16), jax.ShapeDtypeStruct(v_tab.shape, BF16)],
        compiler_params=_params("parallel"),
        name="peer_route",
    )(hb, w_q, subkeys, u_tab, v_tab)


_W_ROWS = PEER_NKEYS // 2


def _wbuild_kernel(a_ref, b_ref, bt_ref, g_ref, w_ref):
    n_tok, n_sel = a_ref.shape
    key_rows = lax.broadcasted_iota(jnp.int32, (PEER_NKEYS, n_sel), 0)
    key_cols = lax.broadcasted_iota(jnp.int32, (n_sel, PEER_NKEYS), 1)
    for t in range(n_tok):
        pa = jnp.where(key_rows == a_ref[t:t + 1, :], g_ref[t:t + 1, :], 0.0).astype(BF16)
        if t % 3:
            pb = jnp.where(key_cols == bt_ref[:, t:t + 1], 1.0, 0.0).astype(BF16)
            w = jnp.dot(pa, pb, preferred_element_type=F32)
        else:
            pb = jnp.where(key_rows == b_ref[t:t + 1, :], 1.0, 0.0).astype(BF16)
            w = lax.dot_general(pa, pb, _NT, preferred_element_type=F32)
        words = pltpu.pack_elementwise([w[:_W_ROWS, :], w[_W_ROWS:, :]], packed_dtype=BF16)
        w_ref[t // SUBLANES, pl.ds(t % SUBLANES, _W_ROWS, stride=SUBLANES), :] = (
            lax.bitcast_convert_type(words, jnp.uint32))


def _wbuild_call(a_tm, b_tm, b_sm, g_tm, tt):
    t, n_sel = a_tm.shape
    rows = _W_ROWS * SUBLANES
    tok_major = pl.BlockSpec((tt, n_sel), lambda i: (i, 0))
    return pl.pallas_call(
        _wbuild_kernel,
        grid=(t // tt,),
        in_specs=[tok_major, tok_major, pl.BlockSpec((n_sel, tt), lambda i: (0, i)), tok_major],
        out_specs=pl.BlockSpec((tt // SUBLANES, rows, PEER_NKEYS), lambda i: (i, 0, 0)),
        out_shape=jax.ShapeDtypeStruct((t // SUBLANES, rows, PEER_NKEYS), jnp.uint32),
        compiler_params=_params("parallel"),
        name="peer_gate_map",
    )(a_tm, b_tm, b_sm, g_tm)


def _gelu(x):
    return 0.5 * x * (1.0 + lax.erf(x * math.sqrt(0.5)))


def _peer_kernel(h_ref, ul_ref, uh_ref, vl_ref, vh_ref, w_ref, g_ref, b_ref, o_ref, hb_ref, gs_ref):
    j = pl.program_id(1)

    @pl.when(j == 0)
    def _():
        hb_ref[...] = h_ref[...].astype(BF16)
        o_ref[...] = jnp.zeros_like(o_ref)

    tm = hb_ref.shape[0]
    half = vl_ref.shape[0]
    hb = hb_ref[...]
    for part, u_ref in enumerate((ul_ref, uh_ref)):
        act = _gelu(jnp.dot(hb, u_ref[...], preferred_element_type=F32))
        for a in range(half // PEER_NKEYS):
            word = w_ref[:, a, :, :].reshape(tm, PEER_NKEYS)
            w_a = pltpu.unpack_elementwise(word, index=part, packed_dtype=BF16, unpacked_dtype=F32)
            cols = slice(a * PEER_NKEYS, (a + 1) * PEER_NKEYS)
            gs_ref[part, :, cols] = (w_a * act[:, cols]).astype(BF16)
    o_ref[...] += (jnp.dot(gs_ref[0], vl_ref[...], preferred_element_type=F32)
                   + jnp.dot(gs_ref[1], vh_ref[...], preferred_element_type=F32))

    @pl.when(j == pl.num_programs(1) - 1)
    def _():
        o_ref[...] = _layer_norm(DEEPNORM_ALPHA * h_ref[...] + o_ref[...], g_ref[...], b_ref[...])


def _peer_call(u_t, v_tab, w4, h, g, b, tm, tn):
    t, d = h.shape
    n_exp = v_tab.shape[0]
    half = tn // 2
    n_steps = n_exp // tn
    lo = pl.BlockSpec((half, d), lambda i, j: (j, 0))
    hi = pl.BlockSpec((half, d), lambda i, j: (j + n_steps, 0))
    tok = pl.BlockSpec((tm, d), lambda i, j: (i, 0))
    vec = pl.BlockSpec((1, d), lambda i, j: (0, 0))
    return pl.pallas_call(
        _peer_kernel,
        grid=(t // tm, n_steps),
        in_specs=[pl.BlockSpec((tm, d), lambda i, j: (i, 0), pipeline_mode=pl.Buffered(1)),
                  pl.BlockSpec((d, half), lambda i, j: (0, j)),
                  pl.BlockSpec((d, half), lambda i, j: (0, j + n_steps)), lo, hi,
                  pl.BlockSpec((tm // SUBLANES, half // PEER_NKEYS, SUBLANES, PEER_NKEYS),
                               lambda i, j: (i, j, 0, 0)),
                  vec, vec],
        out_specs=pl.BlockSpec((tm, d), lambda i, j: (i, 0), pipeline_mode=pl.Buffered(1)),
        out_shape=jax.ShapeDtypeStruct((t, d), F32),
        scratch_shapes=[pltpu.VMEM((tm, d), BF16), pltpu.VMEM((2, tm, half), BF16)],
        compiler_params=pltpu.CompilerParams(dimension_semantics=("parallel", "arbitrary"),
                                             vmem_limit_bytes=VMEM_LIMIT_DENSE),
        name="peer_dense",
    )(h, u_t, u_t, v_tab, v_tab, w4, g, b)


def _layer(x2, batch, seq, w_in, w_gate_up, b_gate, gla_g, ret_g, w_out, ln1_g, ln1_b,
           w_q, subkeys, u_tab, v_tab, ln2_g, ln2_b):
    t, d = x2.shape
    tiles = _tiles(t, seq)
    row = lambda p: p.reshape(1, -1).astype(F32)

    w_up = jnp.pad(w_gate_up, ((0, LANES - GLA_GATE_RANK), (0, 0))).astype(BF16)
    w_t = jnp.swapaxes(w_in, 0, 1)
    bc, xb = _gate_call(x2, w_t, w_up, row(b_gate), tiles.gate_tokens)
    proj = _proj_call(xb, w_t, tiles.proj_tokens, tiles.proj_cols)

    half = RET_DK // 2
    inv = ROPE_BASE ** (-jnp.arange(half, dtype=F32) / half)
    ang = jnp.arange(seq).astype(F32)[:, None] * inv[None, :]
    mix = _mixer_call(proj, bc, jnp.cos(ang), jnp.sin(ang), row(gla_g), row(ret_g), batch, seq,
                      tiles.mixer_chunks)

    h, hb = _out_ln_call(mix, w_out.astype(BF16), x2, row(ln1_g), row(ln1_b), tiles.out_tokens)

    sk = subkeys.reshape(PEER_HEADS * 2, PEER_NKEYS, -1).astype(BF16)
    a_sel, b_sel, bt_sel, gates, u_b, v_b = _route_call(hb, w_q.astype(BF16), sk, u_tab, v_tab,
                                                        tiles.route_tokens)
    w3 = _wbuild_call(a_sel, b_sel, bt_sel, gates, tiles.map_tokens)
    w4 = w3.reshape(t // SUBLANES, _W_ROWS, SUBLANES, PEER_NKEYS)

    return _peer_call(u_b, v_b, w4, h, row(ln2_g), row(ln2_b), tiles.dense_tokens, tiles.dense_experts)


def kernel(x, w_in, w_gla_gate_up, b_gla_gate, gla_norm_g, ret_norm_g, w_out, ln1_g, ln1_b,
           w_peer_q, peer_subkeys, peer_u, peer_v, ln2_g, ln2_b):
    batch, seq, d = x.shape
    h = x.reshape(batch * seq, d)
    for l in range(DEPTH):
        h = _layer(h, batch, seq, w_in[l], w_gla_gate_up[l], b_gla_gate[l], gla_norm_g[l], ret_norm_g[l],
                   w_out[l], ln1_g[l], ln1_b[l], w_peer_q[l], peer_subkeys[l], peer_u[l], peer_v[l],
                   ln2_g[l], ln2_b[l])
    return h.reshape(batch, seq, d)
```

```python
import functools
import math
from typing import NamedTuple

import jax
import jax.numpy as jnp
from jax import lax
from jax.experimental import pallas as pl
from jax.experimental.pallas import tpu as pltpu

F32 = jnp.float32
BF16 = jnp.bfloat16

DEPTH = 1
CHUNK = 64
GLA_HEADS = 4
GLA_DK = 128
GLA_DV = 256
GLA_GATE_RANK = 16
GLA_TAU = 16.0
RET_HEADS = 4
RET_DK = 256
RET_DV = 256
ROPE_BASE = 10000.0
PEER_HEADS = 8
PEER_NKEYS = 128
PEER_TOPK = 16
LN_EPS = 1e-5
DEEPNORM_ALPHA = (2 * DEPTH) ** 0.25

LANES = 128
SUBLANES = 8
VMEM_LIMIT = 56 * 1024 * 1024
VMEM_LIMIT_DENSE = 61 * 1024 * 1024

_C_GQ = 0
_C_GK = _C_GQ + GLA_HEADS * GLA_DK
_C_GV = _C_GK + GLA_HEADS * GLA_DK
_C_GR = _C_GV + GLA_HEADS * GLA_DV
_C_RQ = _C_GR + GLA_HEADS * GLA_DV
_C_RK = _C_RQ + RET_HEADS * RET_DK
_C_RV = _C_RK + RET_HEADS * RET_DK
_C_RR = _C_RV + RET_HEADS * RET_DV
_C_END = _C_RR + RET_HEADS * RET_DV

_NT = (((1,), (1,)), ((), ()))
_TN = (((0,), (0,)), ((), ()))


class _Tiles(NamedTuple):
    gate_tokens: int
    proj_tokens: int
    proj_cols: int
    mixer_chunks: int
    out_tokens: int
    route_tokens: int
    map_tokens: int
    dense_tokens: int
    dense_experts: int


def _tiles(t, seq):
    return _Tiles(gate_tokens=min(1024, t), proj_tokens=min(2048, t), proj_cols=1024,
                  mixer_chunks=min(8, seq // CHUNK), out_tokens=min(512, t), route_tokens=min(256, t),
                  map_tokens=min(256, t), dense_tokens=min(1024, t), dense_experts=1024)


def _params(*sem):
    return pltpu.CompilerParams(dimension_semantics=sem, vmem_limit_bytes=VMEM_LIMIT)


_GATE_ROWS = 256


def _gate_kernel(x_ref, wl_ref, wu_ref, b_ref, o_ref, xb_ref, *, n_split):
    tm = x_ref.shape[0] // n_split
    r = lax.broadcasted_iota(jnp.int32, (tm, tm), 0)
    c = lax.broadcasted_iota(jnp.int32, (tm, tm), 1)
    tri = jnp.where((r >= c) & ((r // CHUNK) == (c // CHUNK)), 1.0, 0.0).astype(F32)
    wl = wl_ref[...].astype(BF16)
    for s in range(n_split):
        rows = slice(s * tm, (s + 1) * tm)
        xb = x_ref[rows, :].astype(BF16)
        xb_ref[rows, :] = xb
        glr = lax.dot_general(xb, wl, _NT, preferred_element_type=F32)
        lane = lax.broadcasted_iota(jnp.int32, glr.shape, 1)
        glr = jnp.where(lane < GLA_GATE_RANK, glr, 0.0)
        z = jnp.dot(glr.astype(BF16), wu_ref[...], preferred_element_type=F32) + b_ref[...]
        log_a = (jnp.minimum(z, 0.0) - jnp.log1p(jnp.exp(-jnp.abs(z)))) / GLA_TAU
        o_ref[rows, :] = jnp.dot(tri, log_a, preferred_element_type=F32, precision=lax.Precision.HIGHEST)


def _gate_call(x, w_t, w_up, b_gate, tm):
    t, d = x.shape
    n = w_up.shape[1]
    return pl.pallas_call(
        functools.partial(_gate_kernel, n_split=max(tm // _GATE_ROWS, 1)),
        grid=(t // tm,),
        in_specs=[pl.BlockSpec((tm, d), lambda i: (i, 0)),
                  pl.BlockSpec((LANES, d), lambda i: (_C_RQ // LANES, 0)),
                  pl.BlockSpec(w_up.shape, lambda i: (0, 0)),
                  pl.BlockSpec((1, n), lambda i: (0, 0))],
        out_specs=[pl.BlockSpec((tm, n), lambda i: (i, 0)),
                   pl.BlockSpec((tm, d), lambda i: (i, 0))],
        out_shape=[jax.ShapeDtypeStruct((t, n), F32), jax.ShapeDtypeStruct((t, d), BF16)],
        compiler_params=_params("parallel"),
        name="gla_gate",
    )(x, w_t, w_up, b_gate)


def _proj_kernel(x_ref, w_ref, o_ref):
    o_ref[...] = lax.dot_general(x_ref[...], w_ref[...].astype(BF16), _NT,
                                 preferred_element_type=F32).astype(o_ref.dtype)


def _proj_call(xb, w_t, tm, tn):
    m, k = xb.shape
    n_gla_tiles = _C_RQ // tn
    assert n_gla_tiles * tn == _C_RQ
    w_row = lambda i, j: (pl.multiple_of(j * tn + jnp.where(j >= n_gla_tiles, GLA_GATE_RANK, 0), GLA_GATE_RANK), 0)
    return pl.pallas_call(
        _proj_kernel,
        grid=(m // tm, _C_END // tn),
        in_specs=[pl.BlockSpec((tm, k), lambda i, j: (i, 0)),
                  pl.BlockSpec((pl.Element(tn), pl.Element(k)), w_row)],
        out_specs=pl.BlockSpec((tm, tn), lambda i, j: (i, j)),
        out_shape=jax.ShapeDtypeStruct((m, _C_END), BF16),
        compiler_params=_params("parallel", "parallel"),
        name="proj_matmul",
    )(xb, w_t)


def _silu(x):
    return x * jax.nn.sigmoid(x)


def _mixer_kernel(p_ref, bc_ref, cos_ref, sin_ref, gg_ref, rg_ref, o_ref, sg_ref, sr_ref, *, n_chunks):
    @pl.when(pl.program_id(1) == 0)
    def _():
        sg_ref[...] = jnp.zeros_like(sg_ref)
        sr_ref[...] = jnp.zeros_like(sr_ref)

    ri = lax.broadcasted_iota(jnp.int32, (CHUNK, CHUNK), 0)
    ci = lax.broadcasted_iota(jnp.int32, (CHUNK, CHUNK), 1)
    lower = ri >= ci
    dist = jnp.abs(ri - ci).astype(F32)
    row = lax.broadcasted_iota(jnp.int32, (CHUNK, RET_DK), 0).astype(F32)
    mid = CHUNK // 2

    def chunk_body(c, carry):
        rows = pl.ds(pl.multiple_of(c * CHUNK, CHUNK), CHUNK)

        def col(ref, base, h, width):
            return ref[rows, base + h * width: base + (h + 1) * width]

        for h in range(GLA_HEADS):
            q = col(p_ref, _C_GQ, h, GLA_DK).astype(F32) * (GLA_DK ** -0.5)
            k = col(p_ref, _C_GK, h, GLA_DK).astype(F32)
            v = col(p_ref, _C_GV, h, GLA_DV)
            gate = col(p_ref, _C_GR, h, GLA_DV).astype(F32)
            bc = col(bc_ref, 0, h, GLA_DK)
            b_last = bc[CHUNK - 1:CHUNK, :]
            b_mid = bc[mid:mid + 1, :]
            e_fwd = jnp.exp(bc - b_mid)
            e_bwd = jnp.exp(b_mid - bc)
            st = sg_ref[h]
            o = lax.dot_general((q * jnp.exp(bc)).astype(BF16), st.astype(BF16), _NT,
                                preferred_element_type=F32)
            a_lo = lax.dot_general((q * e_fwd).astype(BF16), (k * e_bwd).astype(BF16), _NT,
                                   preferred_element_type=F32)
            a_up = lax.dot_general((q * e_bwd).astype(BF16), (k * e_fwd).astype(BF16), _NT,
                                   preferred_element_type=F32)
            a = jnp.where(lower, a_lo, a_up)
            o = o + jnp.dot(a.astype(BF16), v, preferred_element_type=F32)
            u_t = lax.dot_general(v, (k * jnp.exp(b_last - bc)).astype(BF16), _TN,
                                  preferred_element_type=F32)
            sg_ref[h] = st * jnp.exp(b_last) + u_t
            y = o * lax.rsqrt(jnp.mean(o * o, axis=-1, keepdims=True) + LN_EPS)
            y = y * gg_ref[:, h * GLA_DV:(h + 1) * GLA_DV] * _silu(gate)
            o_ref[rows, h * GLA_DV:(h + 1) * GLA_DV] = y.astype(o_ref.dtype)

        cos = cos_ref[rows, :]
        sin = sin_ref[rows, :]
        half = RET_DK // 2

        def rotary(t):
            t1, t2 = t[:, :half], t[:, half:]
            return jnp.concatenate([t1 * cos - t2 * sin, t1 * sin + t2 * cos], axis=-1)

        for h in range(RET_HEADS):
            log_gamma = math.log(1.0 - 2.0 ** (-5.0 - h))
            q = rotary(col(p_ref, _C_RQ, h, RET_DK).astype(F32))
            k = rotary(col(p_ref, _C_RK, h, RET_DK).astype(F32)) * (RET_DK ** -0.5)
            v = col(p_ref, _C_RV, h, RET_DV)
            gate = col(p_ref, _C_RR, h, RET_DV).astype(F32)
            qb = q.astype(BF16)
            st = sr_ref[h]
            o = lax.dot_general(qb, st.astype(BF16), _NT, preferred_element_type=F32)
            o = o * jnp.exp(log_gamma * (row + 1.0))
            a = lax.dot_general(qb, k.astype(BF16), _NT, preferred_element_type=F32)
            a = a * jnp.exp(log_gamma * dist)
            o = o + jnp.dot(a.astype(BF16), v, preferred_element_type=F32)
            k_dec = jnp.exp(log_gamma * (CHUNK - 1.0 - row))
            u_t = lax.dot_general(v, (k * k_dec).astype(BF16), _TN, preferred_element_type=F32)
            sr_ref[h] = st * math.exp(log_gamma * CHUNK) + u_t
            oc = o - jnp.mean(o, axis=-1, keepdims=True)
            y = oc * lax.rsqrt(jnp.mean(oc * oc, axis=-1, keepdims=True) + LN_EPS)
            y = y * rg_ref[:, h * RET_DV:(h + 1) * RET_DV] * _silu(gate)
            c0 = GLA_HEADS * GLA_DV + h * RET_DV
            o_ref[rows, c0:c0 + RET_DV] = y.astype(o_ref.dtype)
        return carry

    lax.fori_loop(0, n_chunks, chunk_body, 0, unroll=n_chunks)


def _mixer_call(proj, bc, cos, sin, gla_g, ret_g, batch, seq, n_chunks):
    t = proj.shape[0]
    tb = n_chunks * CHUNK
    groups = seq // tb
    width = GLA_HEADS * GLA_DV + RET_HEADS * RET_DV
    tok = lambda b, g: (b * groups + g, 0)
    return pl.pallas_call(
        functools.partial(_mixer_kernel, n_chunks=n_chunks),
        grid=(batch, groups),
        in_specs=[pl.BlockSpec((tb, proj.shape[1]), tok),
                  pl.BlockSpec((tb, bc.shape[1]), tok),
                  pl.BlockSpec((tb, cos.shape[1]), lambda b, g: (g, 0)),
                  pl.BlockSpec((tb, sin.shape[1]), lambda b, g: (g, 0)),
                  pl.BlockSpec(gla_g.shape, lambda b, g: (0, 0)),
                  pl.BlockSpec(ret_g.shape, lambda b, g: (0, 0))],
        out_specs=pl.BlockSpec((tb, width), tok),
        out_shape=jax.ShapeDtypeStruct((t, width), BF16),
        scratch_shapes=[pltpu.VMEM((GLA_HEADS, GLA_DV, GLA_DK), F32),
                        pltpu.VMEM((RET_HEADS, RET_DV, RET_DK), F32)],
        compiler_params=_params("parallel", "arbitrary"),
        name="mixer",
    )(proj, bc, cos, sin, gla_g, ret_g)


def _layer_norm(y, g, b):
    mu = jnp.mean(y, axis=-1, keepdims=True)
    yc = y - mu
    var = jnp.mean(yc * yc, axis=-1, keepdims=True)
    return yc * lax.rsqrt(var + LN_EPS) * g + b


def _out_ln_kernel(m_ref, w_ref, x_ref, g_ref, b_ref, h_ref, hb_ref):
    half = m_ref.shape[0] // 2
    for r in range(2):
        rows = slice(r * half, (r + 1) * half)
        mix = jnp.dot(m_ref[rows, :], w_ref[...], preferred_element_type=F32)
        h = _layer_norm(DEEPNORM_ALPHA * x_ref[rows, :] + mix, g_ref[...], b_ref[...])
        h_ref[rows, :] = h
        hb_ref[rows, :] = h.astype(BF16)


def _out_ln_call(mix, w_out, x, g, b, tm):
    t, d = x.shape
    return pl.pallas_call(
        _out_ln_kernel,
        grid=(t // tm,),
        in_specs=[pl.BlockSpec((tm, mix.shape[1]), lambda i: (i, 0)),
                  pl.BlockSpec(w_out.shape, lambda i: (0, 0)),
                  pl.BlockSpec((tm, d), lambda i: (i, 0)),
                  pl.BlockSpec((1, d), lambda i: (0, 0)),
                  pl.BlockSpec((1, d), lambda i: (0, 0))],
        out_specs=[pl.BlockSpec((tm, d), lambda i: (i, 0)),
                   pl.BlockSpec((tm, d), lambda i: (i, 0))],
        out_shape=[jax.ShapeDtypeStruct((t, d), F32), jax.ShapeDtypeStruct((t, d), BF16)],
        compiler_params=_params("parallel"),
        name="out_proj_ln",
    )(mix, w_out, x, g, b)


def _topk_paired(s, key, k):
    h = s.shape[0] // 2
    key_lo, key_hi = key[:h, :], key[h:, :]
    swap = s[h:, :] > s[:h, :]
    front = jnp.where(swap, s[h:, :], s[:h, :])
    back = jnp.where(swap, s[:h, :], s[h:, :])
    front_key = jnp.where(swap, key_hi, key_lo)
    back_key = jnp.where(swap, key_lo, key_hi)
    vals, keys = [], []
    for _ in range(k):
        m = jnp.max(front, axis=0, keepdims=True)
        pick = jnp.min(jnp.where(front == m, front_key, jnp.inf), axis=0, keepdims=True)
        hit = front_key == pick
        vals.append(m)
        keys.append(pick)
        front = jnp.where(hit, back, front)
        front_key = jnp.where(hit, back_key, front_key)
        back = jnp.where(hit, -jnp.inf, back)
    return jnp.concatenate(vals, axis=0), jnp.concatenate(keys, axis=0)


def _topk_quads(s, k):
    h = s.shape[0] // 4
    row = lax.broadcasted_iota(jnp.int32, (h, s.shape[1]), 0).astype(F32)
    v = [s[i * h:(i + 1) * h, :] for i in range(4)]
    key = [row + float(i * h) for i in range(4)]

    def exchange(i, j, static_order):
        if static_order:
            swap = v[j] > v[i]
        else:
            d = v[j] - v[i]
            swap = jnp.where(d == 0.0, key[i] - key[j], d) > 0.0
        v[i], v[j] = jnp.where(swap, v[j], v[i]), jnp.where(swap, v[i], v[j])
        key[i], key[j] = jnp.where(swap, key[j], key[i]), jnp.where(swap, key[i], key[j])

    exchange(0, 1, True)
    exchange(2, 3, True)
    exchange(0, 2, False)
    exchange(1, 3, False)
    exchange(1, 2, False)
    vals, keys = [], []
    for _ in range(k):
        m = jnp.max(v[0], axis=0, keepdims=True)
        pick = jnp.min(jnp.where(v[0] == m, key[0], jnp.inf), axis=0, keepdims=True)
        hit = key[0] == pick
        vals.append(m)
        keys.append(pick)
        for i in range(3):
            v[i] = jnp.where(hit, v[i + 1], v[i])
            key[i] = jnp.where(hit, key[i + 1], key[i])
        v[3] = jnp.where(hit, -jnp.inf, v[3])
    return jnp.concatenate(vals, axis=0), jnp.concatenate(keys, axis=0)


def _route_kernel(h_ref, wq_ref, sk_ref, u_ref, v_ref, a_ref, b_ref, bt_ref, g_ref, ub_ref, vb_ref):
    ub_ref[...] = u_ref[...].T.astype(BF16)
    vb_ref[...] = v_ref[...].astype(BF16)
    kk = PEER_TOPK
    q = jnp.dot(h_ref[...], wq_ref[...], preferred_element_type=F32).astype(BF16)
    n_tok = q.shape[0]
    n_exp = PEER_NKEYS * PEER_NKEYS
    n_cand = sum(kk // (r1 + 1) for r1 in range(kk))
    pad = (-n_cand) % (2 * SUBLANES)
    cand_row = lax.broadcasted_iota(jnp.int32, (n_cand + pad, n_tok), 0).astype(F32) * float(n_exp)
    experts, gates = [], []
    for h in range(PEER_HEADS):
        vals, idxs = [], []
        for p in range(2):
            j = 2 * h + p
            s_t = lax.dot_general(sk_ref[j], q[:, j * LANES:(j + 1) * LANES], _NT,
                                  preferred_element_type=F32)
            v_p, i_p = _topk_quads(s_t, kk)
            vals.append(v_p)
            idxs.append(i_p)
        cand, expert = [], []
        for r1 in range(kk):
            n2 = kk // (r1 + 1)
            cand.append(vals[0][r1:r1 + 1, :] + vals[1][:n2, :])
            expert.append(idxs[0][r1:r1 + 1, :] * float(PEER_NKEYS) + idxs[1][:n2, :])
        cand.append(jnp.full((pad, n_tok), -jnp.inf, F32))
        expert.append(jnp.zeros((pad, n_tok), F32))
        top_s, top_key = _topk_paired(jnp.concatenate(cand, axis=0),
                                      cand_row + jnp.concatenate(expert, axis=0), kk)
        e = jnp.exp(top_s - top_s[0:1, :])
        experts.append(top_key)
        gates.append(e / jnp.sum(e, axis=0, keepdims=True))
    top_e = jnp.concatenate(experts, axis=0).astype(jnp.int32) & (n_exp - 1)
    a_ref[...] = (top_e >> (PEER_NKEYS.bit_length() - 1)).T
    second = top_e & (PEER_NKEYS - 1)
    b_ref[...] = second.T
    bt_ref[...] = second
    g_ref[...] = jnp.concatenate(gates, axis=0).T


def _route_call(hb, w_q, subkeys, u_tab, v_tab, tt):
    t, d = hb.shape
    n_sel = PEER_HEADS * PEER_TOPK
    n_steps = t // tt
    n_exp, d_exp = u_tab.shape
    slab = n_exp // n_steps
    assert slab * n_steps == n_exp and slab % (2 * SUBLANES) == 0
    tok_major = pl.BlockSpec((tt, n_sel), lambda i: (i, 0))
    table = pl.BlockSpec((slab, d_exp), lambda i: (i, 0))
    return pl.pallas_call(
        _route_kernel,
        grid=(n_steps,),
        in_specs=[pl.BlockSpec((tt, d), lambda i: (i, 0)),
                  pl.BlockSpec(w_q.shape, lambda i: (0, 0)),
                  pl.BlockSpec(subkeys.shape, lambda i: (0, 0, 0)),
                  table, table],
        out_specs=[tok_major, tok_major, pl.BlockSpec((n_sel, tt), lambda i: (0, i)), tok_major,
                   pl.BlockSpec((d_exp, slab), lambda i: (0, i)), table],
        out_shape=[jax.ShapeDtypeStruct((t, n_sel), jnp.int32), jax.ShapeDtypeStruct((t, n_sel), jnp.int32),
                   jax.ShapeDtypeStruct((n_sel, t), jnp.int32), jax.ShapeDtypeStruct((t, n_sel), F32),
                   jax.ShapeDtypeStruct((d_exp, n_exp), BF16), jax.ShapeDtypeStruct(v_tab.shape, BF16)],
        compiler_params=_params("parallel"),
        name="peer_route",
    )(hb, w_q, subkeys, u_tab, v_tab)


_W_ROWS = PEER_NKEYS // 2


def _wbuild_kernel(a_ref, b_ref, bt_ref, g_ref, w_ref):
    n_tok, n_sel = a_ref.shape
    key_rows = lax.broadcasted_iota(jnp.int32, (PEER_NKEYS, n_sel), 0)
    key_cols = lax.broadcasted_iota(jnp.int32, (n_sel, PEER_NKEYS), 1)
    for t in range(n_tok):
        pa = jnp.where(key_rows == a_ref[t:t + 1, :], g_ref[t:t + 1, :], 0.0).astype(BF16)
        if t % 3:
            pb = jnp.where(key_cols == bt_ref[:, t:t + 1], 1.0, 0.0).astype(BF16)
            w = jnp.dot(pa, pb, preferred_element_type=F32)
        else:
            pb = jnp.where(key_rows == b_ref[t:t + 1, :], 1.0, 0.0).astype(BF16)
            w = lax.dot_general(pa, pb, _NT, preferred_element_type=F32)
        words = pltpu.pack_elementwise([w[:_W_ROWS, :], w[_W_ROWS:, :]], packed_dtype=BF16)
        w_ref[t // SUBLANES, pl.ds(t % SUBLANES, _W_ROWS, stride=SUBLANES), :] = (
            lax.bitcast_convert_type(words, jnp.uint32))


def _wbuild_call(a_tm, b_tm, b_sm, g_tm, tt):
    t, n_sel = a_tm.shape
    rows = _W_ROWS * SUBLANES
    tok_major = pl.BlockSpec((tt, n_sel), lambda i: (i, 0))
    return pl.pallas_call(
        _wbuild_kernel,
        grid=(t // tt,),
        in_specs=[tok_major, tok_major, pl.BlockSpec((n_sel, tt), lambda i: (0, i)), tok_major],
        out_specs=pl.BlockSpec((tt // SUBLANES, rows, PEER_NKEYS), lambda i: (i, 0, 0)),
        out_shape=jax.ShapeDtypeStruct((t // SUBLANES, rows, PEER_NKEYS), jnp.uint32),
        compiler_params=_params("parallel"),
        name="peer_gate_map",
    )(a_tm, b_tm, b_sm, g_tm)


def _gelu(x):
    return 0.5 * x * (1.0 + lax.erf(x * math.sqrt(0.5)))


def _peer_kernel(hb_ref, h_hbm, ul_ref, uh_ref, vl_ref, vh_ref, w_ref, g_ref, b_ref, o_ref, h_buf, h_sem, gs_ref):
    i = pl.program_id(0)
    j = pl.program_id(1)
    tm = hb_ref.shape[0]

    def residual_copy():
        rows = pl.ds(pl.multiple_of(i * tm, tm), tm)
        return pltpu.make_async_copy(h_hbm.at[rows, :], h_buf, h_sem)

    @pl.when(j == 0)
    def _():
        residual_copy().start()
        o_ref[...] = jnp.zeros_like(o_ref)

    half = vl_ref.shape[0]
    hb = hb_ref[...]
    for part, u_ref in enumerate((ul_ref, uh_ref)):
        act = _gelu(jnp.dot(hb, u_ref[...], preferred_element_type=F32))
        for a in range(half // PEER_NKEYS):
            word = w_ref[:, a, :, :].reshape(tm, PEER_NKEYS)
            w_a = pltpu.unpack_elementwise(word, index=part, packed_dtype=BF16, unpacked_dtype=F32)
            cols = slice(a * PEER_NKEYS, (a + 1) * PEER_NKEYS)
            gs_ref[part, :, cols] = (w_a * act[:, cols]).astype(BF16)
    o_ref[...] += (jnp.dot(gs_ref[0], vl_ref[...], preferred_element_type=F32)
                   + jnp.dot(gs_ref[1], vh_ref[...], preferred_element_type=F32))

    @pl.when(j == pl.num_programs(1) - 1)
    def _():
        residual_copy().wait()
        o_ref[...] = _layer_norm(DEEPNORM_ALPHA * h_buf[...] + o_ref[...], g_ref[...], b_ref[...])


def _peer_call(hb, u_t, v_tab, w4, h, g, b, tm, tn):
    t, d = h.shape
    n_exp = v_tab.shape[0]
    half = tn // 2
    n_steps = n_exp // tn
    lo = pl.BlockSpec((half, d), lambda i, j: (j, 0))
    hi = pl.BlockSpec((half, d), lambda i, j: (j + n_steps, 0))
    tok = pl.BlockSpec((tm, d), lambda i, j: (i, 0))
    vec = pl.BlockSpec((1, d), lambda i, j: (0, 0))
    return pl.pallas_call(
        _peer_kernel,
        grid=(t // tm, n_steps),
        in_specs=[pl.BlockSpec((tm, d), lambda i, j: (i, 0), pipeline_mode=pl.Buffered(1)),
                  pl.BlockSpec(memory_space=pl.ANY),
                  pl.BlockSpec((d, half), lambda i, j: (0, j)),
                  pl.BlockSpec((d, half), lambda i, j: (0, j + n_steps)), lo, hi,
                  pl.BlockSpec((tm // SUBLANES, half // PEER_NKEYS, SUBLANES, PEER_NKEYS),
                               lambda i, j: (i, j, 0, 0)),
                  vec, vec],
        out_specs=pl.BlockSpec((tm, d), lambda i, j: (i, 0), pipeline_mode=pl.Buffered(1)),
        out_shape=jax.ShapeDtypeStruct((t, d), F32),
        scratch_shapes=[pltpu.VMEM((tm, d), F32), pltpu.SemaphoreType.DMA(()), pltpu.VMEM((2, tm, half), BF16)],
        compiler_params=pltpu.CompilerParams(dimension_semantics=("arbitrary", "arbitrary"),
                                             vmem_limit_bytes=VMEM_LIMIT_DENSE),
        name="peer_dense",
    )(hb, h, u_t, u_t, v_tab, v_tab, w4, g, b)


def _layer(x2, batch, seq, w_in, w_gate_up, b_gate, gla_g, ret_g, w_out, ln1_g, ln1_b,
           w_q, subkeys, u_tab, v_tab, ln2_g, ln2_b):
    t, d = x2.shape
    tiles = _tiles(t, seq)
    row = lambda p: p.reshape(1, -1).astype(F32)

    w_up = jnp.pad(w_gate_up, ((0, LANES - GLA_GATE_RANK), (0, 0))).astype(BF16)
    w_t = jnp.swapaxes(w_in, 0, 1)
    bc, xb = _gate_call(x2, w_t, w_up, row(b_gate), tiles.gate_tokens)
    proj = _proj_call(xb, w_t, tiles.proj_tokens, tiles.proj_cols)

    half = RET_DK // 2
    inv = ROPE_BASE ** (-jnp.arange(half, dtype=F32) / half)
    ang = jnp.arange(seq).astype(F32)[:, None] * inv[None, :]
    mix = _mixer_call(proj, bc, jnp.cos(ang), jnp.sin(ang), row(gla_g), row(ret_g), batch, seq,
                      tiles.mixer_chunks)

    h, hb = _out_ln_call(mix, w_out.astype(BF16), x2, row(ln1_g), row(ln1_b), tiles.out_tokens)

    sk = subkeys.reshape(PEER_HEADS * 2, PEER_NKEYS, -1).astype(BF16)
    a_sel, b_sel, bt_sel, gates, u_b, v_b = _route_call(hb, w_q.astype(BF16), sk, u_tab, v_tab,
                                                        tiles.route_tokens)
    w3 = _wbuild_call(a_sel, b_sel, bt_sel, gates, tiles.map_tokens)
    w4 = w3.reshape(t // SUBLANES, _W_ROWS, SUBLANES, PEER_NKEYS)

    return _peer_call(hb, u_b, v_b, w4, h, row(ln2_g), row(ln2_b), tiles.dense_tokens, tiles.dense_experts)


def kernel(x, w_in, w_gla_gate_up, b_gla_gate, gla_norm_g, ret_norm_g, w_out, ln1_g, ln1_b,
           w_peer_q, peer_subkeys, peer_u, peer_v, ln2_g, ln2_b):
    batch, seq, d = x.shape
    h = x.reshape(batch * seq, d)
    for l in range(DEPTH):
        h = _layer(h, batch, seq, w_in[l], w_gla_gate_up[l], b_gla_gate[l], gla_norm_g[l], ret_norm_g[l],
                   w_out[l], ln1_g[l], ln1_b[l], w_peer_q[l], peer_subkeys[l], peer_u[l], peer_v[l],
                   ln2_g[l], ln2_b[l])
    return h.reshape(batch, seq, d)
```

```python
import functools
import math
from typing import NamedTuple

import jax
import jax.numpy as jnp
from jax import lax
from jax.experimental import pallas as pl
from jax.experimental.pallas import tpu as pltpu

F32 = jnp.float32
BF16 = jnp.bfloat16

DEPTH = 1
CHUNK = 64
GLA_HEADS = 4
GLA_DK = 128
GLA_DV = 256
GLA_GATE_RANK = 16
GLA_TAU = 16.0
RET_HEADS = 4
RET_DK = 256
RET_DV = 256
ROPE_BASE = 10000.0
PEER_HEADS = 8
PEER_NKEYS = 128
PEER_TOPK = 16
LN_EPS = 1e-5
DEEPNORM_ALPHA = (2 * DEPTH) ** 0.25

LANES = 128
SUBLANES = 8
VMEM_LIMIT = 56 * 1024 * 1024
VMEM_LIMIT_DENSE = 61 * 1024 * 1024

_C_GQ = 0
_C_GK = _C_GQ + GLA_HEADS * GLA_DK
_C_GV = _C_GK + GLA_HEADS * GLA_DK
_C_GR = _C_GV + GLA_HEADS * GLA_DV
_C_RQ = _C_GR + GLA_HEADS * GLA_DV
_C_RK = _C_RQ + RET_HEADS * RET_DK
_C_RV = _C_RK + RET_HEADS * RET_DK
_C_RR = _C_RV + RET_HEADS * RET_DV
_C_END = _C_RR + RET_HEADS * RET_DV

_NT = (((1,), (1,)), ((), ()))
_TN = (((0,), (0,)), ((), ()))


class _Tiles(NamedTuple):
    gate_tokens: int
    proj_tokens: int
    proj_cols: int
    mixer_chunks: int
    out_tokens: int
    route_tokens: int
    map_tokens: int
    dense_tokens: int
    dense_experts: int


def _tiles(t, seq):
    return _Tiles(gate_tokens=min(1024, t), proj_tokens=min(2048, t), proj_cols=1024,
                  mixer_chunks=min(8, seq // CHUNK), out_tokens=min(512, t), route_tokens=min(256, t),
                  map_tokens=min(256, t), dense_tokens=min(1024, t), dense_experts=1024)


def _params(*sem):
    return pltpu.CompilerParams(dimension_semantics=sem, vmem_limit_bytes=VMEM_LIMIT)


_GATE_ROWS = 256


def _gate_kernel(x_ref, wl_ref, wu_ref, b_ref, o_ref, xb_ref, *, n_split):
    tm = x_ref.shape[0] // n_split
    r = lax.broadcasted_iota(jnp.int32, (tm, tm), 0)
    c = lax.broadcasted_iota(jnp.int32, (tm, tm), 1)
    tri = jnp.where((r >= c) & ((r // CHUNK) == (c // CHUNK)), 1.0, 0.0).astype(F32)
    wl = wl_ref[...].astype(BF16)
    for s in range(n_split):
        rows = slice(s * tm, (s + 1) * tm)
        xb = x_ref[rows, :].astype(BF16)
        xb_ref[rows, :] = xb
        glr = lax.dot_general(xb, wl, _NT, preferred_element_type=F32)
        lane = lax.broadcasted_iota(jnp.int32, glr.shape, 1)
        glr = jnp.where(lane < GLA_GATE_RANK, glr, 0.0)
        z = jnp.dot(glr.astype(BF16), wu_ref[...], preferred_element_type=F32) + b_ref[...]
        log_a = (jnp.minimum(z, 0.0) - jnp.log1p(jnp.exp(-jnp.abs(z)))) / GLA_TAU
        o_ref[rows, :] = jnp.dot(tri, log_a, preferred_element_type=F32, precision=lax.Precision.HIGHEST)


def _gate_call(x, w_t, w_up, b_gate, tm):
    t, d = x.shape
    n = w_up.shape[1]
    return pl.pallas_call(
        functools.partial(_gate_kernel, n_split=max(tm // _GATE_ROWS, 1)),
        grid=(t // tm,),
        in_specs=[pl.BlockSpec((tm, d), lambda i: (i, 0)),
                  pl.BlockSpec((LANES, d), lambda i: (_C_RQ // LANES, 0)),
                  pl.BlockSpec(w_up.shape, lambda i: (0, 0)),
                  pl.BlockSpec((1, n), lambda i: (0, 0))],
        out_specs=[pl.BlockSpec((tm, n), lambda i: (i, 0)),
                   pl.BlockSpec((tm, d), lambda i: (i, 0))],
        out_shape=[jax.ShapeDtypeStruct((t, n), F32), jax.ShapeDtypeStruct((t, d), BF16)],
        compiler_params=_params("parallel"),
        name="gla_gate",
    )(x, w_t, w_up, b_gate)


def _proj_kernel(x_ref, w_ref, o_ref):
    o_ref[...] = lax.dot_general(x_ref[...], w_ref[...].astype(BF16), _NT,
                                 preferred_element_type=F32).astype(o_ref.dtype)


def _proj_call(xb, w_t, tm, tn):
    m, k = xb.shape
    n_gla_tiles = _C_RQ // tn
    assert n_gla_tiles * tn == _C_RQ
    w_row = lambda i, j: (pl.multiple_of(j * tn + jnp.where(j >= n_gla_tiles, GLA_GATE_RANK, 0), GLA_GATE_RANK), 0)
    return pl.pallas_call(
        _proj_kernel,
        grid=(m // tm, _C_END // tn),
        in_specs=[pl.BlockSpec((tm, k), lambda i, j: (i, 0)),
                  pl.BlockSpec((pl.Element(tn), pl.Element(k)), w_row)],
        out_specs=pl.BlockSpec((tm, tn), lambda i, j: (i, j)),
        out_shape=jax.ShapeDtypeStruct((m, _C_END), BF16),
        compiler_params=_params("parallel", "parallel"),
        name="proj_matmul",
    )(xb, w_t)


def _silu(x):
    return x * jax.nn.sigmoid(x)


def _mixer_kernel(p_ref, bc_ref, cos_ref, sin_ref, gg_ref, rg_ref, ws_ref, o_ref, wb_ref, sg_ref, sr_ref, *, n_chunks):
    wb_ref[...] = ws_ref[...].astype(BF16)
    @pl.when(pl.program_id(1) == 0)
    def _():
        sg_ref[...] = jnp.zeros_like(sg_ref)
        sr_ref[...] = jnp.zeros_like(sr_ref)

    ri = lax.broadcasted_iota(jnp.int32, (CHUNK, CHUNK), 0)
    ci = lax.broadcasted_iota(jnp.int32, (CHUNK, CHUNK), 1)
    lower = ri >= ci
    dist = jnp.abs(ri - ci).astype(F32)
    row = lax.broadcasted_iota(jnp.int32, (CHUNK, RET_DK), 0).astype(F32)
    mid = CHUNK // 2

    def chunk_body(c, carry):
        rows = pl.ds(pl.multiple_of(c * CHUNK, CHUNK), CHUNK)

        def col(ref, base, h, width):
            return ref[rows, base + h * width: base + (h + 1) * width]

        for h in range(GLA_HEADS):
            q = col(p_ref, _C_GQ, h, GLA_DK).astype(F32) * (GLA_DK ** -0.5)
            k = col(p_ref, _C_GK, h, GLA_DK).astype(F32)
            v = col(p_ref, _C_GV, h, GLA_DV)
            gate = col(p_ref, _C_GR, h, GLA_DV).astype(F32)
            bc = col(bc_ref, 0, h, GLA_DK)
            b_last = bc[CHUNK - 1:CHUNK, :]
            b_mid = bc[mid:mid + 1, :]
            e_fwd = jnp.exp(bc - b_mid)
            e_bwd = jnp.exp(b_mid - bc)
            st = sg_ref[h]
            o = lax.dot_general((q * jnp.exp(bc)).astype(BF16), st.astype(BF16), _NT,
                                preferred_element_type=F32)
            a_lo = lax.dot_general((q * e_fwd).astype(BF16), (k * e_bwd).astype(BF16), _NT,
                                   preferred_element_type=F32)
            a_up = lax.dot_general((q * e_bwd).astype(BF16), (k * e_fwd).astype(BF16), _NT,
                                   preferred_element_type=F32)
            a = jnp.where(lower, a_lo, a_up)
            o = o + jnp.dot(a.astype(BF16), v, preferred_element_type=F32)
            u_t = lax.dot_general(v, (k * jnp.exp(b_last - bc)).astype(BF16), _TN,
                                  preferred_element_type=F32)
            sg_ref[h] = st * jnp.exp(b_last) + u_t
            y = o * lax.rsqrt(jnp.mean(o * o, axis=-1, keepdims=True) + LN_EPS)
            y = y * gg_ref[:, h * GLA_DV:(h + 1) * GLA_DV] * _silu(gate)
            o_ref[rows, h * GLA_DV:(h + 1) * GLA_DV] = y.astype(o_ref.dtype)

        cos = cos_ref[rows, :]
        sin = sin_ref[rows, :]
        half = RET_DK // 2

        def rotary(t):
            t1, t2 = t[:, :half], t[:, half:]
            return jnp.concatenate([t1 * cos - t2 * sin, t1 * sin + t2 * cos], axis=-1)

        for h in range(RET_HEADS):
            log_gamma = math.log(1.0 - 2.0 ** (-5.0 - h))
            q = rotary(col(p_ref, _C_RQ, h, RET_DK).astype(F32))
            k = rotary(col(p_ref, _C_RK, h, RET_DK).astype(F32)) * (RET_DK ** -0.5)
            v = col(p_ref, _C_RV, h, RET_DV)
            gate = col(p_ref, _C_RR, h, RET_DV).astype(F32)
            qb = q.astype(BF16)
            st = sr_ref[h]
            o = lax.dot_general(qb, st.astype(BF16), _NT, preferred_element_type=F32)
            o = o * jnp.exp(log_gamma * (row + 1.0))
            a = lax.dot_general(qb, k.astype(BF16), _NT, preferred_element_type=F32)
            a = a * jnp.exp(log_gamma * dist)
            o = o + jnp.dot(a.astype(BF16), v, preferred_element_type=F32)
            k_dec = jnp.exp(log_gamma * (CHUNK - 1.0 - row))
            u_t = lax.dot_general(v, (k * k_dec).astype(BF16), _TN, preferred_element_type=F32)
            sr_ref[h] = st * math.exp(log_gamma * CHUNK) + u_t
            oc = o - jnp.mean(o, axis=-1, keepdims=True)
            y = oc * lax.rsqrt(jnp.mean(oc * oc, axis=-1, keepdims=True) + LN_EPS)
            y = y * rg_ref[:, h * RET_DV:(h + 1) * RET_DV] * _silu(gate)
            c0 = GLA_HEADS * GLA_DV + h * RET_DV
            o_ref[rows, c0:c0 + RET_DV] = y.astype(o_ref.dtype)
        return carry

    lax.fori_loop(0, n_chunks, chunk_body, 0, unroll=n_chunks)


def _mixer_call(proj, bc, cos, sin, gla_g, ret_g, w_next, batch, seq, n_chunks):
    t = proj.shape[0]
    tb = n_chunks * CHUNK
    groups = seq // tb
    width = GLA_HEADS * GLA_DV + RET_HEADS * RET_DV
    slab = w_next.shape[0] // (batch * groups)
    assert slab * batch * groups == w_next.shape[0] and slab % (2 * SUBLANES) == 0
    tok = lambda b, g: (b * groups + g, 0)
    w_slab = pl.BlockSpec((slab, w_next.shape[1]), tok)
    return pl.pallas_call(
        functools.partial(_mixer_kernel, n_chunks=n_chunks),
        grid=(batch, groups),
        in_specs=[pl.BlockSpec((tb, proj.shape[1]), tok),
                  pl.BlockSpec((tb, bc.shape[1]), tok),
                  pl.BlockSpec((tb, cos.shape[1]), lambda b, g: (g, 0)),
                  pl.BlockSpec((tb, sin.shape[1]), lambda b, g: (g, 0)),
                  pl.BlockSpec(gla_g.shape, lambda b, g: (0, 0)),
                  pl.BlockSpec(ret_g.shape, lambda b, g: (0, 0)),
                  w_slab],
        out_specs=[pl.BlockSpec((tb, width), tok), w_slab],
        out_shape=[jax.ShapeDtypeStruct((t, width), BF16), jax.ShapeDtypeStruct(w_next.shape, BF16)],
        scratch_shapes=[pltpu.VMEM((GLA_HEADS, GLA_DV, GLA_DK), F32),
                        pltpu.VMEM((RET_HEADS, RET_DV, RET_DK), F32)],
        compiler_params=_params("parallel", "arbitrary"),
        name="mixer",
    )(proj, bc, cos, sin, gla_g, ret_g, w_next)


def _layer_norm(y, g, b):
    mu = jnp.mean(y, axis=-1, keepdims=True)
    yc = y - mu
    var = jnp.mean(yc * yc, axis=-1, keepdims=True)
    return yc * lax.rsqrt(var + LN_EPS) * g + b


def _out_ln_kernel(m_ref, w_ref, x_ref, g_ref, b_ref, ws_ref, h_ref, hb_ref, wb_ref):
    wb_ref[...] = ws_ref[...].astype(BF16)
    half = m_ref.shape[0] // 2
    for r in range(2):
        rows = slice(r * half, (r + 1) * half)
        mix = jnp.dot(m_ref[rows, :], w_ref[...], preferred_element_type=F32)
        h = _layer_norm(DEEPNORM_ALPHA * x_ref[rows, :] + mix, g_ref[...], b_ref[...])
        h_ref[rows, :] = h
        hb_ref[rows, :] = h.astype(BF16)


def _out_ln_call(mix, w_out, x, g, b, w_next, tm):
    t, d = x.shape
    n_steps = t // tm
    slab = w_next.shape[0] // n_steps
    assert slab * n_steps == w_next.shape[0] and slab % (2 * SUBLANES) == 0
    w_slab = pl.BlockSpec((slab, w_next.shape[1]), lambda i: (i, 0))
    return pl.pallas_call(
        _out_ln_kernel,
        grid=(n_steps,),
        in_specs=[pl.BlockSpec((tm, mix.shape[1]), lambda i: (i, 0)),
                  pl.BlockSpec(w_out.shape, lambda i: (0, 0)),
                  pl.BlockSpec((tm, d), lambda i: (i, 0)),
                  pl.BlockSpec((1, d), lambda i: (0, 0)),
                  pl.BlockSpec((1, d), lambda i: (0, 0)),
                  w_slab],
        out_specs=[pl.BlockSpec((tm, d), lambda i: (i, 0)),
                   pl.BlockSpec((tm, d), lambda i: (i, 0)),
                   w_slab],
        out_shape=[jax.ShapeDtypeStruct((t, d), F32), jax.ShapeDtypeStruct((t, d), BF16),
                   jax.ShapeDtypeStruct(w_next.shape, BF16)],
        compiler_params=_params("parallel"),
        name="out_proj_ln",
    )(mix, w_out, x, g, b, w_next)


def _topk_paired(s, key, k):
    h = s.shape[0] // 2
    key_lo, key_hi = key[:h, :], key[h:, :]
    swap = s[h:, :] > s[:h, :]
    front = jnp.where(swap, s[h:, :], s[:h, :])
    back = jnp.where(swap, s[:h, :], s[h:, :])
    front_key = jnp.where(swap, key_hi, key_lo)
    back_key = jnp.where(swap, key_lo, key_hi)
    vals, keys = [], []
    for _ in range(k):
        m = jnp.max(front, axis=0, keepdims=True)
        pick = jnp.min(jnp.where(front == m, front_key, jnp.inf), axis=0, keepdims=True)
        hit = front_key == pick
        vals.append(m)
        keys.append(pick)
        front = jnp.where(hit, back, front)
        front_key = jnp.where(hit, back_key, front_key)
        back = jnp.where(hit, -jnp.inf, back)
    return jnp.concatenate(vals, axis=0), jnp.concatenate(keys, axis=0)


def _topk_quads(s, k):
    h = s.shape[0] // 4
    row = lax.broadcasted_iota(jnp.int32, (h, s.shape[1]), 0).astype(F32)
    v = [s[i * h:(i + 1) * h, :] for i in range(4)]
    key = [row + float(i * h) for i in range(4)]

    def exchange(i, j, static_order):
        if static_order:
            swap = v[j] > v[i]
        else:
            d = v[j] - v[i]
            swap = jnp.where(d == 0.0, key[i] - key[j], d) > 0.0
        v[i], v[j] = jnp.where(swap, v[j], v[i]), jnp.where(swap, v[i], v[j])
        key[i], key[j] = jnp.where(swap, key[j], key[i]), jnp.where(swap, key[i], key[j])

    exchange(0, 1, True)
    exchange(2, 3, True)
    exchange(0, 2, False)
    exchange(1, 3, False)
    exchange(1, 2, False)
    vals, keys = [], []
    for _ in range(k):
        m = jnp.max(v[0], axis=0, keepdims=True)
        pick = jnp.min(jnp.where(v[0] == m, key[0], jnp.inf), axis=0, keepdims=True)
        hit = key[0] == pick
        vals.append(m)
        keys.append(pick)
        for i in range(3):
            v[i] = jnp.where(hit, v[i + 1], v[i])
            key[i] = jnp.where(hit, key[i + 1], key[i])
        v[3] = jnp.where(hit, -jnp.inf, v[3])
    return jnp.concatenate(vals, axis=0), jnp.concatenate(keys, axis=0)


def _route_kernel(h_ref, wq_ref, sk_ref, u_ref, v_ref, a_ref, b_ref, bt_ref, g_ref, ub_ref, vb_ref):
    ub_ref[...] = u_ref[...].T.astype(BF16)
    vb_ref[...] = v_ref[...].astype(BF16)
    kk = PEER_TOPK
    q = jnp.dot(h_ref[...], wq_ref[...], preferred_element_type=F32).astype(BF16)
    n_tok = q.shape[0]
    n_exp = PEER_NKEYS * PEER_NKEYS
    n_cand = sum(kk // (r1 + 1) for r1 in range(kk))
    pad = (-n_cand) % (2 * SUBLANES)
    cand_row = lax.broadcasted_iota(jnp.int32, (n_cand + pad, n_tok), 0).astype(F32) * float(n_exp)
    experts, gates = [], []
    for h in range(PEER_HEADS):
        vals, idxs = [], []
        for p in range(2):
            j = 2 * h + p
            s_t = lax.dot_general(sk_ref[j], q[:, j * LANES:(j + 1) * LANES], _NT,
                                  preferred_element_type=F32)
            v_p, i_p = _topk_quads(s_t, kk)
            vals.append(v_p)
            idxs.append(i_p)
        cand, expert = [], []
        for r1 in range(kk):
            n2 = kk // (r1 + 1)
            cand.append(vals[0][r1:r1 + 1, :] + vals[1][:n2, :])
            expert.append(idxs[0][r1:r1 + 1, :] * float(PEER_NKEYS) + idxs[1][:n2, :])
        cand.append(jnp.full((pad, n_tok), -jnp.inf, F32))
        expert.append(jnp.zeros((pad, n_tok), F32))
        top_s, top_key = _topk_paired(jnp.concatenate(cand, axis=0),
                                      cand_row + jnp.concatenate(expert, axis=0), kk)
        e = jnp.exp(top_s - top_s[0:1, :])
        experts.append(top_key)
        gates.append(e / jnp.sum(e, axis=0, keepdims=True))
    top_e = jnp.concatenate(experts, axis=0).astype(jnp.int32) & (n_exp - 1)
    a_ref[...] = (top_e >> (PEER_NKEYS.bit_length() - 1)).T
    second = top_e & (PEER_NKEYS - 1)
    b_ref[...] = second.T
    bt_ref[...] = second
    g_ref[...] = jnp.concatenate(gates, axis=0).T


def _route_call(hb, w_q, subkeys, u_tab, v_tab, tt):
    t, d = hb.shape
    n_sel = PEER_HEADS * PEER_TOPK
    n_steps = t // tt
    n_exp, d_exp = u_tab.shape
    slab = n_exp // n_steps
    assert slab * n_steps == n_exp and slab % (2 * SUBLANES) == 0
    tok_major = pl.BlockSpec((tt, n_sel), lambda i: (i, 0))
    table = pl.BlockSpec((slab, d_exp), lambda i: (i, 0))
    return pl.pallas_call(
        _route_kernel,
        grid=(n_steps,),
        in_specs=[pl.BlockSpec((tt, d), lambda i: (i, 0)),
                  pl.BlockSpec(w_q.shape, lambda i: (0, 0)),
                  pl.BlockSpec(subkeys.shape, lambda i: (0, 0, 0)),
                  table, table],
        out_specs=[tok_major, tok_major, pl.BlockSpec((n_sel, tt), lambda i: (0, i)), tok_major,
                   pl.BlockSpec((d_exp, slab), lambda i: (0, i)), table],
        out_shape=[jax.ShapeDtypeStruct((t, n_sel), jnp.int32), jax.ShapeDtypeStruct((t, n_sel), jnp.int32),
                   jax.ShapeDtypeStruct((n_sel, t), jnp.int32), jax.ShapeDtypeStruct((t, n_sel), F32),
                   jax.ShapeDtypeStruct((d_exp, n_exp), BF16), jax.ShapeDtypeStruct(v_tab.shape, BF16)],
        compiler_params=_params("parallel"),
        name="peer_route",
    )(hb, w_q, subkeys, u_tab, v_tab)


_W_ROWS = PEER_NKEYS // 2


def _wbuild_kernel(a_ref, b_ref, bt_ref, g_ref, w_ref):
    n_tok, n_sel = a_ref.shape
    key_rows = lax.broadcasted_iota(jnp.int32, (PEER_NKEYS, n_sel), 0)
    key_cols = lax.broadcasted_iota(jnp.int32, (n_sel, PEER_NKEYS), 1)
    for t in range(n_tok):
        pa = jnp.where(key_rows == a_ref[t:t + 1, :], g_ref[t:t + 1, :], 0.0).astype(BF16)
        if t % 3:
            pb = jnp.where(key_cols == bt_ref[:, t:t + 1], 1.0, 0.0).astype(BF16)
            w = jnp.dot(pa, pb, preferred_element_type=F32)
        else:
            pb = jnp.where(key_rows == b_ref[t:t + 1, :], 1.0, 0.0).astype(BF16)
            w = lax.dot_general(pa, pb, _NT, preferred_element_type=F32)
        words = pltpu.pack_elementwise([w[:_W_ROWS, :], w[_W_ROWS:, :]], packed_dtype=BF16)
        w_ref[t // SUBLANES, pl.ds(t % SUBLANES, _W_ROWS, stride=SUBLANES), :] = (
            lax.bitcast_convert_type(words, jnp.uint32))


def _wbuild_call(a_tm, b_tm, b_sm, g_tm, tt):
    t, n_sel = a_tm.shape
    rows = _W_ROWS * SUBLANES
    tok_major = pl.BlockSpec((tt, n_sel), lambda i: (i, 0))
    return pl.pallas_call(
        _wbuild_kernel,
        grid=(t // tt,),
        in_specs=[tok_major, tok_major, pl.BlockSpec((n_sel, tt), lambda i: (0, i)), tok_major],
        out_specs=pl.BlockSpec((tt // SUBLANES, rows, PEER_NKEYS), lambda i: (i, 0, 0)),
        out_shape=jax.ShapeDtypeStruct((t // SUBLANES, rows, PEER_NKEYS), jnp.uint32),
        compiler_params=_params("parallel"),
        name="peer_gate_map",
    )(a_tm, b_tm, b_sm, g_tm)


def _gelu(x):
    return 0.5 * x * (1.0 + lax.erf(x * math.sqrt(0.5)))


def _peer_kernel(hb_ref, h_hbm, ul_ref, uh_ref, vl_ref, vh_ref, w_ref, g_ref, b_ref, o_ref, h_buf, h_sem, gs_ref):
    i = pl.program_id(0)
    j = pl.program_id(1)
    tm = hb_ref.shape[0]

    def residual_copy():
        rows = pl.ds(pl.multiple_of(i * tm, tm), tm)
        return pltpu.make_async_copy(h_hbm.at[rows, :], h_buf, h_sem)

    @pl.when(j == 0)
    def _():
        residual_copy().start()
        o_ref[...] = jnp.zeros_like(o_ref)

    half = vl_ref.shape[0]
    hb = hb_ref[...]
    for part, u_ref in enumerate((ul_ref, uh_ref)):
        act = _gelu(jnp.dot(hb, u_ref[...], preferred_element_type=F32))
        for a in range(half // PEER_NKEYS):
            word = w_ref[:, a, :, :].reshape(tm, PEER_NKEYS)
            w_a = pltpu.unpack_elementwise(word, index=part, packed_dtype=BF16, unpacked_dtype=F32)
            cols = slice(a * PEER_NKEYS, (a + 1) * PEER_NKEYS)
            gs_ref[part, :, cols] = (w_a * act[:, cols]).astype(BF16)
    o_ref[...] += (jnp.dot(gs_ref[0], vl_ref[...], preferred_element_type=F32)
                   + jnp.dot(gs_ref[1], vh_ref[...], preferred_element_type=F32))

    @pl.when(j == pl.num_programs(1) - 1)
    def _():
        residual_copy().wait()
        o_ref[...] = _layer_norm(DEEPNORM_ALPHA * h_buf[...] + o_ref[...], g_ref[...], b_ref[...])


def _peer_call(hb, u_t, v_tab, w4, h, g, b, tm, tn):
    t, d = h.shape
    n_exp = v_tab.shape[0]
    half = tn // 2
    n_steps = n_exp // tn
    lo = pl.BlockSpec((half, d), lambda i, j: (j, 0))
    hi = pl.BlockSpec((half, d), lambda i, j: (j + n_steps, 0))
    tok = pl.BlockSpec((tm, d), lambda i, j: (i, 0))
    vec = pl.BlockSpec((1, d), lambda i, j: (0, 0))
    return pl.pallas_call(
        _peer_kernel,
        grid=(t // tm, n_steps),
        in_specs=[pl.BlockSpec((tm, d), lambda i, j: (i, 0), pipeline_mode=pl.Buffered(1)),
                  pl.BlockSpec(memory_space=pl.ANY),
                  pl.BlockSpec((d, half), lambda i, j: (0, j)),
                  pl.BlockSpec((d, half), lambda i, j: (0, j + n_steps)), lo, hi,
                  pl.BlockSpec((tm // SUBLANES, half // PEER_NKEYS, SUBLANES, PEER_NKEYS),
                               lambda i, j: (i, j, 0, 0)),
                  vec, vec],
        out_specs=pl.BlockSpec((tm, d), lambda i, j: (i, 0), pipeline_mode=pl.Buffered(1)),
        out_shape=jax.ShapeDtypeStruct((t, d), F32),
        scratch_shapes=[pltpu.VMEM((tm, d), F32), pltpu.SemaphoreType.DMA(()), pltpu.VMEM((2, tm, half), BF16)],
        compiler_params=pltpu.CompilerParams(dimension_semantics=("arbitrary", "arbitrary"),
                                             vmem_limit_bytes=VMEM_LIMIT_DENSE),
        name="peer_dense",
    )(hb, h, u_t, u_t, v_tab, v_tab, w4, g, b)


def _layer(x2, batch, seq, w_in, w_gate_up, b_gate, gla_g, ret_g, w_out, ln1_g, ln1_b,
           w_q, subkeys, u_tab, v_tab, ln2_g, ln2_b):
    t, d = x2.shape
    tiles = _tiles(t, seq)
    row = lambda p: p.reshape(1, -1).astype(F32)

    w_up = jnp.pad(w_gate_up, ((0, LANES - GLA_GATE_RANK), (0, 0))).astype(BF16)
    w_t = jnp.swapaxes(w_in, 0, 1)
    bc, xb = _gate_call(x2, w_t, w_up, row(b_gate), tiles.gate_tokens)
    proj = _proj_call(xb, w_t, tiles.proj_tokens, tiles.proj_cols)

    half = RET_DK // 2
    inv = ROPE_BASE ** (-jnp.arange(half, dtype=F32) / half)
    ang = jnp.arange(seq).astype(F32)[:, None] * inv[None, :]
    mix, w_out_b = _mixer_call(proj, bc, jnp.cos(ang), jnp.sin(ang), row(gla_g), row(ret_g), w_out, batch, seq,
                               tiles.mixer_chunks)

    h, hb, w_q_b = _out_ln_call(mix, w_out_b, x2, row(ln1_g), row(ln1_b), w_q, tiles.out_tokens)

    sk = subkeys.reshape(PEER_HEADS * 2, PEER_NKEYS, -1).astype(BF16)
    a_sel, b_sel, bt_sel, gates, u_b, v_b = _route_call(hb, w_q_b, sk, u_tab, v_tab,
                                                        tiles.route_tokens)
    w3 = _wbuild_call(a_sel, b_sel, bt_sel, gates, tiles.map_tokens)
    w4 = w3.reshape(t // SUBLANES, _W_ROWS, SUBLANES, PEER_NKEYS)

    return _peer_call(hb, u_b, v_b, w4, h, row(ln2_g), row(ln2_b), tiles.dense_tokens, tiles.dense_experts)


def kernel(x, w_in, w_gla_gate_up, b_gla_gate, gla_norm_g, ret_norm_g, w_out, ln1_g, ln1_b,
           w_peer_q, peer_subkeys, peer_u, peer_v, ln2_g, ln2_b):
    batch, seq, d = x.shape
    h = x.reshape(batch * seq, d)
    for l in range(DEPTH):
        h = _layer(h, batch, seq, w_in[l], w_gla_gate_up[l], b_gla_gate[l], gla_norm_g[l], ret_norm_g[l],
                   w_out[l], ln1_g[l], ln1_b[l], w_peer_q[l], peer_subkeys[l], peer_u[l], peer_v[l],
                   ln2_g[l], ln2_b[l])
    return h.reshape(batch, seq, d)
```
